```python
import math
import jax, jax.numpy as jnp
from jax import lax
import numpy as np

D_MODEL = 1024
BATCH = 4
SEQ = 4096
DEPTH = 2
DEC_BATCH = 128
DEC_SEQ = 1
PAST_LEN = 2048
PAGE_SIZE = 128

N_META = 16
D_MIX = D_MODEL
HEAD_DIM = 64
D_A = D_MIX // 2
N_HEADS_A = D_A // HEAD_DIM
ROPE_THETA = 500000.0
N_IDX_HEADS = 8
D_IDX = 32
TOPK_MAX = 256
Q_BLOCK = 128
D_B = D_MIX // 4
POOL_WINDOWS = (2, 4, 8, 16)
N_POOL_GROUPS = 4
POOL_GROUP = D_B // N_POOL_GROUPS
POOL_STATE = 16 - 1
D_C = D_MIX - D_A - D_B
CONV_WIDTH = 31
CONV_STATE = CONV_WIDTH - 1
EPS = 1e-6
IN_SPLITS = (D_A, D_A, D_A, D_A, N_IDX_HEADS * D_IDX, N_IDX_HEADS, D_IDX, D_B, D_B, D_C, D_C, D_C)
N_IN = 4 * D_A + N_IDX_HEADS * D_IDX + N_IDX_HEADS + D_IDX + 2 * D_B + 3 * D_C

kernel_name = 'hybrid_dsa_pool_conformer_step'


def rms_norm(x, g):
    xf = x.astype(jnp.float32)
    y = xf * lax.rsqrt(jnp.mean(xf * xf, axis=-1, keepdims=True) + EPS)
    return (y * g.astype(jnp.float32)).astype(x.dtype)


def layer_norm(x, g, b):
    xf = x.astype(jnp.float32)
    mu = jnp.mean(xf, axis=-1, keepdims=True)
    xc = xf - mu
    var = jnp.mean(xc * xc, axis=-1, keepdims=True)
    y = xc * lax.rsqrt(var + EPS) * g.astype(jnp.float32) + b.astype(jnp.float32)
    return y.astype(x.dtype)


def partial_rope(x, pos):
    rot = x.shape[-1] // 4
    half = rot // 2
    inv = ROPE_THETA ** (-(jnp.arange(half, dtype=jnp.float32) / half))
    ang = pos.astype(jnp.float32)[:, None] * inv[None, :]
    cos = jnp.cos(ang)[None, :, None, :]
    sin = jnp.sin(ang)[None, :, None, :]
    x1 = x[..., :half].astype(jnp.float32)
    x2 = x[..., half:rot].astype(jnp.float32)
    r = jnp.concatenate([x1 * cos - x2 * sin, x2 * cos + x1 * sin], axis=-1).astype(x.dtype)
    return jnp.concatenate([r, x[..., rot:]], axis=-1)


def split_columns(z):
    parts, start = [], 0
    for width in IN_SPLITS:
        parts.append(z[..., start:start + width])
        start += width
    return parts


def project(x, pos, norm_g_l, w_in_l, q_norm_g_l, k_norm_g_l):
    b, t = x.shape[0], x.shape[1]
    h = rms_norm(x, norm_g_l)
    z = jnp.einsum('btd,dn->btn', h, w_in_l)
    q, k, v, ga, qi, wi, ki, ub, gb, cv, cg, gc = split_columns(z)
    q = partial_rope(rms_norm(q.reshape(b, t, N_HEADS_A, HEAD_DIM), q_norm_g_l), pos)
    k = partial_rope(rms_norm(k.reshape(b, t, N_HEADS_A, HEAD_DIM), k_norm_g_l), pos)
    v = v.reshape(b, t, N_HEADS_A, HEAD_DIM)
    qi = partial_rope(qi.reshape(b, t, N_IDX_HEADS, D_IDX), pos)
    ki = partial_rope(ki[:, :, None, :], pos)[:, :, 0]
    uc = cv * jax.nn.sigmoid(cg)
    return q, k, v, ga, qi, wi, ki, ub, gb, uc, gc


def indexer_scores(qi, wi, ki):
    dots = jnp.einsum('bqhd,bsd->bqhs', qi, ki).astype(jnp.float32) * (D_IDX ** -0.5)
    w = wi.astype(jnp.float32) * (N_IDX_HEADS ** -0.5)
    return jnp.einsum('bqh,bqhs->bqs', w, jax.nn.relu(dots))


def gathered_attention(q, k_sel, v_sel, valid):
    s = jnp.einsum('bqhd,bqkhd->bqhk', q, k_sel).astype(jnp.float32) * (HEAD_DIM ** -0.5)
    s = jnp.where(valid[:, :, None, :], s, -jnp.inf)
    p = jax.nn.softmax(s, axis=-1)
    return jnp.einsum('bqhk,bqkhd->bqhd', p.astype(v_sel.dtype), v_sel)


def prompt_sparse_attention(q, k, v, qi, wi, ki, top_k):
    b, t = q.shape[0], q.shape[1]
    n_blk = -(-t // Q_BLOCK)
    pad = n_blk * Q_BLOCK - t

    def blocks(a):
        a = jnp.pad(a, [(0, 0), (0, pad)] + [(0, 0)] * (a.ndim - 2))
        return jnp.swapaxes(a.reshape((b, n_blk, Q_BLOCK) + a.shape[2:]), 0, 1)

    key_pos = jnp.arange(t, dtype=jnp.int32)

    def one_block(args):
        i, q_b, qi_b, wi_b = args
        q_pos = i * Q_BLOCK + jnp.arange(Q_BLOCK, dtype=jnp.int32)
        sc = indexer_scores(qi_b, wi_b, ki)
        adm = key_pos[None, :] <= q_pos[:, None]
        sc = jnp.where(adm[None], sc, -jnp.inf)
        _, idx = lax.top_k(sc, top_k)
        valid = idx <= q_pos[None, :, None]
        k_sel = jax.vmap(lambda kk, ii: kk[ii])(k, idx)
        v_sel = jax.vmap(lambda vv, ii: vv[ii])(v, idx)
        return gathered_attention(q_b, k_sel, v_sel, valid)

    o = lax.map(one_block, (jnp.arange(n_blk, dtype=jnp.int32), blocks(q), blocks(qi), blocks(wi)))
    o = jnp.swapaxes(o, 0, 1).reshape(b, n_blk * Q_BLOCK, N_HEADS_A, HEAD_DIM)[:, :t]
    return o.reshape(b, t, D_A)


def sample_sparse_attention(q, k_new, v_new, qi, wi, ki_new, cache_k_l, cache_v_l, cache_ki_l, page_table, top_k):
    db, ts = q.shape[0], q.shape[1]
    ki_past = cache_ki_l[page_table].reshape(db, PAST_LEN, D_IDX)
    ki_all = jnp.concatenate([ki_past, ki_new], axis=1)
    sc = indexer_scores(qi, wi, ki_all)
    q_pos = PAST_LEN + jnp.arange(ts, dtype=jnp.int32)
    key_pos = jnp.arange(PAST_LEN + ts, dtype=jnp.int32)
    sc = jnp.where((key_pos[None, :] <= q_pos[:, None])[None], sc, -jnp.inf)
    _, idx = lax.top_k(sc, top_k)
    valid = idx <= q_pos[None, :, None]
    in_past = idx < PAST_LEN
    pidx = jnp.minimum(idx, PAST_LEN - 1)
    phys = jax.vmap(lambda pt, ii: pt[ii])(page_table, pidx // PAGE_SIZE)
    off = pidx % PAGE_SIZE
    nidx = jnp.clip(idx - PAST_LEN, 0, ts - 1)
    k_sel = jnp.where(in_past[..., None, None], cache_k_l[phys, off],
                      jax.vmap(lambda kk, ii: kk[ii])(k_new, nidx))
    v_sel = jnp.where(in_past[..., None, None], cache_v_l[phys, off],
                      jax.vmap(lambda vv, ii: vv[ii])(v_new, nidx))
    o = gathered_attention(q, k_sel, v_sel, valid)
    return o.reshape(db, ts, D_A)


def pool_branch(u, prev, pos, w_groups, scale):
    b, t = u.shape[0], u.shape[1]
    p = prev.shape[1]
    ext = jnp.concatenate([prev, u], axis=1)
    cs = jnp.pad(jnp.cumsum(ext.astype(jnp.float32), axis=1), ((0, 0), (1, 0), (0, 0)))
    upto = cs[:, p + 1:p + 1 + t]
    groups = []
    for g, win in enumerate(POOL_WINDOWS):
        lo, hi = g * POOL_GROUP, (g + 1) * POOL_GROUP
        before = cs[:, p + 1 - win:p + 1 - win + t, lo:hi]
        cnt = jnp.minimum(pos + 1, win).astype(jnp.float32)[None, :, None]
        groups.append((upto[..., lo:hi] - before) / cnt - u[..., lo:hi].astype(jnp.float32))
    d = jnp.stack(groups, axis=2).astype(u.dtype)
    y = jnp.einsum('btgi,gio->btgo', d, w_groups).reshape(b, t, D_B)
    return y * scale, ext[:, -POOL_STATE:]


def conv_branch(u, prev, w, bias, g, beta):
    ext = jnp.concatenate([prev, u], axis=1)
    y = lax.conv_general_dilated(ext, w[:, None, :], window_strides=(1,), padding='VALID',
                                 dimension_numbers=('NWC', 'WIO', 'NWC'), feature_group_count=D_C)
    y = jax.nn.silu(layer_norm(y + bias, g, beta))
    return y, ext[:, -CONV_STATE:]


def merge_groups(att, ga, pool, gb, conv, gc, w_out_l):
    y = jnp.concatenate([att * jax.nn.silu(ga), pool * jax.nn.silu(gb), conv * jax.nn.silu(gc)], axis=-1)
    return jnp.einsum('btm,md->btd', y, w_out_l)


def setup_inputs(seed: int = 0) -> dict:
    key = jax.random.key(seed)
    ks = jax.random.split(key, 20)
    n_pages = PAST_LEN // PAGE_SIZE
    n_used = DEC_BATCH * n_pages
    n_phys = n_used + n_used // 4
    f32 = jnp.float32
    nrm = lambda k, shape, s: jax.random.normal(k, shape, f32) * s
    page_table = jax.random.permutation(ks[0], n_phys)[:n_used].reshape(DEC_BATCH, n_pages).astype(jnp.int32)
    return {
        'x_prompt': nrm(ks[1], (BATCH, SEQ, D_MODEL), 1.0),
        'x_sample': nrm(ks[2], (DEC_BATCH, DEC_SEQ, D_MODEL), 1.0),
        'cache_k': nrm(ks[3], (DEPTH, n_phys, PAGE_SIZE, N_HEADS_A, HEAD_DIM), 1.0),
        'cache_v': nrm(ks[4], (DEPTH, n_phys, PAGE_SIZE, N_HEADS_A, HEAD_DIM), 1.0),
        'cache_kidx': nrm(ks[5], (DEPTH, n_phys, PAGE_SIZE, D_IDX), 1.0),
        'state_pool': nrm(ks[6], (DEPTH, DEC_BATCH, POOL_STATE, D_B), 1.0),
        'state_conv': nrm(ks[7], (DEPTH, DEC_BATCH, CONV_STATE, D_C), 1.0),
        'page_table': page_table,
        'meta_tokens': nrm(ks[8], (N_META, D_MODEL), 1.0),
        'norm_g': 1.0 + nrm(ks[9], (DEPTH, D_MODEL), 0.02),
        'w_in': nrm(ks[10], (DEPTH, D_MODEL, N_IN), D_MODEL ** -0.5),
        'q_norm_g': 1.0 + nrm(ks[11], (DEPTH, HEAD_DIM), 0.02),
        'k_norm_g': 1.0 + nrm(ks[12], (DEPTH, HEAD_DIM), 0.02),
        'pool_w': nrm(ks[13], (DEPTH, N_POOL_GROUPS, POOL_GROUP, POOL_GROUP), POOL_GROUP ** -0.5),
        'pool_scale': 1.0 + nrm(ks[14], (DEPTH, D_B), 0.02),
        'conv_w': nrm(ks[15], (DEPTH, CONV_WIDTH, D_C), CONV_WIDTH ** -0.5),
        'conv_b': nrm(ks[16], (DEPTH, D_C), 0.02),
        'conv_norm_g': 1.0 + nrm(ks[17], (DEPTH, D_C), 0.02),
        'conv_norm_b': nrm(ks[18], (DEPTH, D_C), 0.02),
        'w_out': nrm(ks[19], (DEPTH, D_MIX, D_MODEL), D_MIX ** -0.5),
    }


def reference(x_prompt, x_sample, cache_k, cache_v, cache_kidx, state_pool, state_conv, page_table,
              meta_tokens, norm_g, w_in, q_norm_g, k_norm_g, pool_w, pool_scale,
              conv_w, conv_b, conv_norm_g, conv_norm_b, w_out):
    b = x_prompt.shape[0]
    t = SEQ + N_META
    meta = jnp.broadcast_to(meta_tokens[None].astype(x_prompt.dtype), (b, N_META, D_MODEL))
    hp = jnp.concatenate([meta, x_prompt], axis=1)
    hs = x_sample
    pos_p = jnp.arange(t, dtype=jnp.int32)
    pos_s = PAST_LEN + jnp.arange(DEC_SEQ, dtype=jnp.int32)
    topk_p = min(TOPK_MAX, t // 4)
    topk_s = min(TOPK_MAX, (PAST_LEN + DEC_SEQ) // 4)
    kp, vp, kip, plp, cvp = [], [], [], [], []
    kss, vss, kis, pls, cvs = [], [], [], [], []
    for l in range(DEPTH):
        q, k, v, ga, qi, wi, ki, ub, gb, uc, gc = project(hp, pos_p, norm_g[l], w_in[l], q_norm_g[l], k_norm_g[l])
        att = prompt_sparse_attention(q, k, v, qi, wi, ki, topk_p)
        pool, pool_st = pool_branch(ub, jnp.zeros((b, POOL_STATE, D_B), ub.dtype), pos_p, pool_w[l], pool_scale[l])
        conv, conv_st = conv_branch(uc, jnp.zeros((b, CONV_STATE, D_C), uc.dtype), conv_w[l], conv_b[l],
                                    conv_norm_g[l], conv_norm_b[l])
        hp = hp + merge_groups(att, ga, pool, gb, conv, gc, w_out[l])
        kp.append(k); vp.append(v); kip.append(ki); plp.append(pool_st); cvp.append(conv_st)
        q, k, v, ga, qi, wi, ki, ub, gb, uc, gc = project(hs, pos_s, norm_g[l], w_in[l], q_norm_g[l], k_norm_g[l])
        att = sample_sparse_attention(q, k, v, qi, wi, ki, cache_k[l], cache_v[l], cache_kidx[l], page_table, topk_s)
        pool, pool_st = pool_branch(ub, state_pool[l], pos_s, pool_w[l], pool_scale[l])
        conv, conv_st = conv_branch(uc, state_conv[l], conv_w[l], conv_b[l], conv_norm_g[l], conv_norm_b[l])
        hs = hs + merge_groups(att, ga, pool, gb, conv, gc, w_out[l])
        kss.append(k); vss.append(v); kis.append(ki); pls.append(pool_st); cvs.append(conv_st)
    y_prompt = hp[:, N_META:]
    y_sample = hs
    new_k_prompt = jnp.stack(kp, axis=0)
    new_v_prompt = jnp.stack(vp, axis=0)
    new_kidx_prompt = jnp.stack(kip, axis=0)
    new_pool_prompt = jnp.stack(plp, axis=0)
    new_conv_prompt = jnp.stack(cvp, axis=0)
    new_k_sample = jnp.stack(kss, axis=0)
    new_v_sample = jnp.stack(vss, axis=0)
    new_kidx_sample = jnp.stack(kis, axis=0)
    new_pool_sample = jnp.stack(pls, axis=0)
    new_conv_sample = jnp.stack(cvs, axis=0)
    return (y_prompt, y_sample, new_k_prompt, new_v_prompt, new_kidx_prompt, new_pool_prompt, new_conv_prompt,
            new_k_sample, new_v_sample, new_kidx_sample, new_pool_sample, new_conv_sample)
```

```python
import functools

import jax
import jax.numpy as jnp
from jax import lax
from jax.experimental import pallas as pl
from jax.experimental.pallas import tpu as pltpu

F32 = jnp.float32
BF16 = jnp.bfloat16

LANES = 128
N_META = 16
HEAD_DIM = 64
N_HEADS = 8
D_A = N_HEADS * HEAD_DIM
N_IDX_HEADS = 8
D_IDX = 32
D_QI = N_IDX_HEADS * D_IDX
D_B = 256
D_C = 256
POOL_WINDOWS = (2, 4, 8, 16)
POOL_GROUP = D_B // len(POOL_WINDOWS)
POOL_STATE = 15
CONV_WIDTH = 31
CONV_STATE = CONV_WIDTH - 1
HALO = 32
TOPK_MAX = 256
ROPE_THETA = 500000.0
EPS = 1e-6
PAGE = 128
Q_SCALE = HEAD_DIM ** -0.5
IDX_SCALE = (D_IDX ** -0.5) * (N_IDX_HEADS ** -0.5)

QB = 128
CK = 384
TM = 384
BISECT_ITERS = 40
BIG_INDEX = 1.0e9
VMEM_LIMIT = 56 * 1024 * 1024

C_Q, C_K, C_V, C_GA, C_QI = 0, 512, 1024, 1536, 2048
C_UB, C_GB, C_CV, C_CG, C_GC, C_KI = 2304, 2560, 2816, 3072, 3328, 3584
N_PERM = C_KI + LANES
A_GA, A_UB, A_GB, A_UC, A_GC, N_AUX = 0, 512, 768, 1024, 1280, 1536
WK_KI, WK_WI = 0, D_IDX


def _sigmoid(x):
    return 1.0 / (1.0 + jnp.exp(-x))


def _silu(x):
    return x * _sigmoid(x)


def _proj_kernel(x_ref, g_ref, w_ref, rope_ref, qg_ref, kg_ref,
                 q_ref, k_ref, kT_ref, v_ref, vb_ref, aux_ref, qi_ref, wk_ref, kiT_ref,
                 t_scr):
    x = x_ref[...]
    ms = jnp.mean(x * x, axis=-1, keepdims=True)
    h = (x * lax.rsqrt(ms + EPS) * g_ref[...]).astype(BF16)

    def proj(c0, c1):
        return jnp.dot(h, w_ref[:, c0:c1], preferred_element_type=F32)

    cos8 = rope_ref[0:8, :]
    sin8 = rope_ref[8:16, :]
    cc4 = rope_ref[16:24, :]
    ss4 = rope_ref[24:32, :]

    zT = proj(C_Q, C_V).T
    for j in range(2 * N_HEADS):
        xh = zT[j * HEAD_DIM:(j + 1) * HEAD_DIM, :]
        g = qg_ref[...] if j < N_HEADS else kg_ref[...]
        xn = xh * lax.rsqrt(jnp.mean(xh * xh, axis=0, keepdims=True) + EPS) * g
        x1 = xn[0:8, :]
        x2 = xn[8:16, :]
        r0 = j * HEAD_DIM
        t_scr[r0:r0 + 8, :] = x1 * cos8 - x2 * sin8
        t_scr[r0 + 8:r0 + 16, :] = x2 * cos8 + x1 * sin8
        t_scr[r0 + 16:r0 + HEAD_DIM, :] = xn[16:HEAD_DIM, :]
    q_ref[...] = (t_scr[0:D_A, :].T * Q_SCALE).astype(BF16)
    kT = t_scr[D_A:2 * D_A, :]
    k_ref[...] = kT.T
    kT_ref[...] = kT.astype(BF16)

    zv = proj(C_V, C_GA)
    v_ref[...] = zv
    vb_ref[...] = zv.astype(BF16)

    aux_ref[:, A_GA:A_UB] = proj(C_GA, C_QI)
    aux_ref[:, A_UB:A_UC] = proj(C_UB, C_CV)
    zc = proj(C_CV, C_KI)
    aux_ref[:, A_UC:A_GC] = zc[:, 0:D_C] * _sigmoid(zc[:, D_C:2 * D_C])
    aux_ref[:, A_GC:N_AUX] = zc[:, 2 * D_C:3 * D_C]

    ziT = proj(C_QI, C_UB).T
    parts = []
    for hh in range(N_IDX_HEADS):
        r0 = hh * D_IDX
        x8 = ziT[r0:r0 + 8, :]
        parts.append(x8 * cc4 + pltpu.roll(x8, 4, 0) * ss4)
        parts.append(ziT[r0 + 8:r0 + D_IDX, :])
    qi_ref[...] = jnp.concatenate(parts, axis=0).T.astype(BF16)

    zwT = proj(C_KI, N_PERM).T
    x8 = zwT[0:8, :]
    ki = jnp.concatenate([x8 * cc4 + pltpu.roll(x8, 4, 0) * ss4, zwT[8:D_IDX, :]], axis=0)
    wi = zwT[D_IDX:D_IDX + N_IDX_HEADS, :] * IDX_SCALE
    wk_ref[...] = jnp.concatenate([ki, wi, zwT[D_IDX + N_IDX_HEADS:, :]], axis=0).T
    kiT_ref[...] = jnp.concatenate([ki] * (LANES // D_IDX), axis=0).astype(BF16)


def _project(x2d, n_batch, tm, g, w, rope, qg, kg):
    rows = x2d.shape[0]
    t = rows // n_batch
    nt = t // tm
    d = x2d.shape[1]
    row_map = lambda b, i: (b * nt + i, 0)
    const = lambda b, i: (0, 0)
    out_shape = (
        jax.ShapeDtypeStruct((rows, D_A), BF16),
        jax.ShapeDtypeStruct((rows, D_A), F32),
        jax.ShapeDtypeStruct((n_batch, D_A, t), BF16),
        jax.ShapeDtypeStruct((rows, D_A), F32),
        jax.ShapeDtypeStruct((rows, D_A), BF16),
        jax.ShapeDtypeStruct((rows, N_AUX), F32),
        jax.ShapeDtypeStruct((rows, D_QI), BF16),
        jax.ShapeDtypeStruct((rows, LANES), F32),
        jax.ShapeDtypeStruct((n_batch, LANES, t), BF16),
    )
    out_specs = (
        pl.BlockSpec((tm, D_A), row_map),
        pl.BlockSpec((tm, D_A), row_map),
        pl.BlockSpec((None, D_A, tm), lambda b, i: (b, 0, i)),
        pl.BlockSpec((tm, D_A), row_map),
        pl.BlockSpec((tm, D_A), row_map),
        pl.BlockSpec((tm, N_AUX), row_map),
        pl.BlockSpec((tm, D_QI), row_map),
        pl.BlockSpec((tm, LANES), row_map),
        pl.BlockSpec((None, LANES, tm), lambda b, i: (b, 0, i)),
    )
    in_specs = [
        pl.BlockSpec((tm, d), row_map),
        pl.BlockSpec((1, d), const),
        pl.BlockSpec((d, N_PERM), const),
        pl.BlockSpec((32, tm), lambda b, i: (0, i)),
        pl.BlockSpec((HEAD_DIM, 1), const),
        pl.BlockSpec((HEAD_DIM, 1), const),
    ]
    return pl.pallas_call(
        _proj_kernel,
        out_shape=out_shape,
        grid=(n_batch, nt),
        in_specs=in_specs,
        out_specs=out_specs,
        scratch_shapes=[pltpu.VMEM((2 * D_A, tm), F32)],
        compiler_params=pltpu.CompilerParams(
            dimension_semantics=("parallel", "parallel"), vmem_limit_bytes=VMEM_LIMIT),
        name="proj",
    )(x2d, g, w, rope, qg, kg)


def _chunk_reduce(i_scr, n_ch, ck, fn, init, combine):
    rows = i_scr.shape[0]

    def body(c, acc):
        for k in range(ck // LANES):
            col = pl.multiple_of(c * ck + k * LANES, LANES)
            acc = combine(acc, fn(i_scr[:, pl.ds(col, LANES)], col))
        return acc

    return lax.fori_loop(0, n_ch, body, jnp.full((rows, LANES), init, F32))


def _count_ge(i_scr, n_ch, ck, mid):
    midb = jnp.broadcast_to(mid, (i_scr.shape[0], LANES))
    acc = _chunk_reduce(i_scr, n_ch, ck, lambda blk, col: jnp.where(blk >= midb, 1.0, 0.0),
                        0.0, jnp.add)
    return jnp.sum(acc, axis=1, keepdims=True)


def _select_bounds(i_scr, n_ch, ck, top_k, trivial, jcut_scr):
    rows = i_scr.shape[0]
    kf = float(top_k)
    neg_inf = -jnp.inf
    pos_inf = jnp.inf

    mn = _chunk_reduce(i_scr, n_ch, ck, lambda blk, col: jnp.where(blk == neg_inf, pos_inf, blk),
                       pos_inf, jnp.minimum)
    mx = _chunk_reduce(i_scr, n_ch, ck, lambda blk, col: blk, neg_inf, jnp.maximum)
    row_min = jnp.min(mn, axis=1, keepdims=True)
    row_max = jnp.max(mx, axis=1, keepdims=True)

    c_top = _count_ge(i_scr, n_ch, ck, row_max)
    top_tie = jnp.logical_and(c_top >= kf, jnp.logical_not(trivial))
    top_exact = jnp.logical_and(c_top == kf, jnp.logical_not(trivial))
    lo0 = jnp.where(top_tie, row_max, row_min)
    hi0 = jnp.where(jnp.logical_or(top_tie, trivial), pos_inf, row_max)
    res0 = jnp.where(jnp.logical_or(trivial, top_exact), 1.0, 0.0)
    act0 = jnp.where(jnp.logical_or(trivial, top_tie), 0.0, 1.0)

    def bis(_, st):
        lo, hi, res, act = st
        mid = 0.5 * lo + 0.5 * hi
        is_act = act > 0.0
        ok = jnp.logical_and(is_act, jnp.logical_and(mid > lo, mid < hi))
        c = _count_ge(i_scr, n_ch, ck, mid)
        eq = jnp.logical_and(ok, c == kf)
        gt = jnp.logical_and(ok, c > kf)
        lt = jnp.logical_and(ok, c < kf)
        lo = jnp.where(jnp.logical_or(eq, gt), mid, lo)
        hi = jnp.where(eq, pos_inf, jnp.where(lt, mid, hi))
        res = jnp.where(eq, 1.0, res)
        act = jnp.where(jnp.logical_or(eq, jnp.logical_not(ok)), 0.0, act)
        return lo, hi, res, act

    lo, hi, res, _ = lax.fori_loop(0, BISECT_ITERS, bis, (lo0, hi0, res0, act0))

    jcut_scr[...] = jnp.full((rows, LANES), BIG_INDEX, F32)
    n_open = jnp.sum(1.0 - res)

    @pl.when(n_open > 0.0)
    def _():
        need = kf - _count_ge(i_scr, n_ch, ck, hi)
        lob = jnp.broadcast_to(lo, (rows, LANES))
        hib = jnp.broadcast_to(hi, (rows, LANES))
        lane = lax.broadcasted_iota(jnp.int32, (rows, LANES), 1)

        def count_band(j):
            jb = jnp.broadcast_to(j, (rows, LANES))

            def fn(blk, col):
                idx = (lane + col).astype(F32)
                in_band = jnp.logical_and(jnp.logical_and(blk >= lob, blk < hib), idx <= jb)
                return jnp.where(in_band, 1.0, 0.0)

            return jnp.sum(_chunk_reduce(i_scr, n_ch, ck, fn, 0.0, jnp.add), axis=1, keepdims=True)

        def jbis(_, st):
            jl, jh = st
            jm = jnp.floor((jl + jh) * 0.5)
            c = count_band(jm)
            enough = c >= need
            return jnp.where(enough, jl, jm), jnp.where(enough, jm, jh)

        width = i_scr.shape[1]
        n_it = max(1, (width + 1).bit_length())
        jl0 = jnp.full((rows, 1), -1.0, F32)
        jh0 = jnp.full((rows, 1), float(width - 1), F32)
        _, jh = lax.fori_loop(0, n_it, jbis, (jl0, jh0))
        jcut = jnp.where(res > 0.0, BIG_INDEX, jh)
        jcut_scr[...] = jnp.broadcast_to(jcut, (rows, LANES))

    return lo, hi, jcut_scr[:, 0:1]


def _selected(blk, col, lo, hi, jcut):
    lane = lax.broadcasted_iota(jnp.int32, blk.shape, 1)
    idx = (lane + col).astype(F32)
    return jnp.logical_or(blk >= hi, jnp.logical_and(blk >= lo, idx <= jcut))


def _attn_kernel(q_ref, qi_ref, wk_ref, kT_ref, kiT_ref, v_ref, o_ref,
                 i_scr, jcut_scr, m_scr, l_scr, acc_scr, *, t_real, top_k):
    i = pl.program_id(1)
    n_ch = (i * QB) // CK + 1
    q_pos = i * QB + lax.broadcasted_iota(jnp.int32, (QB, 1), 0)
    lane = lax.broadcasted_iota(jnp.int32, (QB, LANES), 1)

    qi = qi_ref[...]
    q = q_ref[...]
    qiz, qz = [], []
    per = LANES // D_IDX
    for h in range(N_IDX_HEADS):
        blk = qi[:, (h // per) * LANES:(h // per + 1) * LANES]
        qiz.append(jnp.where(lane // D_IDX == h % per, blk, jnp.zeros_like(blk)))
    for h in range(N_HEADS):
        blk = q[:, (h // 2) * LANES:(h // 2 + 1) * LANES]
        qz.append(jnp.where(lane // HEAD_DIM == h % 2, blk, jnp.zeros_like(blk)))
    w = wk_ref[:, WK_WI:WK_WI + N_IDX_HEADS]

    def score_chunk(c, carry):
        col = pl.multiple_of(c * CK, LANES)
        kc = kiT_ref[:, pl.ds(col, CK)]
        acc = jnp.zeros((QB, CK), F32)
        for h in range(N_IDX_HEADS):
            d = jnp.dot(qiz[h], kc, preferred_element_type=F32)
            acc = acc + w[:, h:h + 1] * jnp.maximum(d, 0.0)
        key_pos = col + lax.broadcasted_iota(jnp.int32, (QB, CK), 1)
        i_scr[:, pl.ds(col, CK)] = jnp.where(key_pos <= q_pos, acc, -jnp.inf)
        return carry

    lax.fori_loop(0, n_ch, score_chunk, 0)

    trivial = jnp.logical_or(q_pos + 1 <= top_k, q_pos >= t_real)
    lo, hi, jcut = _select_bounds(i_scr, n_ch, CK, top_k, trivial, jcut_scr)

    m_scr[...] = jnp.full(m_scr.shape, -jnp.inf, F32)
    l_scr[...] = jnp.zeros(l_scr.shape, F32)
    acc_scr[...] = jnp.zeros(acc_scr.shape, F32)

    def attend_chunk(c, carry):
        col = pl.multiple_of(c * CK, LANES)
        sel = _selected(i_scr[:, pl.ds(col, CK)], col, lo, hi, jcut)
        for h in range(N_HEADS):
            pr = h // 2
            kc = kT_ref[pr * LANES:(pr + 1) * LANES, pl.ds(col, CK)]
            s = jnp.dot(qz[h], kc, preferred_element_type=F32)
            s = jnp.where(sel, s, -jnp.inf)
            m_old = m_scr[h]
            m_new = jnp.maximum(m_old, jnp.max(s, axis=1, keepdims=True))
            m_safe = jnp.where(m_new == -jnp.inf, 0.0, m_new)
            alpha = jnp.exp(m_old - m_safe)
            p = jnp.exp(s - m_safe[:, 0:1])
            l_scr[h] = alpha * l_scr[h] + jnp.sum(p, axis=1, keepdims=True)
            vc = v_ref[pl.ds(col, CK), pr * LANES:(pr + 1) * LANES]
            acc_scr[h] = alpha * acc_scr[h] + jnp.dot(p.astype(BF16), vc, preferred_element_type=F32)
            m_scr[h] = m_new
        return carry

    lax.fori_loop(0, n_ch, attend_chunk, 0)

    for pr in range(N_HEADS // 2):
        o0 = acc_scr[2 * pr] / l_scr[2 * pr]
        o1 = acc_scr[2 * pr + 1] / l_scr[2 * pr + 1]
        o_ref[:, pr * LANES:(pr + 1) * LANES] = jnp.where(lane < HEAD_DIM, o0, o1)


def _prompt_attention(q, qi, wk, kT, kiT, vb, n_batch, t_pad, t_real, top_k):
    nq = t_pad // QB
    row_map = lambda b, i: (b * nq + i, 0)
    kern = functools.partial(_attn_kernel, t_real=t_real, top_k=top_k)
    return pl.pallas_call(
        kern,
        out_shape=jax.ShapeDtypeStruct((n_batch * t_pad, D_A), F32),
        grid=(n_batch, nq),
        in_specs=[
            pl.BlockSpec((QB, D_A), row_map),
            pl.BlockSpec((QB, D_QI), row_map),
            pl.BlockSpec((QB, LANES), row_map),
            pl.BlockSpec((None, D_A, t_pad), lambda b, i: (b, 0, 0)),
            pl.BlockSpec((None, LANES, t_pad), lambda b, i: (b, 0, 0)),
            pl.BlockSpec((t_pad, D_A), lambda b, i: (b, 0)),
        ],
        out_specs=pl.BlockSpec((QB, D_A), row_map),
        scratch_shapes=[
            pltpu.VMEM((QB, t_pad), F32),
            pltpu.VMEM((QB, LANES), F32),
            pltpu.VMEM((N_HEADS, QB, LANES), F32),
            pltpu.VMEM((N_HEADS, QB, LANES), F32),
            pltpu.VMEM((N_HEADS, QB, LANES), F32),
        ],
        compiler_params=pltpu.CompilerParams(
            dimension_semantics=("parallel", "arbitrary"), vmem_limit_bytes=VMEM_LIMIT),
        name="prompt_attn",
    )(q, qi, wk, kT, kiT, vb)


def _gate_project(att, aux_ref, pool, conv, wout_ref, resid):
    ya = att * _silu(aux_ref[:, A_GA:A_UB])
    yb = pool * _silu(aux_ref[:, A_GB:A_UC])
    yc = conv * _silu(aux_ref[:, A_GC:N_AUX])
    y = jnp.concatenate([ya, yb, yc], axis=1).astype(BF16)
    return resid + jnp.dot(y, wout_ref[...], preferred_element_type=F32)


def _pool_mix(sums, u, cnts, poolw_ref, pscale_ref):
    lane = lax.broadcasted_iota(jnp.int32, u.shape, 1)
    d = sums[-1] / cnts[-1]
    for g in range(len(POOL_WINDOWS) - 2, -1, -1):
        d = jnp.where(lane < (g + 1) * POOL_GROUP, sums[g] / cnts[g], d)
    d = (d - u).astype(BF16)
    return jnp.dot(d, poolw_ref[...], preferred_element_type=F32) * pscale_ref[...]


def _conv_norm(y, cb_ref, cg_ref, cbeta_ref):
    y = y + cb_ref[...]
    mu = jnp.mean(y, axis=-1, keepdims=True)
    yc = y - mu
    var = jnp.mean(yc * yc, axis=-1, keepdims=True)
    return _silu(yc * lax.rsqrt(var + EPS) * cg_ref[...] + cbeta_ref[...])


def _merge_kernel(att_ref, aux_ref, hub_ref, huc_ref, hp_ref, wout_ref, poolw_ref, pscale_ref,
                  cw_ref, cb_ref, cg_ref, cbeta_ref, o_ref, ub_scr, uc_scr):
    i = pl.program_id(1)
    tm = att_ref.shape[0]
    first = i == 0
    ub = aux_ref[:, A_UB:A_GB]
    uc = aux_ref[:, A_UC:A_GC]
    ub_scr[0:HALO, :] = jnp.where(first, 0.0, hub_ref[...])
    uc_scr[0:HALO, :] = jnp.where(first, 0.0, huc_ref[...])
    ub_scr[HALO:HALO + tm, :] = ub
    uc_scr[HALO:HALO + tm, :] = uc

    pos = i * tm + lax.broadcasted_iota(jnp.int32, (tm, 1), 0)
    run = ub
    sums, cnts = [], []
    for j in range(1, POOL_WINDOWS[-1]):
        run = run + ub_scr[HALO - j:HALO - j + tm, :]
        if j + 1 in POOL_WINDOWS:
            sums.append(run)
            cnts.append(jnp.minimum(pos + 1, j + 1).astype(F32))
    pool = _pool_mix(sums, ub, cnts, poolw_ref, pscale_ref)

    y = jnp.zeros((tm, D_C), F32)
    for j in range(CONV_WIDTH):
        r0 = HALO - CONV_STATE + j
        y = y + cw_ref[j:j + 1, :] * uc_scr[r0:r0 + tm, :]
    conv = _conv_norm(y, cb_ref, cg_ref, cbeta_ref)

    o_ref[...] = _gate_project(att_ref[...], aux_ref, pool, conv, wout_ref, hp_ref[...])


def _prompt_merge(att, aux, hp, wout, poolw, pscale, cw, cb, cg, cbeta, n_batch, t_pad):
    nt = t_pad // TM
    hb = TM // HALO
    row_map = lambda b, i: (b * nt + i, 0)
    const = lambda b, i: (0, 0)
    d = hp.shape[1]

    def halo_map(col_block):
        return lambda b, i: (jnp.maximum((b * nt + i) * hb - 1, 0), col_block)

    return pl.pallas_call(
        _merge_kernel,
        out_shape=jax.ShapeDtypeStruct(hp.shape, F32),
        grid=(n_batch, nt),
        in_specs=[
            pl.BlockSpec((TM, D_A), row_map),
            pl.BlockSpec((TM, N_AUX), row_map),
            pl.BlockSpec((HALO, D_B), halo_map(A_UB // D_B)),
            pl.BlockSpec((HALO, D_C), halo_map(A_UC // D_C)),
            pl.BlockSpec((TM, d), row_map),
            pl.BlockSpec(wout.shape, const),
            pl.BlockSpec(poolw.shape, const),
            pl.BlockSpec((1, D_B), const),
            pl.BlockSpec(cw.shape, const),
            pl.BlockSpec((1, D_C), const),
            pl.BlockSpec((1, D_C), const),
            pl.BlockSpec((1, D_C), const),
        ],
        out_specs=pl.BlockSpec((TM, d), row_map),
        scratch_shapes=[pltpu.VMEM((HALO + TM, D_B), F32), pltpu.VMEM((HALO + TM, D_C), F32)],
        compiler_params=pltpu.CompilerParams(
            dimension_semantics=("parallel", "parallel"), vmem_limit_bytes=VMEM_LIMIT),
        name="prompt_merge",
    )(att, aux, aux, aux, hp, wout, poolw, pscale, cw, cb, cg, cbeta)


def _sample_index_kernel(pt_ref, *refs, n_pages, top_k):
    del pt_ref
    ki_pages = refs[:n_pages]
    qi_ref, wi_ref, kin_ref, mask_ref, i_scr, jcut_scr = refs[n_pages:]
    b = pl.program_id(0)
    n_q = i_scr.shape[0]
    past = n_pages * PAGE
    qi = qi_ref[...]
    w = wi_ref[...]
    @pl.when(b == 0)
    def _():
        i_scr[...] = jnp.full(i_scr.shape, -jnp.inf, F32)

    row0 = pl.multiple_of((b // 8) * 8, 8)
    mine = lax.broadcasted_iota(jnp.int32, (8, LANES), 0) == b % 8

    def put_row(c0, row):
        grp = i_scr[pl.ds(row0, 8), c0:c0 + LANES]
        i_scr[pl.ds(row0, 8), c0:c0 + LANES] = jnp.where(mine, row, grp)

    for p in range(n_pages):
        kp = ki_pages[p][...].astype(BF16)
        d = lax.dot_general(qi, kp, (((1,), (1,)), ((), ())), preferred_element_type=F32)
        put_row(p * PAGE, jnp.sum(w * jnp.maximum(d, 0.0), axis=0, keepdims=True))
    dn = jnp.sum(qi.astype(F32) * kin_ref[...], axis=1, keepdims=True)
    sn = jnp.sum(w * jnp.maximum(dn, 0.0), axis=0, keepdims=True)
    lane1 = lax.broadcasted_iota(jnp.int32, (1, LANES), 1)
    put_row(past, jnp.where(lane1 == 0, sn, -jnp.inf))

    @pl.when(b == n_q - 1)
    def _():
        width = i_scr.shape[1]
        trivial = jnp.full((n_q, 1), past + 1 <= top_k)
        lo, hi, jcut = _select_bounds(i_scr, 1, width, top_k, trivial, jcut_scr)
        for k in range(width // LANES):
            sel = _selected(i_scr[:, k * LANES:(k + 1) * LANES], k * LANES, lo, hi, jcut)
            mask_ref[:, k * LANES:(k + 1) * LANES] = jnp.where(sel, 1.0, 0.0)


def _sample_index(layer, page_table, cache_kidx, qi3, wi3, kin3, top_k):
    n_q, n_pages = page_table.shape
    width = (n_pages + 1) * PAGE
    kern = functools.partial(_sample_index_kernel, n_pages=n_pages, top_k=top_k)

    def page_map(p):
        return lambda b, pt: (layer, pt[b, p], 0, 0)

    in_specs = [pl.BlockSpec((None, None, PAGE, D_IDX), page_map(p)) for p in range(n_pages)]
    in_specs += [
        pl.BlockSpec((None, N_IDX_HEADS, D_IDX), lambda b, pt: (b, 0, 0)),
        pl.BlockSpec((None, N_IDX_HEADS, 1), lambda b, pt: (b, 0, 0)),
        pl.BlockSpec((None, 1, D_IDX), lambda b, pt: (b, 0, 0)),
    ]
    return pl.pallas_call(
        kern,
        out_shape=jax.ShapeDtypeStruct((n_q, width), F32),
        grid_spec=pltpu.PrefetchScalarGridSpec(
            num_scalar_prefetch=1,
            grid=(n_q,),
            in_specs=in_specs,
            out_specs=pl.BlockSpec((n_q, width), lambda b, pt: (0, 0)),
            scratch_shapes=[pltpu.VMEM((n_q, width), F32), pltpu.VMEM((n_q, LANES), F32)],
        ),
        compiler_params=pltpu.CompilerParams(
            dimension_semantics=("arbitrary",), vmem_limit_bytes=VMEM_LIMIT),
        name="sample_index",
    )(page_table, *([cache_kidx] * n_pages), qi3, wi3, kin3)


def _sample_attn_kernel(pt_ref, *refs, pg):
    del pt_ref
    k_pages = refs[:pg]
    v_pages = refs[pg:2 * pg]
    (q_ref, kn_ref, vn_ref, mask_ref, spread_ref, o_ref, m_scr, l_scr, acc_scr) = refs[2 * pg:]
    j = pl.program_id(1)
    n_j = pl.num_programs(1)
    q = q_ref[...].astype(F32)
    ones_d = jnp.ones((HEAD_DIM, LANES), BF16)
    ones_l = jnp.ones((LANES, LANES), BF16)

    @pl.when(j == 0)
    def _():
        m_scr[...] = jnp.full(m_scr.shape, -jnp.inf, F32)
        l_scr[...] = jnp.zeros(l_scr.shape, F32)
        acc_scr[...] = jnp.zeros(acc_scr.shape, F32)

    def update(s3, v3):
        m_old = m_scr[...]
        m_new = jnp.maximum(m_old, jnp.max(s3, axis=0))
        m_safe = jnp.where(m_new == -jnp.inf, 0.0, m_new)
        alpha = jnp.exp(m_old - m_safe)
        p3 = jnp.exp(s3 - m_safe[None])
        l_scr[...] = alpha * l_scr[...] + jnp.sum(p3, axis=0)
        acc_scr[...] = alpha[:, 0:HEAD_DIM] * acc_scr[...] + jnp.sum(p3[:, :, 0:HEAD_DIM] * v3, axis=0)
        m_scr[...] = m_new

    for p in range(pg):
        k3 = k_pages[p][...]
        prod = (k3 * q[None]).reshape(PAGE * N_HEADS, HEAD_DIM).astype(BF16)
        s = jnp.dot(prod, ones_d, preferred_element_type=F32)
        mrow = mask_ref[pl.ds(j * pg + p, 1), :].astype(BF16)
        hit = jnp.dot(spread_ref[...] * mrow, ones_l, preferred_element_type=F32)
        s = jnp.where(hit > 0.5, s, -jnp.inf)
        update(s.reshape(PAGE, N_HEADS, LANES), v_pages[p][...])

    @pl.when(j == n_j - 1)
    def _():
        prod = (kn_ref[...] * q).astype(BF16)
        s = jnp.dot(prod, ones_d, preferred_element_type=F32)
        hit = mask_ref[n_j * pg:n_j * pg + 1, 0:1]
        s = jnp.where(hit > 0.5, s, -jnp.inf)
        update(s[None], vn_ref[...][None])
        o_ref[...] = acc_scr[...] / l_scr[:, 0:HEAD_DIM]


def _sample_attention(layer, page_table, cache_k, cache_v, q3, kn3, vn3, mask3, spread):
    n_q, n_pages = page_table.shape
    pg = min(8, n_pages)
    kern = functools.partial(_sample_attn_kernel, pg=pg)

    def page_map(p):
        return lambda b, j, pt: (layer, pt[b, j * pg + p], 0, 0, 0)

    page_spec = lambda p: pl.BlockSpec((None, None, PAGE, N_HEADS, HEAD_DIM), page_map(p))
    per_q = lambda b, j, pt: (b, 0, 0)
    in_specs = [page_spec(p) for p in range(pg)] + [page_spec(p) for p in range(pg)]
    in_specs += [
        pl.BlockSpec((None, N_HEADS, HEAD_DIM), per_q),
        pl.BlockSpec((None, N_HEADS, HEAD_DIM), per_q),
        pl.BlockSpec((None, N_HEADS, HEAD_DIM), per_q),
        pl.BlockSpec((None, n_pages + 1, PAGE), per_q),
        pl.BlockSpec(spread.shape, lambda b, j, pt: (0, 0)),
    ]
    return pl.pallas_call(
        kern,
        out_shape=jax.ShapeDtypeStruct((n_q, N_HEADS, HEAD_DIM), F32),
        grid_spec=pltpu.PrefetchScalarGridSpec(
            num_scalar_prefetch=1,
            grid=(n_q, n_pages // pg),
            in_specs=in_specs,
            out_specs=pl.BlockSpec((None, N_HEADS, HEAD_DIM), per_q),
            scratch_shapes=[
                pltpu.VMEM((N_HEADS, LANES), F32),
                pltpu.VMEM((N_HEADS, LANES), F32),
                pltpu.VMEM((N_HEADS, HEAD_DIM), F32),
            ],
        ),
        compiler_params=pltpu.CompilerParams(
            dimension_semantics=("parallel", "arbitrary"), vmem_limit_bytes=VMEM_LIMIT),
        name="sample_attn",
    )(page_table, *([cache_k] * pg), *([cache_v] * pg), q3, kn3, vn3, mask3, spread)


def _sample_merge_kernel(att_ref, aux_ref, sp_ref, sc_ref, hs_ref, wout_ref, poolw_ref, pscale_ref,
                         cw_ref, cb_ref, cg_ref, cbeta_ref, o_ref, *, pos):
    ub = aux_ref[:, A_UB:A_GB]
    uc = aux_ref[:, A_UC:A_GC]
    run = ub
    sums, cnts = [], []
    for j in range(1, POOL_WINDOWS[-1]):
        run = run + sp_ref[POOL_STATE - j]
        if j + 1 in POOL_WINDOWS:
            sums.append(run)
            cnts.append(float(min(pos + 1, j + 1)))
    pool = _pool_mix(sums, ub, cnts, poolw_ref, pscale_ref)

    y = cw_ref[CONV_STATE:CONV_WIDTH, :] * uc
    for j in range(CONV_STATE):
        y = y + cw_ref[j:j + 1, :] * sc_ref[j]
    conv = _conv_norm(y, cb_ref, cg_ref, cbeta_ref)
    o_ref[...] = _gate_project(att_ref[...], aux_ref, pool, conv, wout_ref, hs_ref[...])


def _sample_merge(att, aux, sp_t, sc_t, hs, wout, poolw, pscale, cw, cb, cg, cbeta, pos):
    args = (att, aux, sp_t, sc_t, hs, wout, poolw, pscale, cw, cb, cg, cbeta)
    full = lambda a: pl.BlockSpec(a.shape, lambda i, nd=a.ndim: (0,) * nd)
    return pl.pallas_call(
        functools.partial(_sample_merge_kernel, pos=pos),
        out_shape=jax.ShapeDtypeStruct(hs.shape, F32),
        grid=(1,),
        in_specs=[full(a) for a in args],
        out_specs=full(hs),
        compiler_params=pltpu.CompilerParams(
            dimension_semantics=("arbitrary",), vmem_limit_bytes=VMEM_LIMIT),
        name="sample_merge",
    )(*args)


def _rope_table(pos):
    pos = pos.astype(F32)

    def cs(half):
        inv = ROPE_THETA ** (-(jnp.arange(half, dtype=F32) / half))
        ang = pos[:, None] * inv[None, :]
        return jnp.cos(ang).T, jnp.sin(ang).T

    c8, s8 = cs(HEAD_DIM // 8)
    c4, s4 = cs(D_IDX // 8)
    return jnp.concatenate([c8, s8, c4, c4, -s4, s4], axis=0)


def _permute_w_in(w):
    d = w.shape[0]
    o = 0
    parts = {}
    for name, width in (("q", D_A), ("k", D_A), ("v", D_A), ("ga", D_A), ("qi", D_QI),
                        ("wi", N_IDX_HEADS), ("ki", D_IDX), ("ub", D_B), ("gb", D_B),
                        ("cv", D_C), ("cg", D_C), ("gc", D_C)):
        parts[name] = w[:, o:o + width]
        o += width
    pad = jnp.zeros((d, LANES - D_IDX - N_IDX_HEADS), w.dtype)
    order = ["q", "k", "v", "ga", "qi", "ub", "gb", "cv", "cg", "gc", "ki", "wi"]
    return jnp.concatenate([parts[n] for n in order] + [pad], axis=1).astype(BF16)


def _block_diag(pw):
    n_g, g, _ = pw.shape
    out = jnp.zeros((n_g * g, n_g * g), pw.dtype)
    for i in range(n_g):
        out = out.at[i * g:(i + 1) * g, i * g:(i + 1) * g].set(pw[i])
    return out.astype(BF16)


def kernel(x_prompt, x_sample, cache_k, cache_v, cache_kidx, state_pool, state_conv, page_table,
           meta_tokens, norm_g, w_in, q_norm_g, k_norm_g, pool_w, pool_scale, conv_w, conv_b,
           conv_norm_g, conv_norm_b, w_out):
    n_b, seq, d_model = x_prompt.shape
    n_q = x_sample.shape[0]
    depth = w_in.shape[0]
    n_pages = page_table.shape[1]
    past = n_pages * PAGE
    t_real = seq + N_META
    t_pad = -(-t_real // TM) * TM
    top_p = min(TOPK_MAX, t_real // 4)
    top_s = min(TOPK_MAX, (past + 1) // 4)

    meta = jnp.broadcast_to(meta_tokens[None].astype(F32), (n_b, N_META, d_model))
    tail = jnp.zeros((n_b, t_pad - t_real, d_model), F32)
    hp = jnp.concatenate([meta, x_prompt, tail], axis=1).reshape(n_b * t_pad, d_model)
    hs = x_sample.reshape(n_q, d_model)

    rope_p = _rope_table(jnp.arange(t_pad))
    rope_s = _rope_table(jnp.full((n_q,), past))
    sub = lax.broadcasted_iota(jnp.int32, (PAGE * N_HEADS, LANES), 0)
    ln = lax.broadcasted_iota(jnp.int32, (PAGE * N_HEADS, LANES), 1)
    spread = (sub // N_HEADS == ln).astype(BF16)

    outs = {name: [] for name in ("kp", "vp", "kip", "plp", "cvp", "ks", "vs", "kis", "pls", "cvs")}
    for l in range(depth):
        w_l = _permute_w_in(w_in[l])
        g_l = norm_g[l].reshape(1, d_model)
        qg = q_norm_g[l].reshape(HEAD_DIM, 1)
        kg = k_norm_g[l].reshape(HEAD_DIM, 1)
        wout = w_out[l].astype(BF16)
        poolw = _block_diag(pool_w[l])
        pscale = pool_scale[l].reshape(1, D_B)
        cw = conv_w[l]
        cb = conv_b[l].reshape(1, D_C)
        cg = conv_norm_g[l].reshape(1, D_C)
        cbeta = conv_norm_b[l].reshape(1, D_C)

        q, k, kT, v, vb, aux, qi, wk, kiT = _project(hp, n_b, TM, g_l, w_l, rope_p, qg, kg)
        att = _prompt_attention(q, qi, wk, kT, kiT, vb, n_b, t_pad, t_real, top_p)
        hp = _prompt_merge(att, aux, hp, wout, poolw, pscale, cw, cb, cg, cbeta, n_b, t_pad)
        k3 = k.reshape(n_b, t_pad, D_A)
        v3 = v.reshape(n_b, t_pad, D_A)
        aux3 = aux.reshape(n_b, t_pad, N_AUX)
        outs["kp"].append(k3[:, :t_real].reshape(n_b, t_real, N_HEADS, HEAD_DIM))
        outs["vp"].append(v3[:, :t_real].reshape(n_b, t_real, N_HEADS, HEAD_DIM))
        outs["kip"].append(wk.reshape(n_b, t_pad, LANES)[:, :t_real, WK_KI:WK_KI + D_IDX])
        outs["plp"].append(aux3[:, t_real - POOL_STATE:t_real, A_UB:A_GB])
        outs["cvp"].append(aux3[:, t_real - CONV_STATE:t_real, A_UC:A_GC])

        q, k, _, v, _, aux, qi, wk, _ = _project(hs, 1, n_q, g_l, w_l, rope_s, qg, kg)
        qi3 = qi.reshape(n_q, N_IDX_HEADS, D_IDX)
        wi3 = wk[:, WK_WI:WK_WI + N_IDX_HEADS].reshape(n_q, N_IDX_HEADS, 1)
        kin3 = wk[:, WK_KI:WK_KI + D_IDX].reshape(n_q, 1, D_IDX)
        mask = _sample_index(l, page_table, cache_kidx, qi3, wi3, kin3, top_s)
        att = _sample_attention(
            l, page_table, cache_k, cache_v,
            q.reshape(n_q, N_HEADS, HEAD_DIM), k.reshape(n_q, N_HEADS, HEAD_DIM),
            v.reshape(n_q, N_HEADS, HEAD_DIM), mask.reshape(n_q, n_pages + 1, PAGE), spread)
        sp_t = jnp.swapaxes(state_pool[l], 0, 1)
        sc_t = jnp.swapaxes(state_conv[l], 0, 1)
        hs = _sample_merge(att.reshape(n_q, D_A), aux, sp_t, sc_t, hs, wout, poolw, pscale,
                           cw, cb, cg, cbeta, past)
        outs["ks"].append(k.reshape(n_q, 1, N_HEADS, HEAD_DIM))
        outs["vs"].append(v.reshape(n_q, 1, N_HEADS, HEAD_DIM))
        outs["kis"].append(wk[:, WK_KI:WK_KI + D_IDX].reshape(n_q, 1, D_IDX))
        outs["pls"].append(jnp.concatenate([state_pool[l][:, 1:], aux[:, None, A_UB:A_GB]], axis=1))
        outs["cvs"].append(jnp.concatenate([state_conv[l][:, 1:], aux[:, None, A_UC:A_GC]], axis=1))

    y_prompt = hp.reshape(n_b, t_pad, d_model)[:, N_META:t_real]
    y_sample = hs.reshape(n_q, 1, d_model)
    st = lambda name: jnp.stack(outs[name], axis=0)
    return (y_prompt, y_sample, st("kp"), st("vp"), st("kip"), st("plp"), st("cvp"),
            st("ks"), st("vs"), st("kis"), st("pls"), st("cvs"))
```

```python
import functools

import jax
import jax.numpy as jnp
from jax import lax
from jax.experimental import pallas as pl
from jax.experimental.pallas import tpu as pltpu

F32 = jnp.float32
BF16 = jnp.bfloat16

LANES = 128
SUBLANES = 8
N_META = 16
HEAD_DIM = 64
N_HEADS = 8
D_A = N_HEADS * HEAD_DIM
N_IDX_HEADS = 8
D_IDX = 32
D_QI = N_IDX_HEADS * D_IDX
D_B = 256
D_C = 256
POOL_WINDOWS = (2, 4, 8, 16)
POOL_GROUP = D_B // len(POOL_WINDOWS)
POOL_STATE = 15
CONV_WIDTH = 31
CONV_STATE = CONV_WIDTH - 1
HALO = 32
TOPK_MAX = 256
ROPE_THETA = 500000.0
EPS = 1e-6
PAGE = 128
LOG2E = 1.4426950408889634
Q_SCALE = HEAD_DIM ** -0.5 * LOG2E
IDX_SCALE = (D_IDX ** -0.5) * (N_IDX_HEADS ** -0.5)

QB = 128
CK = 384
TM = 384
MAX_BISECT = 64
BIG_INDEX = 1.0e9
MASKED = -1.0e30
TINY = 1.1754944e-38
ACC_ROWS = 64
VMEM_LIMIT = 56 * 1024 * 1024

C_Q, C_K, C_V, C_GA, C_QI = 0, 512, 1024, 1536, 2048
C_UB, C_GB, C_CV, C_CG, C_GC, C_KI = 2304, 2560, 2816, 3072, 3328, 3584
N_PERM = C_KI + LANES
A_GA, A_UB, A_GB, A_UC, A_GC, N_AUX = 0, 512, 768, 1024, 1280, 1536
WK_KI, WK_WI = 0, D_IDX


def _sigmoid(x):
    return 1.0 / (1.0 + jnp.exp(-x))


def _silu(x):
    return x * _sigmoid(x)


def _proj_kernel(x_ref, g_ref, w_ref, rope_ref, qg_ref, kg_ref,
                 q_ref, k_ref, kT_ref, v_ref, vb_ref, aux_ref, qi_ref, wk_ref, kiT_ref,
                 t_scr):
    x = x_ref[...]
    ms = jnp.mean(x * x, axis=-1, keepdims=True)
    h = (x * lax.rsqrt(ms + EPS) * g_ref[...]).astype(BF16)

    def proj(c0, c1):
        return jnp.dot(h, w_ref[:, c0:c1], preferred_element_type=F32)

    cos8 = rope_ref[0:8, :]
    sin8 = rope_ref[8:16, :]
    cc4 = rope_ref[16:24, :]
    ss4 = rope_ref[24:32, :]

    zT = proj(C_Q, C_V).T
    for j in range(2 * N_HEADS):
        xh = zT[j * HEAD_DIM:(j + 1) * HEAD_DIM, :]
        g = qg_ref[...] if j < N_HEADS else kg_ref[...]
        xn = xh * lax.rsqrt(jnp.mean(xh * xh, axis=0, keepdims=True) + EPS) * g
        x1 = xn[0:8, :]
        x2 = xn[8:16, :]
        r0 = j * HEAD_DIM
        t_scr[r0:r0 + 8, :] = x1 * cos8 - x2 * sin8
        t_scr[r0 + 8:r0 + 16, :] = x2 * cos8 + x1 * sin8
        t_scr[r0 + 16:r0 + HEAD_DIM, :] = xn[16:HEAD_DIM, :]
    q_ref[...] = (t_scr[0:D_A, :].T * Q_SCALE).astype(BF16)
    kT = t_scr[D_A:2 * D_A, :]
    k_ref[...] = kT.T
    kT_ref[...] = kT.astype(BF16)

    zv = proj(C_V, C_GA)
    v_ref[...] = zv
    vb_ref[...] = zv.astype(BF16)

    aux_ref[:, A_GA:A_UB] = proj(C_GA, C_QI)
    aux_ref[:, A_UB:A_UC] = proj(C_UB, C_CV)
    zc = proj(C_CV, C_KI)
    aux_ref[:, A_UC:A_GC] = zc[:, 0:D_C] * _sigmoid(zc[:, D_C:2 * D_C])
    aux_ref[:, A_GC:N_AUX] = zc[:, 2 * D_C:3 * D_C]

    ziT = proj(C_QI, C_UB).T
    parts = []
    for hh in range(N_IDX_HEADS):
        r0 = hh * D_IDX
        x8 = ziT[r0:r0 + 8, :]
        parts.append(x8 * cc4 + pltpu.roll(x8, 4, 0) * ss4)
        parts.append(ziT[r0 + 8:r0 + D_IDX, :])
    qi_ref[...] = jnp.concatenate(parts, axis=0).T.astype(BF16)

    zwT = proj(C_KI, N_PERM).T
    x8 = zwT[0:8, :]
    ki = jnp.concatenate([x8 * cc4 + pltpu.roll(x8, 4, 0) * ss4, zwT[8:D_IDX, :]], axis=0)
    wi = zwT[D_IDX:D_IDX + N_IDX_HEADS, :] * IDX_SCALE
    wk_ref[...] = jnp.concatenate([ki, wi, zwT[D_IDX + N_IDX_HEADS:, :]], axis=0).T
    kiT_ref[...] = jnp.concatenate([ki] * (LANES // D_IDX), axis=0).astype(BF16)


def _project(x2d, n_batch, tm, g, w, rope, qg, kg):
    rows = x2d.shape[0]
    t = rows // n_batch
    nt = t // tm
    d = x2d.shape[1]
    row_map = lambda b, i: (b * nt + i, 0)
    const = lambda b, i: (0, 0)
    out_shape = (
        jax.ShapeDtypeStruct((rows, D_A), BF16),
        jax.ShapeDtypeStruct((rows, D_A), F32),
        jax.ShapeDtypeStruct((n_batch, D_A, t), BF16),
        jax.ShapeDtypeStruct((rows, D_A), F32),
        jax.ShapeDtypeStruct((rows, D_A), BF16),
        jax.ShapeDtypeStruct((rows, N_AUX), F32),
        jax.ShapeDtypeStruct((rows, D_QI), BF16),
        jax.ShapeDtypeStruct((rows, LANES), F32),
        jax.ShapeDtypeStruct((n_batch, LANES, t), BF16),
    )
    out_specs = (
        pl.BlockSpec((tm, D_A), row_map),
        pl.BlockSpec((tm, D_A), row_map),
        pl.BlockSpec((None, D_A, tm), lambda b, i: (b, 0, i)),
        pl.BlockSpec((tm, D_A), row_map),
        pl.BlockSpec((tm, D_A), row_map),
        pl.BlockSpec((tm, N_AUX), row_map),
        pl.BlockSpec((tm, D_QI), row_map),
        pl.BlockSpec((tm, LANES), row_map),
        pl.BlockSpec((None, LANES, tm), lambda b, i: (b, 0, i)),
    )
    in_specs = [
        pl.BlockSpec((tm, d), row_map),
        pl.BlockSpec((1, d), const),
        pl.BlockSpec((d, N_PERM), const),
        pl.BlockSpec((32, tm), lambda b, i: (0, i)),
        pl.BlockSpec((HEAD_DIM, 1), const),
        pl.BlockSpec((HEAD_DIM, 1), const),
    ]
    return pl.pallas_call(
        _proj_kernel,
        out_shape=out_shape,
        grid=(n_batch, nt),
        in_specs=in_specs,
        out_specs=out_specs,
        scratch_shapes=[pltpu.VMEM((2 * D_A, tm), F32)],
        compiler_params=pltpu.CompilerParams(
            dimension_semantics=("parallel", "parallel"), vmem_limit_bytes=VMEM_LIMIT),
        name="proj",
    )(x2d, g, w, rope, qg, kg)


def _chunk_fold(i_scr, n_ch, ck, fn, inits):
    rows = i_scr.shape[0]

    def body(c, accs):
        for k in range(ck // LANES):
            col = pl.multiple_of(c * ck + k * LANES, LANES)
            accs = fn(accs, i_scr[:, pl.ds(col, LANES)], col)
        return accs

    return lax.fori_loop(0, n_ch, body, tuple(jnp.full((rows, LANES), v, F32) for v in inits))


def _lane_total(acc):
    return jnp.dot(acc.astype(BF16), jnp.ones((LANES, LANES), BF16), preferred_element_type=F32)


def _replicate(col):
    return jnp.broadcast_to(col, (col.shape[0], LANES))


def _count_ge(i_scr, n_ch, ck, midb):
    (acc,) = _chunk_fold(i_scr, n_ch, ck,
                         lambda a, blk, col: (a[0] + jnp.where(blk >= midb, 1.0, 0.0),), (0.0,))
    return _lane_total(acc)


def _select_bounds(i_scr, n_ch, ck, top_k, trivial, jcut_scr):
    rows = i_scr.shape[0]
    kf = float(top_k)
    ninf = -jnp.inf
    pinf = jnp.inf
    land, lor, lnot = jnp.logical_and, jnp.logical_or, jnp.logical_not

    trivial = _replicate(trivial)
    mn, mx = _chunk_fold(
        i_scr, n_ch, ck,
        lambda a, blk, col: (jnp.minimum(a[0], jnp.where(blk == ninf, pinf, blk)), jnp.maximum(a[1], blk)),
        (pinf, ninf))
    cpos, cnn = _chunk_fold(
        i_scr, n_ch, ck,
        lambda a, blk, col: (a[0] + jnp.where(blk > 0.0, 1.0, 0.0), a[1] + jnp.where(blk >= 0.0, 1.0, 0.0)),
        (0.0, 0.0))
    row_min = _replicate(jnp.min(mn, axis=1, keepdims=True))
    row_max = _replicate(jnp.max(mx, axis=1, keepdims=True))
    c_pos = _lane_total(cpos)
    c_nn = _lane_total(cnn)
    c_top = _count_ge(i_scr, n_ch, ck, row_max)

    top_tie = land(c_top >= kf, lnot(trivial))
    zero_tie = land(land(c_pos < kf, c_nn >= kf), lnot(lor(trivial, top_tie)))
    neg_side = c_nn < kf
    lo0 = jnp.where(top_tie, row_max, jnp.where(zero_tie, 0.0, jnp.where(neg_side, row_min, TINY)))
    lo0 = jnp.where(trivial, row_min, lo0)
    hi0 = jnp.where(lor(trivial, top_tie), pinf,
                    jnp.where(zero_tie, jnp.where(c_nn == kf, pinf, TINY),
                              jnp.where(neg_side, 0.0, row_max)))
    pos_exact = land(lnot(neg_side), c_pos == kf)
    hi0 = jnp.where(land(pos_exact, lnot(lor(trivial, lor(top_tie, zero_tie)))), pinf, hi0)
    done0 = lor(lor(trivial, land(top_tie, c_top == kf)), lor(land(zero_tie, c_nn == kf), pos_exact))
    res0 = jnp.where(done0, 1.0, 0.0)
    act0 = jnp.where(lor(done0, lor(top_tie, zero_tie)), 0.0, 1.0)

    def bis_cond(st):
        return jnp.logical_and(st[0] < MAX_BISECT, st[1] > 0.0)

    def bis_body(st):
        it, _, lo, hi, res, act = st
        n_act = jnp.sum(act)
        mid = 0.5 * lo + 0.5 * hi
        ok = land(act > 0.0, land(mid > lo, mid < hi))
        c = _count_ge(i_scr, n_ch, ck, mid)
        eq = land(ok, c == kf)
        gt = land(ok, c > kf)
        lt = land(ok, c < kf)
        lo = jnp.where(lor(eq, gt), mid, lo)
        hi = jnp.where(eq, pinf, jnp.where(lt, mid, hi))
        res = jnp.where(eq, 1.0, res)
        act = jnp.where(lor(eq, lnot(ok)), 0.0, act)
        return it + 1, n_act, lo, hi, res, act

    _, _, lo, hi, res, _ = lax.while_loop(
        bis_cond, bis_body, (jnp.int32(0), jnp.sum(act0), lo0, hi0, res0, act0))

    jcut_scr[...] = jnp.full((rows, LANES), BIG_INDEX, F32)
    n_open = jnp.sum(1.0 - res)

    @pl.when(n_open > 0.0)
    def _():
        need = kf - _count_ge(i_scr, n_ch, ck, hi)
        lane = lax.broadcasted_iota(jnp.int32, (rows, LANES), 1)

        def count_band(jb):
            def fn(a, blk, col):
                idx = (lane + col).astype(F32)
                in_band = land(land(blk >= lo, blk < hi), idx <= jb)
                return (a[0] + jnp.where(in_band, 1.0, 0.0),)

            (acc,) = _chunk_fold(i_scr, n_ch, ck, fn, (0.0,))
            return _lane_total(acc)

        def j_cond(st):
            return jnp.logical_and(st[0] < 32, st[1] > 0.0)

        def j_body(st):
            it, _, jl, jh, jc, open_ = st
            n_still = jnp.sum(open_)
            jm = jnp.floor((jl + jh + 1.0) * 0.5)
            c = count_band(jm)
            is_open = open_ > 0.0
            eq = land(is_open, c == need)
            jl = jnp.where(land(is_open, c < need), jm, jl)
            jh = jnp.where(land(is_open, c > need), jm, jh)
            jc = jnp.where(eq, jm, jc)
            open_ = jnp.where(eq, 0.0, open_)
            return it + 1, n_still, jl, jh, jc, open_

        width = i_scr.shape[1]
        open0 = 1.0 - res
        full = lambda v: jnp.full((rows, LANES), v, F32)
        st0 = (jnp.int32(0), jnp.sum(open0), full(-1.0), full(float(width - 1)), full(BIG_INDEX), open0)
        jcut_scr[...] = lax.while_loop(j_cond, j_body, st0)[4]

    return lo[:, 0:1], hi[:, 0:1], jcut_scr[:, 0:1]


def _fold_t(it_scr, n_ch, ck, fn, inits):
    def body(c, accs):
        row = pl.multiple_of(c * ck, SUBLANES)
        return fn(accs, it_scr[pl.ds(row, ck), :], row)

    return lax.fori_loop(0, n_ch, body, tuple(jnp.full((ACC_ROWS, LANES), v, F32) for v in inits))


def _vreg_fold(x, op):
    return op(x.reshape(x.shape[0] // ACC_ROWS, ACC_ROWS, LANES), axis=0)


def _count_ge_t(it_scr, n_ch, ck, mid):
    (acc,) = _fold_t(
        it_scr, n_ch, ck,
        lambda a, blk, row: (a[0] + _vreg_fold(jnp.where(blk >= mid, 1.0, 0.0), jnp.sum),), (0.0,))
    return jnp.sum(acc, axis=0, keepdims=True)


def _select_bounds_t(it_scr, n_ch, ck, top_k, trivial):
    kf = float(top_k)
    ninf = -jnp.inf
    pinf = jnp.inf
    land, lor, lnot = jnp.logical_and, jnp.logical_or, jnp.logical_not
    passes = 4

    mn, mx = _fold_t(
        it_scr, n_ch, ck,
        lambda a, blk, row: (jnp.minimum(a[0], _vreg_fold(jnp.where(blk == ninf, pinf, blk), jnp.min)),
                             jnp.maximum(a[1], _vreg_fold(blk, jnp.max))),
        (pinf, ninf))
    cpos, cnn = _fold_t(
        it_scr, n_ch, ck,
        lambda a, blk, row: (a[0] + _vreg_fold(jnp.where(blk > 0.0, 1.0, 0.0), jnp.sum),
                             a[1] + _vreg_fold(jnp.where(blk >= 0.0, 1.0, 0.0), jnp.sum)),
        (0.0, 0.0))
    col_min = jnp.min(mn, axis=0, keepdims=True)
    col_max = jnp.max(mx, axis=0, keepdims=True)
    c_pos = jnp.sum(cpos, axis=0, keepdims=True)
    c_nn = jnp.sum(cnn, axis=0, keepdims=True)
    c_top = _count_ge_t(it_scr, n_ch, ck, col_max)

    top_tie = land(c_top >= kf, lnot(trivial))
    zero_tie = land(land(c_pos < kf, c_nn >= kf), lnot(lor(trivial, top_tie)))
    neg_side = c_nn < kf
    lo0 = jnp.where(top_tie, col_max, jnp.where(zero_tie, 0.0, jnp.where(neg_side, col_min, TINY)))
    lo0 = jnp.where(trivial, col_min, lo0)
    hi0 = jnp.where(lor(trivial, top_tie), pinf,
                    jnp.where(zero_tie, jnp.where(c_nn == kf, pinf, TINY),
                              jnp.where(neg_side, 0.0, col_max)))
    pos_exact = land(lnot(neg_side), c_pos == kf)
    hi0 = jnp.where(land(pos_exact, lnot(lor(trivial, lor(top_tie, zero_tie)))), pinf, hi0)
    done0 = lor(lor(trivial, land(top_tie, c_top == kf)), lor(land(zero_tie, c_nn == kf), pos_exact))
    res0 = jnp.where(done0, 1.0, 0.0)
    act0 = jnp.where(lor(done0, lor(top_tie, zero_tie)), 0.0, 1.0)

    def bis_cond(st):
        return jnp.logical_and(st[0] < MAX_BISECT, st[1] > 0.0)

    def bis_body(st):
        it, _, lo, hi, res, act = st
        for _ in range(passes):
            mid = 0.5 * lo + 0.5 * hi
            ok = land(act > 0.0, land(mid > lo, mid < hi))
            c = _count_ge_t(it_scr, n_ch, ck, mid)
            eq = land(ok, c == kf)
            gt = land(ok, c > kf)
            lt = land(ok, c < kf)
            lo = jnp.where(lor(eq, gt), mid, lo)
            hi = jnp.where(eq, pinf, jnp.where(lt, mid, hi))
            res = jnp.where(eq, 1.0, res)
            act = jnp.where(lor(eq, lnot(ok)), 0.0, act)
        return it + passes, jnp.sum(act), lo, hi, res, act

    _, _, lo, hi, res, _ = lax.while_loop(
        bis_cond, bis_body, (jnp.int32(0), jnp.sum(act0), lo0, hi0, res0, act0))

    def cut_band():
        need = kf - _count_ge_t(it_scr, n_ch, ck, hi)

        def count_band(j):
            def fn(a, blk, row):
                idx = (lax.broadcasted_iota(jnp.int32, blk.shape, 0) + row).astype(F32)
                in_band = land(land(blk >= lo, blk < hi), idx <= j)
                return (a[0] + _vreg_fold(jnp.where(in_band, 1.0, 0.0), jnp.sum),)

            (acc,) = _fold_t(it_scr, n_ch, ck, fn, (0.0,))
            return jnp.sum(acc, axis=0, keepdims=True)

        def j_cond(st):
            return jnp.logical_and(st[0] < 32, st[1] > 0.0)

        def j_body(st):
            it, _, jl, jh, jc, open_ = st
            jm = jnp.floor((jl + jh + 1.0) * 0.5)
            c = count_band(jm)
            is_open = open_ > 0.0
            eq = land(is_open, c == need)
            jl = jnp.where(land(is_open, c < need), jm, jl)
            jh = jnp.where(land(is_open, c > need), jm, jh)
            jc = jnp.where(eq, jm, jc)
            open_ = jnp.where(eq, 0.0, open_)
            return it + 1, jnp.sum(open_), jl, jh, jc, open_

        open0 = 1.0 - res
        full = lambda v: jnp.full((1, LANES), v, F32)
        st0 = (jnp.int32(0), jnp.sum(open0), full(-1.0), full(float(it_scr.shape[0] - 1)),
               full(BIG_INDEX), open0)
        return lax.while_loop(j_cond, j_body, st0)[4]

    jcut = lax.cond(jnp.sum(1.0 - res) > 0.0, cut_band, lambda: jnp.full((1, LANES), BIG_INDEX, F32))
    return lo, hi, jcut


def _as_column(row):
    return jnp.broadcast_to(row, (LANES, LANES)).T


def _selected(blk, col, lo, hi, jcut):
    lane = lax.broadcasted_iota(jnp.int32, blk.shape, 1)
    idx = (lane + col).astype(F32)
    return jnp.logical_or(blk >= hi, jnp.logical_and(blk >= lo, idx <= jcut))


def _attn_kernel(q_ref, qi_ref, wk_ref, kT_ref, kiT_ref, v_ref, o_ref,
                 i_scr, it_scr, s_scr, m_scr, l_scr, acc_scr, *, t_real, top_k):
    i = pl.program_id(1)
    n_ch = (i * QB) // CK + 1
    q_pos = i * QB + lax.broadcasted_iota(jnp.int32, (QB, 1), 0)
    lane = lax.broadcasted_iota(jnp.int32, (QB, LANES), 1)
    n_sub = CK // LANES

    qi = qi_ref[...]
    q = q_ref[...]
    qiz, qz = [], []
    per = LANES // D_IDX
    for h in range(N_IDX_HEADS):
        blk = qi[:, (h // per) * LANES:(h // per + 1) * LANES]
        qiz.append(jnp.where(lane // D_IDX == h % per, blk, jnp.zeros_like(blk)))
    for h in range(N_HEADS):
        blk = q[:, (h // 2) * LANES:(h // 2 + 1) * LANES]
        qz.append(jnp.where(lane // HEAD_DIM == h % 2, blk, jnp.zeros_like(blk)))
    w = wk_ref[:, WK_WI:WK_WI + N_IDX_HEADS]

    def score_chunk(c, carry):
        col = pl.multiple_of(c * CK, LANES)
        kc = kiT_ref[:, pl.ds(col, CK)]
        acc = jnp.zeros((QB, CK), F32)
        for h in range(N_IDX_HEADS):
            d = jnp.dot(qiz[h], kc, preferred_element_type=F32)
            acc = acc + w[:, h:h + 1] * jnp.maximum(d, 0.0)
        key_pos = col + lax.broadcasted_iota(jnp.int32, (QB, CK), 1)
        sc = jnp.where(key_pos <= q_pos, acc, -jnp.inf)
        i_scr[:, pl.ds(col, CK)] = sc
        for k in range(n_sub):
            row = pl.multiple_of(col + k * LANES, LANES)
            it_scr[pl.ds(row, LANES), :] = sc[:, k * LANES:(k + 1) * LANES].T
        return carry

    lax.fori_loop(0, n_ch, score_chunk, 0)

    q_row = i * QB + lax.broadcasted_iota(jnp.int32, (1, LANES), 1)
    trivial = jnp.logical_or(q_row + 1 <= top_k, q_row >= t_real)
    lo, hi, jcut = (_as_column(r)[:, 0:1] for r in _select_bounds_t(it_scr, n_ch, CK, top_k, trivial))

    m_scr[...] = jnp.full(m_scr.shape, MASKED, F32)

    def logits_chunk(c, carry):
        col = pl.multiple_of(c * CK, LANES)
        sel = _selected(i_scr[:, pl.ds(col, CK)], col, lo, hi, jcut)
        bias = jnp.where(sel, 0.0, MASKED)
        for h in range(N_HEADS):
            pr = h // 2
            kc = kT_ref[pr * LANES:(pr + 1) * LANES, pl.ds(col, CK)]
            s = jnp.dot(qz[h], kc, preferred_element_type=F32) + bias
            s_scr[h, :, pl.ds(col, CK)] = s
            mx = s[:, 0:LANES]
            for k in range(1, n_sub):
                mx = jnp.maximum(mx, s[:, k * LANES:(k + 1) * LANES])
            m_scr[h] = jnp.maximum(m_scr[h], mx)
        return carry

    lax.fori_loop(0, n_ch, logits_chunk, 0)

    for h in range(N_HEADS):
        m_scr[h] = jnp.broadcast_to(jnp.max(m_scr[h], axis=1, keepdims=True), (QB, LANES))
    l_scr[...] = jnp.zeros(l_scr.shape, F32)
    acc_scr[...] = jnp.zeros(acc_scr.shape, F32)

    def value_chunk(c, carry):
        col = pl.multiple_of(c * CK, LANES)
        for h in range(N_HEADS):
            pr = h // 2
            m = m_scr[h]
            ps = [jnp.exp2(s_scr[h, :, pl.ds(pl.multiple_of(col + k * LANES, LANES), LANES)] - m)
                  for k in range(n_sub)]
            lsum = ps[0]
            for k in range(1, n_sub):
                lsum = lsum + ps[k]
            l_scr[h] = l_scr[h] + lsum
            p = jnp.concatenate(ps, axis=1).astype(BF16)
            vc = v_ref[pl.ds(col, CK), pr * LANES:(pr + 1) * LANES]
            acc_scr[h] = acc_scr[h] + jnp.dot(p, vc, preferred_element_type=F32)
        return carry

    lax.fori_loop(0, n_ch, value_chunk, 0)

    for pr in range(N_HEADS // 2):
        o0 = acc_scr[2 * pr] / jnp.sum(l_scr[2 * pr], axis=1, keepdims=True)
        o1 = acc_scr[2 * pr + 1] / jnp.sum(l_scr[2 * pr + 1], axis=1, keepdims=True)
        o_ref[:, pr * LANES:(pr + 1) * LANES] = jnp.where(lane < HEAD_DIM, o0, o1)


def _prompt_attention(q, qi, wk, kT, kiT, vb, n_batch, t_pad, t_real, top_k):
    nq = t_pad // QB
    row_map = lambda b, i: (b * nq + i, 0)
    kern = functools.partial(_attn_kernel, t_real=t_real, top_k=top_k)
    return pl.pallas_call(
        kern,
        out_shape=jax.ShapeDtypeStruct((n_batch * t_pad, D_A), F32),
        grid=(n_batch, nq),
        in_specs=[
            pl.BlockSpec((QB, D_A), row_map),
            pl.BlockSpec((QB, D_QI), row_map),
            pl.BlockSpec((QB, LANES), row_map),
            pl.BlockSpec((None, D_A, t_pad), lambda b, i: (b, 0, 0)),
            pl.BlockSpec((None, LANES, t_pad), lambda b, i: (b, 0, 0)),
            pl.BlockSpec((t_pad, D_A), lambda b, i: (b, 0)),
        ],
        out_specs=pl.BlockSpec((QB, D_A), row_map),
        scratch_shapes=[
            pltpu.VMEM((QB, t_pad), F32),
            pltpu.VMEM((t_pad, QB), F32),
            pltpu.VMEM((N_HEADS, QB, t_pad), F32),
            pltpu.VMEM((N_HEADS, QB, LANES), F32),
            pltpu.VMEM((N_HEADS, QB, LANES), F32),
            pltpu.VMEM((N_HEADS, QB, LANES), F32),
        ],
        compiler_params=pltpu.CompilerParams(
            dimension_semantics=("parallel", "arbitrary"), vmem_limit_bytes=VMEM_LIMIT),
        name="prompt_attn",
    )(q, qi, wk, kT, kiT, vb)


def _gate_project(att, aux_ref, pool, conv, wout_ref, resid):
    ya = att * _silu(aux_ref[:, A_GA:A_UB])
    yb = pool * _silu(aux_ref[:, A_GB:A_UC])
    yc = conv * _silu(aux_ref[:, A_GC:N_AUX])
    y = jnp.concatenate([ya, yb, yc], axis=1).astype(BF16)
    return resid + jnp.dot(y, wout_ref[...], preferred_element_type=F32)


def _pool_mix(sums, u, cnts, poolw_ref, pscale_ref):
    lane = lax.broadcasted_iota(jnp.int32, u.shape, 1)
    d = sums[-1] / cnts[-1]
    for g in range(len(POOL_WINDOWS) - 2, -1, -1):
        d = jnp.where(lane < (g + 1) * POOL_GROUP, sums[g] / cnts[g], d)
    d = (d - u).astype(BF16)
    return jnp.dot(d, poolw_ref[...], preferred_element_type=F32) * pscale_ref[...]


def _conv_norm(y, cb_ref, cg_ref, cbeta_ref):
    y = y + cb_ref[...]
    mu = jnp.mean(y, axis=-1, keepdims=True)
    yc = y - mu
    var = jnp.mean(yc * yc, axis=-1, keepdims=True)
    return _silu(yc * lax.rsqrt(var + EPS) * cg_ref[...] + cbeta_ref[...])


def _merge_kernel(att_ref, aux_ref, hub_ref, huc_ref, hp_ref, wout_ref, poolw_ref, pscale_ref,
                  cw_ref, cb_ref, cg_ref, cbeta_ref, o_ref, ub_scr, uc_scr):
    i = pl.program_id(1)
    tm = att_ref.shape[0]
    first = i == 0
    ub = aux_ref[:, A_UB:A_GB]
    uc = aux_ref[:, A_UC:A_GC]
    ub_scr[0:HALO, :] = jnp.where(first, 0.0, hub_ref[...])
    uc_scr[0:HALO, :] = jnp.where(first, 0.0, huc_ref[...])
    ub_scr[HALO:HALO + tm, :] = ub
    uc_scr[HALO:HALO + tm, :] = uc

    pos = i * tm + lax.broadcasted_iota(jnp.int32, (tm, 1), 0)
    run = ub
    sums, cnts = [], []
    for j in range(1, POOL_WINDOWS[-1]):
        run = run + ub_scr[HALO - j:HALO - j + tm, :]
        if j + 1 in POOL_WINDOWS:
            sums.append(run)
            cnts.append(jnp.minimum(pos + 1, j + 1).astype(F32))
    pool = _pool_mix(sums, ub, cnts, poolw_ref, pscale_ref)

    y = jnp.zeros((tm, D_C), F32)
    for j in range(CONV_WIDTH):
        r0 = HALO - CONV_STATE + j
        y = y + cw_ref[j:j + 1, :] * uc_scr[r0:r0 + tm, :]
    conv = _conv_norm(y, cb_ref, cg_ref, cbeta_ref)

    o_ref[...] = _gate_project(att_ref[...], aux_ref, pool, conv, wout_ref, hp_ref[...])


def _prompt_merge(att, aux, hp, wout, poolw, pscale, cw, cb, cg, cbeta, n_batch, t_pad):
    nt = t_pad // TM
    hb = TM // HALO
    row_map = lambda b, i: (b * nt + i, 0)
    const = lambda b, i: (0, 0)
    d = hp.shape[1]

    def halo_map(col_block):
        return lambda b, i: (jnp.maximum((b * nt + i) * hb - 1, 0), col_block)

    return pl.pallas_call(
        _merge_kernel,
        out_shape=jax.ShapeDtypeStruct(hp.shape, F32),
        grid=(n_batch, nt),
        in_specs=[
            pl.BlockSpec((TM, D_A), row_map),
            pl.BlockSpec((TM, N_AUX), row_map),
            pl.BlockSpec((HALO, D_B), halo_map(A_UB // D_B)),
            pl.BlockSpec((HALO, D_C), halo_map(A_UC // D_C)),
            pl.BlockSpec((TM, d), row_map),
            pl.BlockSpec(wout.shape, const),
            pl.BlockSpec(poolw.shape, const),
            pl.BlockSpec((1, D_B), const),
            pl.BlockSpec(cw.shape, const),
            pl.BlockSpec((1, D_C), const),
            pl.BlockSpec((1, D_C), const),
            pl.BlockSpec((1, D_C), const),
        ],
        out_specs=pl.BlockSpec((TM, d), row_map),
        scratch_shapes=[pltpu.VMEM((HALO + TM, D_B), F32), pltpu.VMEM((HALO + TM, D_C), F32)],
        compiler_params=pltpu.CompilerParams(
            dimension_semantics=("parallel", "parallel"), vmem_limit_bytes=VMEM_LIMIT),
        name="prompt_merge",
    )(att, aux, aux, aux, hp, wout, poolw, pscale, cw, cb, cg, cbeta)


def _sample_index_kernel(pt_ref, *refs, n_pages, top_k):
    del pt_ref
    ki_pages = refs[:n_pages]
    qi_ref, wi_ref, kin_ref, mask_ref, i_scr, it_scr = refs[n_pages:]
    b = pl.program_id(0)
    n_q = i_scr.shape[0]
    past = n_pages * PAGE
    qi = qi_ref[...]
    w = wi_ref[...]

    @pl.when(b == 0)
    def _():
        i_scr[...] = jnp.full(i_scr.shape, -jnp.inf, F32)

    row0 = pl.multiple_of((b // SUBLANES) * SUBLANES, SUBLANES)
    mine = lax.broadcasted_iota(jnp.int32, (SUBLANES, LANES), 0) == b % SUBLANES

    def put_row(c0, row):
        grp = i_scr[pl.ds(row0, SUBLANES), c0:c0 + LANES]
        i_scr[pl.ds(row0, SUBLANES), c0:c0 + LANES] = jnp.where(mine, row, grp)

    for p in range(n_pages):
        kp = ki_pages[p][...].astype(BF16)
        d = jnp.dot(qi, kp, preferred_element_type=F32)
        put_row(p * PAGE, jnp.sum(w * jnp.maximum(d, 0.0), axis=0, keepdims=True))
    dn = jnp.sum(qi.astype(F32) * kin_ref[...], axis=1, keepdims=True)
    sn = jnp.sum(w * jnp.maximum(dn, 0.0), axis=0, keepdims=True)
    lane1 = lax.broadcasted_iota(jnp.int32, (1, LANES), 1)
    put_row(past, jnp.where(lane1 == 0, sn, -jnp.inf))

    @pl.when(b == n_q - 1)
    def _():
        width = i_scr.shape[1]
        for k in range(width // LANES):
            it_scr[k * LANES:(k + 1) * LANES, :] = i_scr[:, k * LANES:(k + 1) * LANES].T
        trivial = jnp.full((1, LANES), past + 1 <= top_k)
        lo, hi, jcut = (_as_column(r)[:, 0:1] for r in _select_bounds_t(it_scr, 1, width, top_k, trivial))
        for k in range(width // LANES):
            sel = _selected(i_scr[:, k * LANES:(k + 1) * LANES], k * LANES, lo, hi, jcut)
            mask_ref[:, k * LANES:(k + 1) * LANES] = jnp.where(sel, 1.0, 0.0)


def _sample_index(layer, page_table, cache_kidx_t, qi3, wi3, kin3, top_k):
    n_q, n_pages = page_table.shape
    width = (n_pages + 1) * PAGE
    kern = functools.partial(_sample_index_kernel, n_pages=n_pages, top_k=top_k)

    def page_map(p):
        return lambda b, pt: (layer, pt[b, p], 0, 0)

    in_specs = [pl.BlockSpec((None, None, D_IDX, PAGE), page_map(p)) for p in range(n_pages)]
    in_specs += [
        pl.BlockSpec((None, N_IDX_HEADS, D_IDX), lambda b, pt: (b, 0, 0)),
        pl.BlockSpec((None, N_IDX_HEADS, 1), lambda b, pt: (b, 0, 0)),
        pl.BlockSpec((None, 1, D_IDX), lambda b, pt: (b, 0, 0)),
    ]
    return pl.pallas_call(
        kern,
        out_shape=jax.ShapeDtypeStruct((n_q, width), F32),
        grid_spec=pltpu.PrefetchScalarGridSpec(
            num_scalar_prefetch=1,
            grid=(n_q,),
            in_specs=in_specs,
            out_specs=pl.BlockSpec((n_q, width), lambda b, pt: (0, 0)),
            scratch_shapes=[pltpu.VMEM((n_q, width), F32), pltpu.VMEM((width, n_q), F32)],
        ),
        compiler_params=pltpu.CompilerParams(
            dimension_semantics=("arbitrary",), vmem_limit_bytes=VMEM_LIMIT),
        name="sample_index",
    )(page_table, *([cache_kidx_t] * n_pages), qi3, wi3, kin3)


def _sample_attn_kernel(pt_ref, *refs, n_pages):
    del pt_ref
    k_pages = refs[:n_pages]
    v_pages = refs[n_pages:2 * n_pages]
    (q8_ref, qT_ref, kn_ref, vnT_ref, mask_ref, o_ref, oT_scr) = refs[2 * n_pages:]
    b = pl.program_id(0)
    n_q = pl.num_programs(0)
    lane = lax.broadcasted_iota(jnp.int32, (HEAD_DIM, LANES), 1)
    qT = qT_ref[...]
    vnT = vnT_ref[...]
    s_new = jnp.sum(q8_ref[...] * kn_ref[...], axis=1, keepdims=True)
    hit_new = mask_ref[n_pages:n_pages + 1, 0:1] > 0.5
    hits = [mask_ref[p:p + 1, :] > 0.5 for p in range(n_pages)]

    @pl.when(b == 0)
    def _():
        oT_scr[...] = jnp.zeros(oT_scr.shape, F32)

    for h in range(N_HEADS):
        qcol = jnp.broadcast_to(qT[:, h:h + 1], (HEAD_DIM, LANES))
        s = [jnp.where(hits[p], jnp.sum(k_pages[p][h] * qcol, axis=0, keepdims=True), MASKED)
             for p in range(n_pages)]
        sn = jnp.where(hit_new, s_new[h:h + 1, :], MASKED)
        mx = s[0]
        for p in range(1, n_pages):
            mx = jnp.maximum(mx, s[p])
        m = jnp.maximum(jnp.max(mx, axis=1, keepdims=True), sn)
        pn = jnp.exp2(sn - m)
        acc = jnp.zeros((HEAD_DIM, LANES), F32)
        lsum = jnp.zeros((1, LANES), F32)
        for p in range(n_pages):
            pp = jnp.exp2(s[p] - m)
            lsum = lsum + pp
            acc = acc + v_pages[p][h] * pp
        l = jnp.sum(lsum, axis=1, keepdims=True) + pn
        o_h = (jnp.sum(acc, axis=1, keepdims=True) + vnT[:, h:h + 1] * pn) / l
        rows = slice(h * HEAD_DIM, (h + 1) * HEAD_DIM)
        oT_scr[rows, :] = jnp.where(lane == b, o_h, oT_scr[rows, :])

    @pl.when(b == n_q - 1)
    def _():
        o_ref[...] = oT_scr[...].T


def _sample_attention(layer, page_table, cache_k_t, cache_v_t, q8, qT, kn, vnT, mask3):
    n_q, n_pages = page_table.shape
    assert n_q == LANES, "one output lane per sample query"
    kern = functools.partial(_sample_attn_kernel, n_pages=n_pages)

    def page_map(p):
        return lambda b, pt: (layer, pt[b, p], 0, 0, 0)

    page_spec = lambda p: pl.BlockSpec((None, None, N_HEADS, HEAD_DIM, PAGE), page_map(p))
    per_q = lambda b, pt: (b, 0, 0)
    in_specs = [page_spec(p) for p in range(n_pages)] + [page_spec(p) for p in range(n_pages)]
    in_specs += [
        pl.BlockSpec((None, N_HEADS, HEAD_DIM), per_q),
        pl.BlockSpec((None, HEAD_DIM, N_HEADS), per_q),
        pl.BlockSpec((None, N_HEADS, HEAD_DIM), per_q),
        pl.BlockSpec((None, HEAD_DIM, N_HEADS), per_q),
        pl.BlockSpec((None, n_pages + 1, PAGE), per_q),
    ]
    return pl.pallas_call(
        kern,
        out_shape=jax.ShapeDtypeStruct((n_q, D_A), F32),
        grid_spec=pltpu.PrefetchScalarGridSpec(
            num_scalar_prefetch=1,
            grid=(n_q,),
            in_specs=in_specs,
            out_specs=pl.BlockSpec((n_q, D_A), lambda b, pt: (0, 0)),
            scratch_shapes=[pltpu.VMEM((D_A, n_q), F32)],
        ),
        compiler_params=pltpu.CompilerParams(
            dimension_semantics=("arbitrary",), vmem_limit_bytes=VMEM_LIMIT),
        name="sample_attn",
    )(page_table, *([cache_k_t] * n_pages), *([cache_v_t] * n_pages), q8, qT, kn, vnT, mask3)


def _sample_merge_kernel(att_ref, aux_ref, sp_ref, sc_ref, hs_ref, wout_ref, poolw_ref, pscale_ref,
                         cw_ref, cb_ref, cg_ref, cbeta_ref, o_ref, *, pos):
    ub = aux_ref[:, A_UB:A_GB]
    uc = aux_ref[:, A_UC:A_GC]
    run = ub
    sums, cnts = [], []
    for j in range(1, POOL_WINDOWS[-1]):
        run = run + sp_ref[POOL_STATE - j]
        if j + 1 in POOL_WINDOWS:
            sums.append(run)
            cnts.append(float(min(pos + 1, j + 1)))
    pool = _pool_mix(sums, ub, cnts, poolw_ref, pscale_ref)

    y = cw_ref[CONV_STATE:CONV_WIDTH, :] * uc
    for j in range(CONV_STATE):
        y = y + cw_ref[j:j + 1, :] * sc_ref[j]
    conv = _conv_norm(y, cb_ref, cg_ref, cbeta_ref)
    o_ref[...] = _gate_project(att_ref[...], aux_ref, pool, conv, wout_ref, hs_ref[...])


def _sample_merge(att, aux, sp_t, sc_t, hs, wout, poolw, pscale, cw, cb, cg, cbeta, pos):
    args = (att, aux, sp_t, sc_t, hs, wout, poolw, pscale, cw, cb, cg, cbeta)
    full = lambda a: pl.BlockSpec(a.shape, lambda i, nd=a.ndim: (0,) * nd)
    return pl.pallas_call(
        functools.partial(_sample_merge_kernel, pos=pos),
        out_shape=jax.ShapeDtypeStruct(hs.shape, F32),
        grid=(1,),
        in_specs=[full(a) for a in args],
        out_specs=full(hs),
        compiler_params=pltpu.CompilerParams(
            dimension_semantics=("arbitrary",), vmem_limit_bytes=VMEM_LIMIT),
        name="sample_merge",
    )(*args)


def _rope_table(pos):
    pos = pos.astype(F32)

    def cs(half):
        inv = ROPE_THETA ** (-(jnp.arange(half, dtype=F32) / half))
        ang = pos[:, None] * inv[None, :]
        return jnp.cos(ang).T, jnp.sin(ang).T

    c8, s8 = cs(HEAD_DIM // 8)
    c4, s4 = cs(D_IDX // 8)
    return jnp.concatenate([c8, s8, c4, c4, -s4, s4], axis=0)


def _permute_w_in(w):
    d = w.shape[0]
    o = 0
    parts = {}
    for name, width in (("q", D_A), ("k", D_A), ("v", D_A), ("ga", D_A), ("qi", D_QI),
                        ("wi", N_IDX_HEADS), ("ki", D_IDX), ("ub", D_B), ("gb", D_B),
                        ("cv", D_C), ("cg", D_C), ("gc", D_C)):
        parts[name] = w[:, o:o + width]
        o += width
    pad = jnp.zeros((d, LANES - D_IDX - N_IDX_HEADS), w.dtype)
    order = ["q", "k", "v", "ga", "qi", "ub", "gb", "cv", "cg", "gc", "ki", "wi"]
    return jnp.concatenate([parts[n] for n in order] + [pad], axis=1).astype(BF16)


def _block_diag(pw):
    n_g, g, _ = pw.shape
    out = jnp.zeros((n_g * g, n_g * g), pw.dtype)
    for i in range(n_g):
        out = out.at[i * g:(i + 1) * g, i * g:(i + 1) * g].set(pw[i])
    return out.astype(BF16)


def kernel(x_prompt, x_sample, cache_k, cache_v, cache_kidx, state_pool, state_conv, page_table,
           meta_tokens, norm_g, w_in, q_norm_g, k_norm_g, pool_w, pool_scale, conv_w, conv_b,
           conv_norm_g, conv_norm_b, w_out):
    n_b, seq, d_model = x_prompt.shape
    n_q = x_sample.shape[0]
    depth = w_in.shape[0]
    n_pages = page_table.shape[1]
    past = n_pages * PAGE
    t_real = seq + N_META
    t_pad = -(-t_real // TM) * TM
    top_p = min(TOPK_MAX, t_real // 4)
    top_s = min(TOPK_MAX, (past + 1) // 4)

    meta = jnp.broadcast_to(meta_tokens[None].astype(F32), (n_b, N_META, d_model))
    tail = jnp.zeros((n_b, t_pad - t_real, d_model), F32)
    hp = jnp.concatenate([meta, x_prompt, tail], axis=1).reshape(n_b * t_pad, d_model)
    hs = x_sample.reshape(n_q, d_model)

    rope_p = _rope_table(jnp.arange(t_pad))
    rope_s = _rope_table(jnp.full((n_q,), past))
    cache_k_t = jnp.transpose(cache_k, (0, 1, 3, 4, 2))
    cache_v_t = jnp.transpose(cache_v, (0, 1, 3, 4, 2))
    cache_kidx_t = jnp.transpose(cache_kidx, (0, 1, 3, 2))

    outs = {name: [] for name in ("kp", "vp", "kip", "plp", "cvp", "ks", "vs", "kis", "pls", "cvs")}
    for l in range(depth):
        w_l = _permute_w_in(w_in[l])
        g_l = norm_g[l].reshape(1, d_model)
        qg = q_norm_g[l].reshape(HEAD_DIM, 1)
        kg = k_norm_g[l].reshape(HEAD_DIM, 1)
        wout = w_out[l].astype(BF16)
        poolw = _block_diag(pool_w[l])
        pscale = pool_scale[l].reshape(1, D_B)
        cw = conv_w[l]
        cb = conv_b[l].reshape(1, D_C)
        cg = conv_norm_g[l].reshape(1, D_C)
        cbeta = conv_norm_b[l].reshape(1, D_C)

        q, k, kT, v, vb, aux, qi, wk, kiT = _project(hp, n_b, TM, g_l, w_l, rope_p, qg, kg)
        att = _prompt_attention(q, qi, wk, kT, kiT, vb, n_b, t_pad, t_real, top_p)
        hp = _prompt_merge(att, aux, hp, wout, poolw, pscale, cw, cb, cg, cbeta, n_b, t_pad)
        k3 = k.reshape(n_b, t_pad, D_A)
        v3 = v.reshape(n_b, t_pad, D_A)
        aux3 = aux.reshape(n_b, t_pad, N_AUX)
        outs["kp"].append(k3[:, :t_real].reshape(n_b, t_real, N_HEADS, HEAD_DIM))
        outs["vp"].append(v3[:, :t_real].reshape(n_b, t_real, N_HEADS, HEAD_DIM))
        outs["kip"].append(wk.reshape(n_b, t_pad, LANES)[:, :t_real, WK_KI:WK_KI + D_IDX])
        outs["plp"].append(aux3[:, t_real - POOL_STATE:t_real, A_UB:A_GB])
        outs["cvp"].append(aux3[:, t_real - CONV_STATE:t_real, A_UC:A_GC])

        q, k, _, v, _, aux, qi, wk, _ = _project(hs, 1, n_q, g_l, w_l, rope_s, qg, kg)
        qi3 = qi.reshape(n_q, N_IDX_HEADS, D_IDX)
        wi3 = wk[:, WK_WI:WK_WI + N_IDX_HEADS].reshape(n_q, N_IDX_HEADS, 1)
        kin3 = wk[:, WK_KI:WK_KI + D_IDX].reshape(n_q, 1, D_IDX)
        mask = _sample_index(l, page_table, cache_kidx_t, qi3, wi3, kin3, top_s)
        q8 = q.astype(F32).reshape(n_q, N_HEADS, HEAD_DIM)
        kn = k.reshape(n_q, N_HEADS, HEAD_DIM)
        vnT = jnp.swapaxes(v.reshape(n_q, N_HEADS, HEAD_DIM), 1, 2)
        att = _sample_attention(l, page_table, cache_k_t, cache_v_t, q8, jnp.swapaxes(q8, 1, 2),
                                kn, vnT, mask.reshape(n_q, n_pages + 1, PAGE))
        sp_t = jnp.swapaxes(state_pool[l], 0, 1)
        sc_t = jnp.swapaxes(state_conv[l], 0, 1)
        hs = _sample_merge(att, aux, sp_t, sc_t, hs, wout, poolw, pscale, cw, cb, cg, cbeta, past)
        outs["ks"].append(k.reshape(n_q, 1, N_HEADS, HEAD_DIM))
        outs["vs"].append(v.reshape(n_q, 1, N_HEADS, HEAD_DIM))
        outs["kis"].append(wk[:, WK_KI:WK_KI + D_IDX].reshape(n_q, 1, D_IDX))
        outs["pls"].append(jnp.concatenate([state_pool[l][:, 1:], aux[:, None, A_UB:A_GB]], axis=1))
        outs["cvs"].append(jnp.concatenate([state_conv[l][:, 1:], aux[:, None, A_UC:A_GC]], axis=1))

    y_prompt = hp.reshape(n_b, t_pad, d_model)[:, N_META:t_real]
    y_sample = hs.reshape(n_q, 1, d_model)
    st = lambda name: jnp.stack(outs[name], axis=0)
    return (y_prompt, y_sample, st("kp"), st("vp"), st("kip"), st("plp"), st("cvp"),
            st("ks"), st("vs"), st("kis"), st("pls"), st("cvs"))
```

```python
import functools

import jax
import jax.numpy as jnp
from jax import lax
from jax.experimental import pallas as pl
from jax.experimental.pallas import tpu as pltpu

F32 = jnp.float32
BF16 = jnp.bfloat16

LANES = 128
SUBLANES = 8
N_META = 16
HEAD_DIM = 64
N_HEADS = 8
D_A = N_HEADS * HEAD_DIM
N_IDX_HEADS = 8
D_IDX = 32
D_QI = N_IDX_HEADS * D_IDX
D_B = 256
D_C = 256
POOL_WINDOWS = (2, 4, 8, 16)
POOL_GROUP = D_B // len(POOL_WINDOWS)
POOL_STATE = 15
CONV_WIDTH = 31
CONV_STATE = CONV_WIDTH - 1
HALO = 32
TOPK_MAX = 256
ROPE_THETA = 500000.0
EPS = 1e-6
PAGE = 128
LOG2E = 1.4426950408889634
Q_SCALE = HEAD_DIM ** -0.5 * LOG2E
IDX_SCALE = (D_IDX ** -0.5) * (N_IDX_HEADS ** -0.5)

QB = 128
CK = 384
TM = 384
MAX_BISECT = 64
BIG_INDEX = 1.0e9
MASKED = -1.0e30
TINY = 1.1754944e-38
ACC_ROWS = 64
VMEM_LIMIT = 56 * 1024 * 1024

C_Q, C_K, C_V, C_GA, C_QI = 0, 512, 1024, 1536, 2048
C_UB, C_GB, C_CV, C_CG, C_GC, C_KI = 2304, 2560, 2816, 3072, 3328, 3584
N_PERM = C_KI + LANES
A_GA, A_UB, A_GB, A_UC, A_GC, N_AUX = 0, 512, 768, 1024, 1280, 1536
WK_KI, WK_WI = 0, D_IDX


def _sigmoid(x):
    return 1.0 / (1.0 + jnp.exp(-x))


def _silu(x):
    return x * _sigmoid(x)


def _proj_kernel(x_ref, g_ref, w_ref, rope_ref, qg_ref, kg_ref,
                 q_ref, kTf_ref, kT_ref, vTf_ref, vb_ref, aux_ref, qi_ref, wk_ref, kiTf_ref, kiT_ref,
                 t_scr):
    x = x_ref[...]
    ms = jnp.mean(x * x, axis=-1, keepdims=True)
    h = (x * lax.rsqrt(ms + EPS) * g_ref[...]).astype(BF16)

    def proj(c0, c1):
        return jnp.dot(h, w_ref[:, c0:c1], preferred_element_type=F32)

    cos8 = rope_ref[0:8, :]
    sin8 = rope_ref[8:16, :]
    cc4 = rope_ref[16:24, :]
    ss4 = rope_ref[24:32, :]

    zT = proj(C_Q, C_V).T
    for j in range(2 * N_HEADS):
        xh = zT[j * HEAD_DIM:(j + 1) * HEAD_DIM, :]
        g = qg_ref[...] if j < N_HEADS else kg_ref[...]
        xn = xh * lax.rsqrt(jnp.mean(xh * xh, axis=0, keepdims=True) + EPS) * g
        x1 = xn[0:8, :]
        x2 = xn[8:16, :]
        r0 = j * HEAD_DIM
        t_scr[r0:r0 + 8, :] = x1 * cos8 - x2 * sin8
        t_scr[r0 + 8:r0 + 16, :] = x2 * cos8 + x1 * sin8
        t_scr[r0 + 16:r0 + HEAD_DIM, :] = xn[16:HEAD_DIM, :]
    q_ref[...] = (t_scr[0:D_A, :].T * Q_SCALE).astype(BF16)
    kT = t_scr[D_A:2 * D_A, :]
    kTf_ref[...] = kT
    kT_ref[...] = kT.astype(BF16)

    zv = proj(C_V, C_GA)
    vTf_ref[...] = zv.T
    vb_ref[...] = zv.astype(BF16)

    aux_ref[:, A_GA:A_UB] = proj(C_GA, C_QI)
    aux_ref[:, A_UB:A_UC] = proj(C_UB, C_CV)
    zc = proj(C_CV, C_KI)
    aux_ref[:, A_UC:A_GC] = zc[:, 0:D_C] * _sigmoid(zc[:, D_C:2 * D_C])
    aux_ref[:, A_GC:N_AUX] = zc[:, 2 * D_C:3 * D_C]

    ziT = proj(C_QI, C_UB).T
    parts = []
    for hh in range(N_IDX_HEADS):
        r0 = hh * D_IDX
        x8 = ziT[r0:r0 + 8, :]
        parts.append(x8 * cc4 + pltpu.roll(x8, 4, 0) * ss4)
        parts.append(ziT[r0 + 8:r0 + D_IDX, :])
    qi_ref[...] = jnp.concatenate(parts, axis=0).T.astype(BF16)

    zwT = proj(C_KI, N_PERM).T
    x8 = zwT[0:8, :]
    ki = jnp.concatenate([x8 * cc4 + pltpu.roll(x8, 4, 0) * ss4, zwT[8:D_IDX, :]], axis=0)
    wi = zwT[D_IDX:D_IDX + N_IDX_HEADS, :] * IDX_SCALE
    wk_ref[...] = jnp.concatenate([ki, wi, zwT[D_IDX + N_IDX_HEADS:, :]], axis=0).T
    kiTf_ref[...] = ki
    kiT_ref[...] = jnp.concatenate([ki] * (LANES // D_IDX), axis=0).astype(BF16)


def _project(x2d, n_batch, t_real, tm, g, w, rope, qg, kg):
    rows = x2d.shape[0]
    t = rows // n_batch
    nt = t // tm
    d = x2d.shape[1]
    row_map = lambda b, i: (b * nt + i, 0)
    col_map = lambda b, i: (b, 0, i)
    const = lambda b, i: (0, 0)
    out_shape = (
        jax.ShapeDtypeStruct((rows, D_A), BF16),
        jax.ShapeDtypeStruct((n_batch, D_A, t_real), F32),
        jax.ShapeDtypeStruct((n_batch, D_A, t), BF16),
        jax.ShapeDtypeStruct((n_batch, D_A, t_real), F32),
        jax.ShapeDtypeStruct((rows, D_A), BF16),
        jax.ShapeDtypeStruct((rows, N_AUX), F32),
        jax.ShapeDtypeStruct((rows, D_QI), BF16),
        jax.ShapeDtypeStruct((rows, LANES), F32),
        jax.ShapeDtypeStruct((n_batch, D_IDX, t_real), F32),
        jax.ShapeDtypeStruct((n_batch, LANES, t), BF16),
    )
    out_specs = (
        pl.BlockSpec((tm, D_A), row_map),
        pl.BlockSpec((None, D_A, tm), col_map),
        pl.BlockSpec((None, D_A, tm), col_map),
        pl.BlockSpec((None, D_A, tm), col_map),
        pl.BlockSpec((tm, D_A), row_map),
        pl.BlockSpec((tm, N_AUX), row_map),
        pl.BlockSpec((tm, D_QI), row_map),
        pl.BlockSpec((tm, LANES), row_map),
        pl.BlockSpec((None, D_IDX, tm), col_map),
        pl.BlockSpec((None, LANES, tm), col_map),
    )
    in_specs = [
        pl.BlockSpec((tm, d), row_map),
        pl.BlockSpec((1, d), const),
        pl.BlockSpec((d, N_PERM), const),
        pl.BlockSpec((32, tm), lambda b, i: (0, i)),
        pl.BlockSpec((HEAD_DIM, 1), const),
        pl.BlockSpec((HEAD_DIM, 1), const),
    ]
    return pl.pallas_call(
        _proj_kernel,
        out_shape=out_shape,
        grid=(n_batch, nt),
        in_specs=in_specs,
        out_specs=out_specs,
        scratch_shapes=[pltpu.VMEM((2 * D_A, tm), F32)],
        compiler_params=pltpu.CompilerParams(
            dimension_semantics=("parallel", "parallel"), vmem_limit_bytes=VMEM_LIMIT),
        name="proj",
    )(x2d, g, w, rope, qg, kg)


def _chunk_fold(i_scr, n_ch, ck, fn, inits):
    rows = i_scr.shape[0]

    def body(c, accs):
        for k in range(ck // LANES):
            col = pl.multiple_of(c * ck + k * LANES, LANES)
            accs = fn(accs, i_scr[:, pl.ds(col, LANES)], col)
        return accs

    return lax.fori_loop(0, n_ch, body, tuple(jnp.full((rows, LANES), v, F32) for v in inits))


def _lane_total(acc):
    return jnp.dot(acc.astype(BF16), jnp.ones((LANES, LANES), BF16), preferred_element_type=F32)


def _replicate(col):
    return jnp.broadcast_to(col, (col.shape[0], LANES))


def _count_ge(i_scr, n_ch, ck, midb):
    (acc,) = _chunk_fold(i_scr, n_ch, ck,
                         lambda a, blk, col: (a[0] + jnp.where(blk >= midb, 1.0, 0.0),), (0.0,))
    return _lane_total(acc)


def _select_bounds(i_scr, n_ch, ck, top_k, trivial, jcut_scr):
    rows = i_scr.shape[0]
    kf = float(top_k)
    ninf = -jnp.inf
    pinf = jnp.inf
    land, lor, lnot = jnp.logical_and, jnp.logical_or, jnp.logical_not

    trivial = _replicate(trivial)
    mn, mx = _chunk_fold(
        i_scr, n_ch, ck,
        lambda a, blk, col: (jnp.minimum(a[0], jnp.where(blk == ninf, pinf, blk)), jnp.maximum(a[1], blk)),
        (pinf, ninf))
    cpos, cnn = _chunk_fold(
        i_scr, n_ch, ck,
        lambda a, blk, col: (a[0] + jnp.where(blk > 0.0, 1.0, 0.0), a[1] + jnp.where(blk >= 0.0, 1.0, 0.0)),
        (0.0, 0.0))
    row_min = _replicate(jnp.min(mn, axis=1, keepdims=True))
    row_max = _replicate(jnp.max(mx, axis=1, keepdims=True))
    c_pos = _lane_total(cpos)
    c_nn = _lane_total(cnn)
    c_top = _count_ge(i_scr, n_ch, ck, row_max)

    top_tie = land(c_top >= kf, lnot(trivial))
    zero_tie = land(land(c_pos < kf, c_nn >= kf), lnot(lor(trivial, top_tie)))
    neg_side = c_nn < kf
    lo0 = jnp.where(top_tie, row_max, jnp.where(zero_tie, 0.0, jnp.where(neg_side, row_min, TINY)))
    lo0 = jnp.where(trivial, row_min, lo0)
    hi0 = jnp.where(lor(trivial, top_tie), pinf,
                    jnp.where(zero_tie, jnp.where(c_nn == kf, pinf, TINY),
                              jnp.where(neg_side, 0.0, row_max)))
    pos_exact = land(lnot(neg_side), c_pos == kf)
    hi0 = jnp.where(land(pos_exact, lnot(lor(trivial, lor(top_tie, zero_tie)))), pinf, hi0)
    done0 = lor(lor(trivial, land(top_tie, c_top == kf)), lor(land(zero_tie, c_nn == kf), pos_exact))
    res0 = jnp.where(done0, 1.0, 0.0)
    act0 = jnp.where(lor(done0, lor(top_tie, zero_tie)), 0.0, 1.0)

    def bis_cond(st):
        return jnp.logical_and(st[0] < MAX_BISECT, st[1] > 0.0)

    def bis_body(st):
        it, _, lo, hi, res, act = st
        n_act = jnp.sum(act)
        mid = 0.5 * lo + 0.5 * hi
        ok = land(act > 0.0, land(mid > lo, mid < hi))
        c = _count_ge(i_scr, n_ch, ck, mid)
        eq = land(ok, c == kf)
        gt = land(ok, c > kf)
        lt = land(ok, c < kf)
        lo = jnp.where(lor(eq, gt), mid, lo)
        hi = jnp.where(eq, pinf, jnp.where(lt, mid, hi))
        res = jnp.where(eq, 1.0, res)
        act = jnp.where(lor(eq, lnot(ok)), 0.0, act)
        return it + 1, n_act, lo, hi, res, act

    _, _, lo, hi, res, _ = lax.while_loop(
        bis_cond, bis_body, (jnp.int32(0), jnp.sum(act0), lo0, hi0, res0, act0))

    jcut_scr[...] = jnp.full((rows, LANES), BIG_INDEX, F32)
    n_open = jnp.sum(1.0 - res)

    @pl.when(n_open > 0.0)
    def _():
        need = kf - _count_ge(i_scr, n_ch, ck, hi)
        lane = lax.broadcasted_iota(jnp.int32, (rows, LANES), 1)

        def count_band(jb):
            def fn(a, blk, col):
                idx = (lane + col).astype(F32)
                in_band = land(land(blk >= lo, blk < hi), idx <= jb)
                return (a[0] + jnp.where(in_band, 1.0, 0.0),)

            (acc,) = _chunk_fold(i_scr, n_ch, ck, fn, (0.0,))
            return _lane_total(acc)

        def j_cond(st):
            return jnp.logical_and(st[0] < 32, st[1] > 0.0)

        def j_body(st):
            it, _, jl, jh, jc, open_ = st
            n_still = jnp.sum(open_)
            jm = jnp.floor((jl + jh + 1.0) * 0.5)
            c = count_band(jm)
            is_open = open_ > 0.0
            eq = land(is_open, c == need)
            jl = jnp.where(land(is_open, c < need), jm, jl)
            jh = jnp.where(land(is_open, c > need), jm, jh)
            jc = jnp.where(eq, jm, jc)
            open_ = jnp.where(eq, 0.0, open_)
            return it + 1, n_still, jl, jh, jc, open_

        width = i_scr.shape[1]
        open0 = 1.0 - res
        full = lambda v: jnp.full((rows, LANES), v, F32)
        st0 = (jnp.int32(0), jnp.sum(open0), full(-1.0), full(float(width - 1)), full(BIG_INDEX), open0)
        jcut_scr[...] = lax.while_loop(j_cond, j_body, st0)[4]

    return lo[:, 0:1], hi[:, 0:1], jcut_scr[:, 0:1]


def _fold_t(it_scr, n_ch, ck, fn, inits):
    def body(c, accs):
        row = pl.multiple_of(c * ck, SUBLANES)
        return fn(accs, it_scr[pl.ds(row, ck), :], row)

    return lax.fori_loop(0, n_ch, body, tuple(jnp.full((ACC_ROWS, LANES), v, F32) for v in inits))


def _vreg_fold(x, op):
    return op(x.reshape(x.shape[0] // ACC_ROWS, ACC_ROWS, LANES), axis=0)


def _count_ge_t(it_scr, n_ch, ck, mid):
    (acc,) = _fold_t(
        it_scr, n_ch, ck,
        lambda a, blk, row: (a[0] + _vreg_fold(jnp.where(blk >= mid, 1.0, 0.0), jnp.sum),), (0.0,))
    return jnp.sum(acc, axis=0, keepdims=True)


def _select_bounds_t(it_scr, n_ch, ck, top_k, trivial):
    kf = float(top_k)
    ninf = -jnp.inf
    pinf = jnp.inf
    land, lor, lnot = jnp.logical_and, jnp.logical_or, jnp.logical_not
    passes = 4

    mn, mx = _fold_t(
        it_scr, n_ch, ck,
        lambda a, blk, row: (jnp.minimum(a[0], _vreg_fold(jnp.where(blk == ninf, pinf, blk), jnp.min)),
                             jnp.maximum(a[1], _vreg_fold(blk, jnp.max))),
        (pinf, ninf))
    cpos, cnn = _fold_t(
        it_scr, n_ch, ck,
        lambda a, blk, row: (a[0] + _vreg_fold(jnp.where(blk > 0.0, 1.0, 0.0), jnp.sum),
                             a[1] + _vreg_fold(jnp.where(blk >= 0.0, 1.0, 0.0), jnp.sum)),
        (0.0, 0.0))
    col_min = jnp.min(mn, axis=0, keepdims=True)
    col_max = jnp.max(mx, axis=0, keepdims=True)
    c_pos = jnp.sum(cpos, axis=0, keepdims=True)
    c_nn = jnp.sum(cnn, axis=0, keepdims=True)
    c_top = _count_ge_t(it_scr, n_ch, ck, col_max)

    top_tie = land(c_top >= kf, lnot(trivial))
    zero_tie = land(land(c_pos < kf, c_nn >= kf), lnot(lor(trivial, top_tie)))
    neg_side = c_nn < kf
    lo0 = jnp.where(top_tie, col_max, jnp.where(zero_tie, 0.0, jnp.where(neg_side, col_min, TINY)))
    lo0 = jnp.where(trivial, col_min, lo0)
    hi0 = jnp.where(lor(trivial, top_tie), pinf,
                    jnp.where(zero_tie, jnp.where(c_nn == kf, pinf, TINY),
                              jnp.where(neg_side, 0.0, col_max)))
    pos_exact = land(lnot(neg_side), c_pos == kf)
    hi0 = jnp.where(land(pos_exact, lnot(lor(trivial, lor(top_tie, zero_tie)))), pinf, hi0)
    done0 = lor(lor(trivial, land(top_tie, c_top == kf)), lor(land(zero_tie, c_nn == kf), pos_exact))
    res0 = jnp.where(done0, 1.0, 0.0)
    act0 = jnp.where(lor(done0, lor(top_tie, zero_tie)), 0.0, 1.0)

    def bis_cond(st):
        return jnp.logical_and(st[0] < MAX_BISECT, st[1] > 0.0)

    def bis_body(st):
        it, _, lo, hi, res, act = st
        for _ in range(passes):
            mid = 0.5 * lo + 0.5 * hi
            ok = land(act > 0.0, land(mid > lo, mid < hi))
            c = _count_ge_t(it_scr, n_ch, ck, mid)
            eq = land(ok, c == kf)
            gt = land(ok, c > kf)
            lt = land(ok, c < kf)
            lo = jnp.where(lor(eq, gt), mid, lo)
            hi = jnp.where(eq, pinf, jnp.where(lt, mid, hi))
            res = jnp.where(eq, 1.0, res)
            act = jnp.where(lor(eq, lnot(ok)), 0.0, act)
        return it + passes, jnp.sum(act), lo, hi, res, act

    _, _, lo, hi, res, _ = lax.while_loop(
        bis_cond, bis_body, (jnp.int32(0), jnp.sum(act0), lo0, hi0, res0, act0))

    def cut_band():
        need = kf - _count_ge_t(it_scr, n_ch, ck, hi)

        def count_band(j):
            def fn(a, blk, row):
                idx = (lax.broadcasted_iota(jnp.int32, blk.shape, 0) + row).astype(F32)
                in_band = land(land(blk >= lo, blk < hi), idx <= j)
                return (a[0] + _vreg_fold(jnp.where(in_band, 1.0, 0.0), jnp.sum),)

            (acc,) = _fold_t(it_scr, n_ch, ck, fn, (0.0,))
            return jnp.sum(acc, axis=0, keepdims=True)

        def j_cond(st):
            return jnp.logical_and(st[0] < 32, st[1] > 0.0)

        def j_body(st):
            it, _, jl, jh, jc, open_ = st
            jm = jnp.floor((jl + jh + 1.0) * 0.5)
            c = count_band(jm)
            is_open = open_ > 0.0
            eq = land(is_open, c == need)
            jl = jnp.where(land(is_open, c < need), jm, jl)
            jh = jnp.where(land(is_open, c > need), jm, jh)
            jc = jnp.where(eq, jm, jc)
            open_ = jnp.where(eq, 0.0, open_)
            return it + 1, jnp.sum(open_), jl, jh, jc, open_

        open0 = 1.0 - res
        full = lambda v: jnp.full((1, LANES), v, F32)
        st0 = (jnp.int32(0), jnp.sum(open0), full(-1.0), full(float(it_scr.shape[0] - 1)),
               full(BIG_INDEX), open0)
        return lax.while_loop(j_cond, j_body, st0)[4]

    jcut = lax.cond(jnp.sum(1.0 - res) > 0.0, cut_band, lambda: jnp.full((1, LANES), BIG_INDEX, F32))
    return lo, hi, jcut


def _paired_loop(n, body, group=3):
    def grouped(j, carry):
        for u in range(group):
            carry = body(group * j + u, carry)
        return carry

    lax.fori_loop(0, n // group, grouped, 0)
    lax.fori_loop((n // group) * group, n, body, 0)


def _as_column(row):
    return jnp.broadcast_to(row, (LANES, LANES)).T


def _selected(blk, col, lo, hi, jcut):
    lane = lax.broadcasted_iota(jnp.int32, blk.shape, 1)
    idx = (lane + col).astype(F32)
    return jnp.logical_or(blk >= hi, jnp.logical_and(blk >= lo, idx <= jcut))


def _attn_kernel(q_ref, qi_ref, wk_ref, kT_ref, kiT_ref, v_ref, o_ref,
                 i_scr, it_scr, s_scr, m_scr, l_scr, acc_scr, *, t_real, top_k):
    i = pl.program_id(1)
    n_ch = (i * QB) // CK + 1
    q_pos = i * QB + lax.broadcasted_iota(jnp.int32, (QB, 1), 0)
    lane = lax.broadcasted_iota(jnp.int32, (QB, LANES), 1)
    n_sub = CK // LANES

    qi = qi_ref[...]
    q = q_ref[...]
    qiz, qz = [], []
    per = LANES // D_IDX
    for h in range(N_IDX_HEADS):
        blk = qi[:, (h // per) * LANES:(h // per + 1) * LANES]
        qiz.append(jnp.where(lane // D_IDX == h % per, blk, jnp.zeros_like(blk)))
    for h in range(N_HEADS):
        blk = q[:, (h // 2) * LANES:(h // 2 + 1) * LANES]
        qz.append(jnp.where(lane // HEAD_DIM == h % 2, blk, jnp.zeros_like(blk)))
    w = wk_ref[:, WK_WI:WK_WI + N_IDX_HEADS]

    def score_chunk(c, carry):
        col = pl.multiple_of(c * CK, LANES)
        kc = kiT_ref[:, pl.ds(col, CK)]
        acc = jnp.zeros((QB, CK), F32)
        for h in range(N_IDX_HEADS):
            d = jnp.dot(qiz[h], kc, preferred_element_type=F32)
            acc = acc + w[:, h:h + 1] * jnp.maximum(d, 0.0)
        key_pos = col + lax.broadcasted_iota(jnp.int32, (QB, CK), 1)
        sc = jnp.where(key_pos <= q_pos, acc, -jnp.inf)
        i_scr[:, pl.ds(col, CK)] = sc
        for k in range(n_sub):
            row = pl.multiple_of(col + k * LANES, LANES)
            it_scr[pl.ds(row, LANES), :] = sc[:, k * LANES:(k + 1) * LANES].T
        return carry

    _paired_loop(n_ch, score_chunk)

    q_row = i * QB + lax.broadcasted_iota(jnp.int32, (1, LANES), 1)
    trivial = jnp.logical_or(q_row + 1 <= top_k, q_row >= t_real)
    lo, hi, jcut = (_as_column(r)[:, 0:1] for r in _select_bounds_t(it_scr, n_ch, CK, top_k, trivial))

    m_scr[...] = jnp.full(m_scr.shape, MASKED, F32)

    def logits_chunk(c, carry):
        col = pl.multiple_of(c * CK, LANES)
        sel = _selected(i_scr[:, pl.ds(col, CK)], col, lo, hi, jcut)
        bias = jnp.where(sel, 0.0, MASKED)
        for h in range(N_HEADS):
            pr = h // 2
            kc = kT_ref[pr * LANES:(pr + 1) * LANES, pl.ds(col, CK)]
            s = jnp.dot(qz[h], kc, preferred_element_type=F32) + bias
            s_scr[h, :, pl.ds(col, CK)] = s
            mx = s[:, 0:LANES]
            for k in range(1, n_sub):
                mx = jnp.maximum(mx, s[:, k * LANES:(k + 1) * LANES])
            m_scr[h] = jnp.maximum(m_scr[h], mx)
        return carry

    _paired_loop(n_ch, logits_chunk)

    for h in range(N_HEADS):
        m_scr[h] = jnp.broadcast_to(jnp.max(m_scr[h], axis=1, keepdims=True), (QB, LANES))
    l_scr[...] = jnp.zeros(l_scr.shape, F32)
    acc_scr[...] = jnp.zeros(acc_scr.shape, F32)

    def value_chunk(c, carry):
        col = pl.multiple_of(c * CK, LANES)
        for h in range(N_HEADS):
            pr = h // 2
            m = m_scr[h]
            ps = [jnp.exp2(s_scr[h, :, pl.ds(pl.multiple_of(col + k * LANES, LANES), LANES)] - m)
                  for k in range(n_sub)]
            lsum = ps[0]
            for k in range(1, n_sub):
                lsum = lsum + ps[k]
            l_scr[h] = l_scr[h] + lsum
            p = jnp.concatenate(ps, axis=1).astype(BF16)
            vc = v_ref[pl.ds(col, CK), pr * LANES:(pr + 1) * LANES]
            acc_scr[h] = acc_scr[h] + jnp.dot(p, vc, preferred_element_type=F32)
        return carry

    _paired_loop(n_ch, value_chunk)

    for pr in range(N_HEADS // 2):
        o0 = acc_scr[2 * pr] / jnp.sum(l_scr[2 * pr], axis=1, keepdims=True)
        o1 = acc_scr[2 * pr + 1] / jnp.sum(l_scr[2 * pr + 1], axis=1, keepdims=True)
        o_ref[:, pr * LANES:(pr + 1) * LANES] = jnp.where(lane < HEAD_DIM, o0, o1)


def _prompt_attention(q, qi, wk, kT, kiT, vb, n_batch, t_pad, t_real, top_k):
    nq = t_pad // QB
    row_map = lambda b, i: (b * nq + i, 0)
    kern = functools.partial(_attn_kernel, t_real=t_real, top_k=top_k)
    return pl.pallas_call(
        kern,
        out_shape=jax.ShapeDtypeStruct((n_batch * t_pad, D_A), F32),
        grid=(n_batch, nq),
        in_specs=[
            pl.BlockSpec((QB, D_A), row_map),
            pl.BlockSpec((QB, D_QI), row_map),
            pl.BlockSpec((QB, LANES), row_map),
            pl.BlockSpec((None, D_A, t_pad), lambda b, i: (b, 0, 0)),
            pl.BlockSpec((None, LANES, t_pad), lambda b, i: (b, 0, 0)),
            pl.BlockSpec((t_pad, D_A), lambda b, i: (b, 0)),
        ],
        out_specs=pl.BlockSpec((QB, D_A), row_map),
        scratch_shapes=[
            pltpu.VMEM((QB, t_pad), F32),
            pltpu.VMEM((t_pad, QB), F32),
            pltpu.VMEM((N_HEADS, QB, t_pad), F32),
            pltpu.VMEM((N_HEADS, QB, LANES), F32),
            pltpu.VMEM((N_HEADS, QB, LANES), F32),
            pltpu.VMEM((N_HEADS, QB, LANES), F32),
        ],
        compiler_params=pltpu.CompilerParams(
            dimension_semantics=("parallel", "arbitrary"), vmem_limit_bytes=VMEM_LIMIT),
        name="prompt_attn",
    )(q, qi, wk, kT, kiT, vb)


def _gate_project(att, aux_ref, pool, conv, wout_ref, resid):
    ya = att * _silu(aux_ref[:, A_GA:A_UB])
    yb = pool * _silu(aux_ref[:, A_GB:A_UC])
    yc = conv * _silu(aux_ref[:, A_GC:N_AUX])
    y = jnp.concatenate([ya, yb, yc], axis=1).astype(BF16)
    return resid + jnp.dot(y, wout_ref[...], preferred_element_type=F32)


def _pool_mix(sums, u, cnts, poolw_ref, pscale_ref):
    lane = lax.broadcasted_iota(jnp.int32, u.shape, 1)
    d = sums[-1] / cnts[-1]
    for g in range(len(POOL_WINDOWS) - 2, -1, -1):
        d = jnp.where(lane < (g + 1) * POOL_GROUP, sums[g] / cnts[g], d)
    d = (d - u).astype(BF16)
    return jnp.dot(d, poolw_ref[...], preferred_element_type=F32) * pscale_ref[...]


def _conv_norm(y, cb_ref, cg_ref, cbeta_ref):
    y = y + cb_ref[...]
    mu = jnp.mean(y, axis=-1, keepdims=True)
    yc = y - mu
    var = jnp.mean(yc * yc, axis=-1, keepdims=True)
    return _silu(yc * lax.rsqrt(var + EPS) * cg_ref[...] + cbeta_ref[...])


def _merge_kernel(att_ref, aux_ref, hub_ref, huc_ref, hp_ref, wout_ref, poolw_ref, pscale_ref,
                  cw_ref, cb_ref, cg_ref, cbeta_ref, o_ref, ub_scr, uc_scr):
    i = pl.program_id(1)
    tm = att_ref.shape[0]
    first = i == 0
    ub = aux_ref[:, A_UB:A_GB]
    uc = aux_ref[:, A_UC:A_GC]
    ub_scr[0:HALO, :] = jnp.where(first, 0.0, hub_ref[...])
    uc_scr[0:HALO, :] = jnp.where(first, 0.0, huc_ref[...])
    ub_scr[HALO:HALO + tm, :] = ub
    uc_scr[HALO:HALO + tm, :] = uc

    pos = i * tm + lax.broadcasted_iota(jnp.int32, (tm, 1), 0)
    run = ub
    sums, cnts = [], []
    for j in range(1, POOL_WINDOWS[-1]):
        run = run + ub_scr[HALO - j:HALO - j + tm, :]
        if j + 1 in POOL_WINDOWS:
            sums.append(run)
            cnts.append(jnp.minimum(pos + 1, j + 1).astype(F32))
    pool = _pool_mix(sums, ub, cnts, poolw_ref, pscale_ref)

    y = jnp.zeros((tm, D_C), F32)
    for j in range(CONV_WIDTH):
        r0 = HALO - CONV_STATE + j
        y = y + cw_ref[j:j + 1, :] * uc_scr[r0:r0 + tm, :]
    conv = _conv_norm(y, cb_ref, cg_ref, cbeta_ref)

    o_ref[...] = _gate_project(att_ref[...], aux_ref, pool, conv, wout_ref, hp_ref[...])


def _prompt_merge(att, aux, hp, wout, poolw, pscale, cw, cb, cg, cbeta, n_batch, t_pad):
    nt = t_pad // TM
    hb = TM // HALO
    row_map = lambda b, i: (b * nt + i, 0)
    const = lambda b, i: (0, 0)
    d = hp.shape[1]

    def halo_map(col_block):
        return lambda b, i: (jnp.maximum((b * nt + i) * hb - 1, 0), col_block)

    return pl.pallas_call(
        _merge_kernel,
        out_shape=jax.ShapeDtypeStruct(hp.shape, F32),
        grid=(n_batch, nt),
        in_specs=[
            pl.BlockSpec((TM, D_A), row_map),
            pl.BlockSpec((TM, N_AUX), row_map),
            pl.BlockSpec((HALO, D_B), halo_map(A_UB // D_B)),
            pl.BlockSpec((HALO, D_C), halo_map(A_UC // D_C)),
            pl.BlockSpec((TM, d), row_map),
            pl.BlockSpec(wout.shape, const),
            pl.BlockSpec(poolw.shape, const),
            pl.BlockSpec((1, D_B), const),
            pl.BlockSpec(cw.shape, const),
            pl.BlockSpec((1, D_C), const),
            pl.BlockSpec((1, D_C), const),
            pl.BlockSpec((1, D_C), const),
        ],
        out_specs=pl.BlockSpec((TM, d), row_map),
        scratch_shapes=[pltpu.VMEM((HALO + TM, D_B), F32), pltpu.VMEM((HALO + TM, D_C), F32)],
        compiler_params=pltpu.CompilerParams(
            dimension_semantics=("parallel", "parallel"), vmem_limit_bytes=VMEM_LIMIT),
        name="prompt_merge",
    )(att, aux, aux, aux, hp, wout, poolw, pscale, cw, cb, cg, cbeta)


def _sample_index_kernel(pt_ref, *refs, n_pages, top_k):
    del pt_ref
    ki_pages = refs[:n_pages]
    qi_ref, wi_ref, kin_ref, mask_ref, i_scr, it_scr = refs[n_pages:]
    b = pl.program_id(0)
    n_q = i_scr.shape[0]
    past = n_pages * PAGE
    qi = qi_ref[...]
    w = wi_ref[...]

    @pl.when(b == 0)
    def _():
        i_scr[...] = jnp.full(i_scr.shape, -jnp.inf, F32)

    row0 = pl.multiple_of((b // SUBLANES) * SUBLANES, SUBLANES)
    mine = lax.broadcasted_iota(jnp.int32, (SUBLANES, LANES), 0) == b % SUBLANES

    def put_row(c0, row):
        grp = i_scr[pl.ds(row0, SUBLANES), c0:c0 + LANES]
        i_scr[pl.ds(row0, SUBLANES), c0:c0 + LANES] = jnp.where(mine, row, grp)

    for p in range(n_pages):
        kp = ki_pages[p][...].astype(BF16)
        d = jnp.dot(qi, kp, preferred_element_type=F32)
        put_row(p * PAGE, jnp.sum(w * jnp.maximum(d, 0.0), axis=0, keepdims=True))
    dn = jnp.sum(qi.astype(F32) * kin_ref[...], axis=1, keepdims=True)
    sn = jnp.sum(w * jnp.maximum(dn, 0.0), axis=0, keepdims=True)
    lane1 = lax.broadcasted_iota(jnp.int32, (1, LANES), 1)
    put_row(past, jnp.where(lane1 == 0, sn, -jnp.inf))

    @pl.when(b == n_q - 1)
    def _():
        width = i_scr.shape[1]
        for k in range(width // LANES):
            it_scr[k * LANES:(k + 1) * LANES, :] = i_scr[:, k * LANES:(k + 1) * LANES].T
        trivial = jnp.full((1, LANES), past + 1 <= top_k)
        lo, hi, jcut = (_as_column(r)[:, 0:1] for r in _select_bounds_t(it_scr, 1, width, top_k, trivial))
        for k in range(width // LANES):
            sel = _selected(i_scr[:, k * LANES:(k + 1) * LANES], k * LANES, lo, hi, jcut)
            mask_ref[:, k * LANES:(k + 1) * LANES] = jnp.where(sel, 1.0, 0.0)


def _sample_index(layer, page_table, cache_kidx_t, qi3, wi3, kin3, top_k):
    n_q, n_pages = page_table.shape
    width = (n_pages + 1) * PAGE
    kern = functools.partial(_sample_index_kernel, n_pages=n_pages, top_k=top_k)

    def page_map(p):
        return lambda b, pt: (layer, pt[b, p], 0, 0)

    in_specs = [pl.BlockSpec((None, None, D_IDX, PAGE), page_map(p)) for p in range(n_pages)]
    in_specs += [
        pl.BlockSpec((None, N_IDX_HEADS, D_IDX), lambda b, pt: (b, 0, 0)),
        pl.BlockSpec((None, N_IDX_HEADS, 1), lambda b, pt: (b, 0, 0)),
        pl.BlockSpec((None, 1, D_IDX), lambda b, pt: (b, 0, 0)),
    ]
    return pl.pallas_call(
        kern,
        out_shape=jax.ShapeDtypeStruct((n_q, width), F32),
        grid_spec=pltpu.PrefetchScalarGridSpec(
            num_scalar_prefetch=1,
            grid=(n_q,),
            in_specs=in_specs,
            out_specs=pl.BlockSpec((n_q, width), lambda b, pt: (0, 0)),
            scratch_shapes=[pltpu.VMEM((n_q, width), F32), pltpu.VMEM((width, n_q), F32)],
        ),
        compiler_params=pltpu.CompilerParams(
            dimension_semantics=("arbitrary",), vmem_limit_bytes=VMEM_LIMIT),
        name="sample_index",
    )(page_table, *([cache_kidx_t] * n_pages), qi3, wi3, kin3)


def _sample_attn_kernel(pt_ref, *refs, n_pages):
    del pt_ref
    k_pages = refs[:n_pages]
    v_pages = refs[n_pages:2 * n_pages]
    (q8_ref, qT_ref, kn_ref, vnT_ref, mask_ref, o_ref, oT_scr) = refs[2 * n_pages:]
    b = pl.program_id(0)
    n_q = pl.num_programs(0)
    lane = lax.broadcasted_iota(jnp.int32, (HEAD_DIM, LANES), 1)
    qT = qT_ref[...]
    vnT = vnT_ref[...]
    s_new = jnp.sum(q8_ref[...] * kn_ref[...], axis=1, keepdims=True)
    hit_new = mask_ref[n_pages:n_pages + 1, 0:1] > 0.5
    hits = [mask_ref[p:p + 1, :] > 0.5 for p in range(n_pages)]

    @pl.when(b == 0)
    def _():
        oT_scr[...] = jnp.zeros(oT_scr.shape, F32)

    for h in range(N_HEADS):
        qcol = jnp.broadcast_to(qT[:, h:h + 1], (HEAD_DIM, LANES))
        s = [jnp.where(hits[p], jnp.sum(k_pages[p][h] * qcol, axis=0, keepdims=True), MASKED)
             for p in range(n_pages)]
        sn = jnp.where(hit_new, s_new[h:h + 1, :], MASKED)
        mx = s[0]
        for p in range(1, n_pages):
            mx = jnp.maximum(mx, s[p])
        m = jnp.maximum(jnp.max(mx, axis=1, keepdims=True), sn)
        pn = jnp.exp2(sn - m)
        acc = jnp.zeros((HEAD_DIM, LANES), F32)
        lsum = jnp.zeros((1, LANES), F32)
        for p in range(n_pages):
            pp = jnp.exp2(s[p] - m)
            lsum = lsum + pp
            acc = acc + v_pages[p][h] * pp
        l = jnp.sum(lsum, axis=1, keepdims=True) + pn
        o_h = (jnp.sum(acc, axis=1, keepdims=True) + vnT[:, h:h + 1] * pn) / l
        rows = slice(h * HEAD_DIM, (h + 1) * HEAD_DIM)
        oT_scr[rows, :] = jnp.where(lane == b, o_h, oT_scr[rows, :])

    @pl.when(b == n_q - 1)
    def _():
        o_ref[...] = oT_scr[...].T


def _sample_attention(layer, page_table, cache_k_t, cache_v_t, q8, qT, kn, vnT, mask3):
    n_q, n_pages = page_table.shape
    assert n_q == LANES, "one output lane per sample query"
    kern = functools.partial(_sample_attn_kernel, n_pages=n_pages)

    def page_map(p):
        return lambda b, pt: (layer, pt[b, p], 0, 0, 0)

    page_spec = lambda p: pl.BlockSpec((None, None, N_HEADS, HEAD_DIM, PAGE), page_map(p))
    per_q = lambda b, pt: (b, 0, 0)
    in_specs = [page_spec(p) for p in range(n_pages)] + [page_spec(p) for p in range(n_pages)]
    in_specs += [
        pl.BlockSpec((None, N_HEADS, HEAD_DIM), per_q),
        pl.BlockSpec((None, HEAD_DIM, N_HEADS), per_q),
        pl.BlockSpec((None, N_HEADS, HEAD_DIM), per_q),
        pl.BlockSpec((None, HEAD_DIM, N_HEADS), per_q),
        pl.BlockSpec((None, n_pages + 1, PAGE), per_q),
    ]
    return pl.pallas_call(
        kern,
        out_shape=jax.ShapeDtypeStruct((n_q, D_A), F32),
        grid_spec=pltpu.PrefetchScalarGridSpec(
            num_scalar_prefetch=1,
            grid=(n_q,),
            in_specs=in_specs,
            out_specs=pl.BlockSpec((n_q, D_A), lambda b, pt: (0, 0)),
            scratch_shapes=[pltpu.VMEM((D_A, n_q), F32)],
        ),
        compiler_params=pltpu.CompilerParams(
            dimension_semantics=("arbitrary",), vmem_limit_bytes=VMEM_LIMIT),
        name="sample_attn",
    )(page_table, *([cache_k_t] * n_pages), *([cache_v_t] * n_pages), q8, qT, kn, vnT, mask3)


def _sample_merge_kernel(att_ref, aux_ref, sp_ref, sc_ref, hs_ref, wout_ref, poolw_ref, pscale_ref,
                         cw_ref, cb_ref, cg_ref, cbeta_ref, o_ref, *, pos):
    ub = aux_ref[:, A_UB:A_GB]
    uc = aux_ref[:, A_UC:A_GC]
    run = ub
    sums, cnts = [], []
    for j in range(1, POOL_WINDOWS[-1]):
        run = run + sp_ref[POOL_STATE - j]
        if j + 1 in POOL_WINDOWS:
            sums.append(run)
            cnts.append(float(min(pos + 1, j + 1)))
    pool = _pool_mix(sums, ub, cnts, poolw_ref, pscale_ref)

    y = cw_ref[CONV_STATE:CONV_WIDTH, :] * uc
    for j in range(CONV_STATE):
        y = y + cw_ref[j:j + 1, :] * sc_ref[j]
    conv = _conv_norm(y, cb_ref, cg_ref, cbeta_ref)
    o_ref[...] = _gate_project(att_ref[...], aux_ref, pool, conv, wout_ref, hs_ref[...])


def _sample_merge(att, aux, sp_t, sc_t, hs, wout, poolw, pscale, cw, cb, cg, cbeta, pos):
    args = (att, aux, sp_t, sc_t, hs, wout, poolw, pscale, cw, cb, cg, cbeta)
    full = lambda a: pl.BlockSpec(a.shape, lambda i, nd=a.ndim: (0,) * nd)
    return pl.pallas_call(
        functools.partial(_sample_merge_kernel, pos=pos),
        out_shape=jax.ShapeDtypeStruct(hs.shape, F32),
        grid=(1,),
        in_specs=[full(a) for a in args],
        out_specs=full(hs),
        compiler_params=pltpu.CompilerParams(
            dimension_semantics=("arbitrary",), vmem_limit_bytes=VMEM_LIMIT),
        name="sample_merge",
    )(*args)


def _rope_table(pos):
    pos = pos.astype(F32)

    def cs(half):
        inv = ROPE_THETA ** (-(jnp.arange(half, dtype=F32) / half))
        ang = pos[:, None] * inv[None, :]
        return jnp.cos(ang).T, jnp.sin(ang).T

    c8, s8 = cs(HEAD_DIM // 8)
    c4, s4 = cs(D_IDX // 8)
    return jnp.concatenate([c8, s8, c4, c4, -s4, s4], axis=0)


def _permute_w_in(w):
    d = w.shape[0]
    o = 0
    parts = {}
    for name, width in (("q", D_A), ("k", D_A), ("v", D_A), ("ga", D_A), ("qi", D_QI),
                        ("wi", N_IDX_HEADS), ("ki", D_IDX), ("ub", D_B), ("gb", D_B),
                        ("cv", D_C), ("cg", D_C), ("gc", D_C)):
        parts[name] = w[:, o:o + width]
        o += width
    pad = jnp.zeros((d, LANES - D_IDX - N_IDX_HEADS), w.dtype)
    order = ["q", "k", "v", "ga", "qi", "ub", "gb", "cv", "cg", "gc", "ki", "wi"]
    return jnp.concatenate([parts[n] for n in order] + [pad], axis=1).astype(BF16)


def _block_diag(pw):
    n_g, g, _ = pw.shape
    out = jnp.zeros((n_g * g, n_g * g), pw.dtype)
    for i in range(n_g):
        out = out.at[i * g:(i + 1) * g, i * g:(i + 1) * g].set(pw[i])
    return out.astype(BF16)


def kernel(x_prompt, x_sample, cache_k, cache_v, cache_kidx, state_pool, state_conv, page_table,
           meta_tokens, norm_g, w_in, q_norm_g, k_norm_g, pool_w, pool_scale, conv_w, conv_b,
           conv_norm_g, conv_norm_b, w_out):
    n_b, seq, d_model = x_prompt.shape
    n_q = x_sample.shape[0]
    depth = w_in.shape[0]
    n_pages = page_table.shape[1]
    past = n_pages * PAGE
    t_real = seq + N_META
    t_pad = -(-t_real // TM) * TM
    top_p = min(TOPK_MAX, t_real // 4)
    top_s = min(TOPK_MAX, (past + 1) // 4)

    meta = jnp.broadcast_to(meta_tokens[None].astype(F32), (n_b, N_META, d_model))
    tail = jnp.zeros((n_b, t_pad - t_real, d_model), F32)
    hp = jnp.concatenate([meta, x_prompt, tail], axis=1).reshape(n_b * t_pad, d_model)
    hs = x_sample.reshape(n_q, d_model)

    rope_p = _rope_table(jnp.arange(t_pad))
    rope_s = _rope_table(jnp.full((n_q,), past))
    cache_k_t = jnp.transpose(cache_k, (0, 1, 3, 4, 2))
    cache_v_t = jnp.transpose(cache_v, (0, 1, 3, 4, 2))
    cache_kidx_t = jnp.transpose(cache_kidx, (0, 1, 3, 2))

    outs = {name: [] for name in ("kp", "vp", "kip", "plp", "cvp", "ks", "vs", "kis", "pls", "cvs")}
    for l in range(depth):
        w_l = _permute_w_in(w_in[l])
        g_l = norm_g[l].reshape(1, d_model)
        qg = q_norm_g[l].reshape(HEAD_DIM, 1)
        kg = k_norm_g[l].reshape(HEAD_DIM, 1)
        wout = w_out[l].astype(BF16)
        poolw = _block_diag(pool_w[l])
        pscale = pool_scale[l].reshape(1, D_B)
        cw = conv_w[l]
        cb = conv_b[l].reshape(1, D_C)
        cg = conv_norm_g[l].reshape(1, D_C)
        cbeta = conv_norm_b[l].reshape(1, D_C)

        q, kTf, kT, vTf, vb, aux, qi, wk, kiTf, kiT = _project(
            hp, n_b, t_real, TM, g_l, w_l, rope_p, qg, kg)
        att = _prompt_attention(q, qi, wk, kT, kiT, vb, n_b, t_pad, t_real, top_p)
        hp = _prompt_merge(att, aux, hp, wout, poolw, pscale, cw, cb, cg, cbeta, n_b, t_pad)
        aux3 = aux.reshape(n_b, t_pad, N_AUX)
        heads_last = lambda xT: jnp.transpose(xT.reshape(xT.shape[0], N_HEADS, HEAD_DIM, -1), (0, 3, 1, 2))
        outs["kp"].append(heads_last(kTf))
        outs["vp"].append(heads_last(vTf))
        outs["kip"].append(jnp.swapaxes(kiTf, 1, 2))
        outs["plp"].append(aux3[:, t_real - POOL_STATE:t_real, A_UB:A_GB])
        outs["cvp"].append(aux3[:, t_real - CONV_STATE:t_real, A_UC:A_GC])

        q, kTf, _, vTf, _, aux, qi, wk, kiTf, _ = _project(hs, 1, n_q, n_q, g_l, w_l, rope_s, qg, kg)
        qi3 = qi.reshape(n_q, N_IDX_HEADS, D_IDX)
        wi3 = wk[:, WK_WI:WK_WI + N_IDX_HEADS].reshape(n_q, N_IDX_HEADS, 1)
        kin3 = wk[:, WK_KI:WK_KI + D_IDX].reshape(n_q, 1, D_IDX)
        mask = _sample_index(l, page_table, cache_kidx_t, qi3, wi3, kin3, top_s)
        q8 = q.astype(F32).reshape(n_q, N_HEADS, HEAD_DIM)
        k_new = heads_last(kTf).reshape(n_q, 1, N_HEADS, HEAD_DIM)
        v_new = heads_last(vTf).reshape(n_q, 1, N_HEADS, HEAD_DIM)
        att = _sample_attention(l, page_table, cache_k_t, cache_v_t, q8, jnp.swapaxes(q8, 1, 2),
                                k_new[:, 0], jnp.swapaxes(v_new[:, 0], 1, 2),
                                mask.reshape(n_q, n_pages + 1, PAGE))
        sp_t = jnp.swapaxes(state_pool[l], 0, 1)
        sc_t = jnp.swapaxes(state_conv[l], 0, 1)
        hs = _sample_merge(att, aux, sp_t, sc_t, hs, wout, poolw, pscale, cw, cb, cg, cbeta, past)
        outs["ks"].append(k_new)
        outs["vs"].append(v_new)
        outs["kis"].append(jnp.swapaxes(kiTf, 1, 2).reshape(n_q, 1, D_IDX))
        outs["pls"].append(jnp.concatenate([state_pool[l][:, 1:], aux[:, None, A_UB:A_GB]], axis=1))
        outs["cvs"].append(jnp.concatenate([state_conv[l][:, 1:], aux[:, None, A_UC:A_GC]], axis=1))

    y_prompt = hp.reshape(n_b, t_pad, d_model)[:, N_META:t_real]
    y_sample = hs.reshape(n_q, 1, d_model)
    st = lambda name: jnp.stack(outs[name], axis=0)
    return (y_prompt, y_sample, st("kp"), st("vp"), st("kip"), st("plp"), st("cvp"),
            st("ks"), st("vs"), st("kis"), st("pls"), st("cvs"))
```

```python
import functools

import jax
import jax.numpy as jnp
from jax import lax
from jax.experimental import pallas as pl
from jax.experimental.pallas import tpu as pltpu

F32 = jnp.float32
BF16 = jnp.bfloat16

LANES = 128
SUBLANES = 8
N_META = 16
HEAD_DIM = 64
N_HEADS = 8
D_A = N_HEADS * HEAD_DIM
N_IDX_HEADS = 8
D_IDX = 32
D_QI = N_IDX_HEADS * D_IDX
D_B = 256
D_C = 256
POOL_WINDOWS = (2, 4, 8, 16)
POOL_GROUP = D_B // len(POOL_WINDOWS)
POOL_STATE = 15
CONV_WIDTH = 31
CONV_STATE = CONV_WIDTH - 1
HALO = 32
TOPK_MAX = 256
ROPE_THETA = 500000.0
EPS = 1e-6
PAGE = 128
LOG2E = 1.4426950408889634
Q_SCALE = HEAD_DIM ** -0.5 * LOG2E
IDX_SCALE = (D_IDX ** -0.5) * (N_IDX_HEADS ** -0.5)

QB = 128
CK = 384
TM = 384
MAX_BISECT = 64
BIG_INDEX = 1.0e9
MASKED = -1.0e30
TINY = 1.1754944e-38
ACC_ROWS = 64
VMEM_LIMIT = 56 * 1024 * 1024

C_Q, C_K, C_V, C_GA, C_QI = 0, 512, 1024, 1536, 2048
C_UB, C_GB, C_CV, C_CG, C_GC, C_KI = 2304, 2560, 2816, 3072, 3328, 3584
N_PERM = C_KI + LANES
A_GA, A_UB, A_GB, A_UC, A_GC, N_AUX = 0, 512, 768, 1024, 1280, 1536
WK_KI, WK_WI = 0, D_IDX


def _sigmoid(x):
    return 0.5 * jnp.tanh(0.5 * x) + 0.5


def _silu(x):
    return x * _sigmoid(x)


def _proj_kernel(x_ref, g_ref, w_ref, rope_ref, qg_ref, kg_ref,
                 q_ref, kTf_ref, kT_ref, vTf_ref, vb_ref, aux_ref, qi_ref, wk_ref, kiTf_ref, kiT_ref,
                 t_scr):
    x = x_ref[...]
    ms = jnp.mean(x * x, axis=-1, keepdims=True)
    h = (x * lax.rsqrt(ms + EPS) * g_ref[...]).astype(BF16)

    def proj(c0, c1):
        return jnp.dot(h, w_ref[:, c0:c1], preferred_element_type=F32)

    cos8 = rope_ref[0:8, :]
    sin8 = rope_ref[8:16, :]
    cc4 = rope_ref[16:24, :]
    ss4 = rope_ref[24:32, :]

    zT = proj(C_Q, C_V).T
    for j in range(2 * N_HEADS):
        xh = zT[j * HEAD_DIM:(j + 1) * HEAD_DIM, :]
        g = qg_ref[...] if j < N_HEADS else kg_ref[...]
        xn = xh * lax.rsqrt(jnp.mean(xh * xh, axis=0, keepdims=True) + EPS) * g
        x1 = xn[0:8, :]
        x2 = xn[8:16, :]
        r0 = j * HEAD_DIM
        t_scr[r0:r0 + 8, :] = x1 * cos8 - x2 * sin8
        t_scr[r0 + 8:r0 + 16, :] = x2 * cos8 + x1 * sin8
        t_scr[r0 + 16:r0 + HEAD_DIM, :] = xn[16:HEAD_DIM, :]
    q_ref[...] = (t_scr[0:D_A, :].T * Q_SCALE).astype(BF16)
    kT = t_scr[D_A:2 * D_A, :]
    kTf_ref[...] = kT
    kT_ref[...] = kT.astype(BF16)

    zv = proj(C_V, C_GA)
    vTf_ref[...] = zv.T
    vb_ref[...] = zv.astype(BF16)

    aux_ref[:, A_GA:A_UB] = proj(C_GA, C_QI)
    aux_ref[:, A_UB:A_UC] = proj(C_UB, C_CV)
    zc = proj(C_CV, C_KI)
    aux_ref[:, A_UC:A_GC] = zc[:, 0:D_C] * _sigmoid(zc[:, D_C:2 * D_C])
    aux_ref[:, A_GC:N_AUX] = zc[:, 2 * D_C:3 * D_C]

    ziT = proj(C_QI, C_UB).T
    parts = []
    for hh in range(N_IDX_HEADS):
        r0 = hh * D_IDX
        x8 = ziT[r0:r0 + 8, :]
        parts.append(x8 * cc4 + pltpu.roll(x8, 4, 0) * ss4)
        parts.append(ziT[r0 + 8:r0 + D_IDX, :])
    qi_ref[...] = jnp.concatenate(parts, axis=0).T.astype(BF16)

    zwT = proj(C_KI, N_PERM).T
    x8 = zwT[0:8, :]
    ki = jnp.concatenate([x8 * cc4 + pltpu.roll(x8, 4, 0) * ss4, zwT[8:D_IDX, :]], axis=0)
    wi = zwT[D_IDX:D_IDX + N_IDX_HEADS, :] * IDX_SCALE
    wk_ref[...] = jnp.concatenate([ki, wi, zwT[D_IDX + N_IDX_HEADS:, :]], axis=0).T
    kiTf_ref[...] = ki
    kiT_ref[...] = jnp.concatenate([ki] * (LANES // D_IDX), axis=0).astype(BF16)


def _project(x2d, n_batch, t_real, tm, g, w, rope, qg, kg):
    rows = x2d.shape[0]
    t = rows // n_batch
    nt = t // tm
    d = x2d.shape[1]
    row_map = lambda b, i: (b * nt + i, 0)
    col_map = lambda b, i: (b, 0, i)
    const = lambda b, i: (0, 0)
    out_shape = (
        jax.ShapeDtypeStruct((rows, D_A), BF16),
        jax.ShapeDtypeStruct((n_batch, D_A, t_real), F32),
        jax.ShapeDtypeStruct((n_batch, D_A, t), BF16),
        jax.ShapeDtypeStruct((n_batch, D_A, t_real), F32),
        jax.ShapeDtypeStruct((rows, D_A), BF16),
        jax.ShapeDtypeStruct((rows, N_AUX), F32),
        jax.ShapeDtypeStruct((rows, D_QI), BF16),
        jax.ShapeDtypeStruct((rows, LANES), F32),
        jax.ShapeDtypeStruct((n_batch, D_IDX, t_real), F32),
        jax.ShapeDtypeStruct((n_batch, LANES, t), BF16),
    )
    out_specs = (
        pl.BlockSpec((tm, D_A), row_map),
        pl.BlockSpec((None, D_A, tm), col_map),
        pl.BlockSpec((None, D_A, tm), col_map),
        pl.BlockSpec((None, D_A, tm), col_map),
        pl.BlockSpec((tm, D_A), row_map),
        pl.BlockSpec((tm, N_AUX), row_map),
        pl.BlockSpec((tm, D_QI), row_map),
        pl.BlockSpec((tm, LANES), row_map),
        pl.BlockSpec((None, D_IDX, tm), col_map),
        pl.BlockSpec((None, LANES, tm), col_map),
    )
    in_specs = [
        pl.BlockSpec((tm, d), row_map),
        pl.BlockSpec((1, d), const),
        pl.BlockSpec((d, N_PERM), const),
        pl.BlockSpec((32, tm), lambda b, i: (0, i)),
        pl.BlockSpec((HEAD_DIM, 1), const),
        pl.BlockSpec((HEAD_DIM, 1), const),
    ]
    return pl.pallas_call(
        _proj_kernel,
        out_shape=out_shape,
        grid=(n_batch, nt),
        in_specs=in_specs,
        out_specs=out_specs,
        scratch_shapes=[pltpu.VMEM((2 * D_A, tm), F32)],
        compiler_params=pltpu.CompilerParams(
            dimension_semantics=("parallel", "parallel"), vmem_limit_bytes=VMEM_LIMIT),
        name="proj",
    )(x2d, g, w, rope, qg, kg)


def _chunk_fold(i_scr, n_ch, ck, fn, inits):
    rows = i_scr.shape[0]

    def body(c, accs):
        for k in range(ck // LANES):
            col = pl.multiple_of(c * ck + k * LANES, LANES)
            accs = fn(accs, i_scr[:, pl.ds(col, LANES)], col)
        return accs

    return lax.fori_loop(0, n_ch, body, tuple(jnp.full((rows, LANES), v, F32) for v in inits))


def _lane_total(acc):
    return jnp.dot(acc.astype(BF16), jnp.ones((LANES, LANES), BF16), preferred_element_type=F32)


def _replicate(col):
    return jnp.broadcast_to(col, (col.shape[0], LANES))


def _count_ge(i_scr, n_ch, ck, midb):
    (acc,) = _chunk_fold(i_scr, n_ch, ck,
                         lambda a, blk, col: (a[0] + jnp.where(blk >= midb, 1.0, 0.0),), (0.0,))
    return _lane_total(acc)


def _select_bounds(i_scr, n_ch, ck, top_k, trivial, jcut_scr):
    rows = i_scr.shape[0]
    kf = float(top_k)
    ninf = -jnp.inf
    pinf = jnp.inf
    land, lor, lnot = jnp.logical_and, jnp.logical_or, jnp.logical_not

    trivial = _replicate(trivial)
    mn, mx = _chunk_fold(
        i_scr, n_ch, ck,
        lambda a, blk, col: (jnp.minimum(a[0], jnp.where(blk == ninf, pinf, blk)), jnp.maximum(a[1], blk)),
        (pinf, ninf))
    cpos, cnn = _chunk_fold(
        i_scr, n_ch, ck,
        lambda a, blk, col: (a[0] + jnp.where(blk > 0.0, 1.0, 0.0), a[1] + jnp.where(blk >= 0.0, 1.0, 0.0)),
        (0.0, 0.0))
    row_min = _replicate(jnp.min(mn, axis=1, keepdims=True))
    row_max = _replicate(jnp.max(mx, axis=1, keepdims=True))
    c_pos = _lane_total(cpos)
    c_nn = _lane_total(cnn)
    c_top = _count_ge(i_scr, n_ch, ck, row_max)

    top_tie = land(c_top >= kf, lnot(trivial))
    zero_tie = land(land(c_pos < kf, c_nn >= kf), lnot(lor(trivial, top_tie)))
    neg_side = c_nn < kf
    lo0 = jnp.where(top_tie, row_max, jnp.where(zero_tie, 0.0, jnp.where(neg_side, row_min, TINY)))
    lo0 = jnp.where(trivial, row_min, lo0)
    hi0 = jnp.where(lor(trivial, top_tie), pinf,
                    jnp.where(zero_tie, jnp.where(c_nn == kf, pinf, TINY),
                              jnp.where(neg_side, 0.0, row_max)))
    pos_exact = land(lnot(neg_side), c_pos == kf)
    hi0 = jnp.where(land(pos_exact, lnot(lor(trivial, lor(top_tie, zero_tie)))), pinf, hi0)
    done0 = lor(lor(trivial, land(top_tie, c_top == kf)), lor(land(zero_tie, c_nn == kf), pos_exact))
    res0 = jnp.where(done0, 1.0, 0.0)
    act0 = jnp.where(lor(done0, lor(top_tie, zero_tie)), 0.0, 1.0)

    def bis_cond(st):
        return jnp.logical_and(st[0] < MAX_BISECT, st[1] > 0.0)

    def bis_body(st):
        it, _, lo, hi, res, act = st
        n_act = jnp.sum(act)
        mid = 0.5 * lo + 0.5 * hi
        ok = land(act > 0.0, land(mid > lo, mid < hi))
        c = _count_ge(i_scr, n_ch, ck, mid)
        eq = land(ok, c == kf)
        gt = land(ok, c > kf)
        lt = land(ok, c < kf)
        lo = jnp.where(lor(eq, gt), mid, lo)
        hi = jnp.where(eq, pinf, jnp.where(lt, mid, hi))
        res = jnp.where(eq, 1.0, res)
        act = jnp.where(lor(eq, lnot(ok)), 0.0, act)
        return it + 1, n_act, lo, hi, res, act

    _, _, lo, hi, res, _ = lax.while_loop(
        bis_cond, bis_body, (jnp.int32(0), jnp.sum(act0), lo0, hi0, res0, act0))

    jcut_scr[...] = jnp.full((rows, LANES), BIG_INDEX, F32)
    n_open = jnp.sum(1.0 - res)

    @pl.when(n_open > 0.0)
    def _():
        need = kf - _count_ge(i_scr, n_ch, ck, hi)
        lane = lax.broadcasted_iota(jnp.int32, (rows, LANES), 1)

        def count_band(jb):
            def fn(a, blk, col):
                idx = (lane + col).astype(F32)
                in_band = land(land(blk >= lo, blk < hi), idx <= jb)
                return (a[0] + jnp.where(in_band, 1.0, 0.0),)

            (acc,) = _chunk_fold(i_scr, n_ch, ck, fn, (0.0,))
            return _lane_total(acc)

        def j_cond(st):
            return jnp.logical_and(st[0] < 32, st[1] > 0.0)

        def j_body(st):
            it, _, jl, jh, jc, open_ = st
            n_still = jnp.sum(open_)
            jm = jnp.floor((jl + jh + 1.0) * 0.5)
            c = count_band(jm)
            is_open = open_ > 0.0
            eq = land(is_open, c == need)
            jl = jnp.where(land(is_open, c < need), jm, jl)
            jh = jnp.where(land(is_open, c > need), jm, jh)
            jc = jnp.where(eq, jm, jc)
            open_ = jnp.where(eq, 0.0, open_)
            return it + 1, n_still, jl, jh, jc, open_

        width = i_scr.shape[1]
        open0 = 1.0 - res
        full = lambda v: jnp.full((rows, LANES), v, F32)
        st0 = (jnp.int32(0), jnp.sum(open0), full(-1.0), full(float(width - 1)), full(BIG_INDEX), open0)
        jcut_scr[...] = lax.while_loop(j_cond, j_body, st0)[4]

    return lo[:, 0:1], hi[:, 0:1], jcut_scr[:, 0:1]


def _fold_t(it_scr, n_blk, ck, fn, inits):
    per = ck // LANES

    def body(c, accs):
        row = pl.multiple_of(c * ck, LANES)
        return fn(accs, it_scr[pl.ds(row, ck), :], row)

    def tail(c, accs):
        row = pl.multiple_of(c * LANES, LANES)
        return fn(accs, it_scr[pl.ds(row, LANES), :], row)

    accs = tuple(jnp.full((ACC_ROWS, LANES), v, F32) for v in inits)
    accs = lax.fori_loop(0, n_blk // per, body, accs)
    return lax.fori_loop((n_blk // per) * per, n_blk, tail, accs)


def _vreg_fold(x, op):
    return op(x.reshape(x.shape[0] // ACC_ROWS, ACC_ROWS, LANES), axis=0)


def _count_ge_t(it_scr, n_blk, ck, mid):
    (acc,) = _fold_t(
        it_scr, n_blk, ck,
        lambda a, blk, row: (a[0] + _vreg_fold(jnp.where(blk >= mid, 1.0, 0.0), jnp.sum),), (0.0,))
    return jnp.sum(acc, axis=0, keepdims=True)


def _select_bounds_t(it_scr, n_blk, ck, top_k, trivial):
    kf = float(top_k)
    ninf = -jnp.inf
    pinf = jnp.inf
    land, lor, lnot = jnp.logical_and, jnp.logical_or, jnp.logical_not
    passes = 4

    mn, mx = _fold_t(
        it_scr, n_blk, ck,
        lambda a, blk, row: (jnp.minimum(a[0], _vreg_fold(jnp.where(blk == ninf, pinf, blk), jnp.min)),
                             jnp.maximum(a[1], _vreg_fold(blk, jnp.max))),
        (pinf, ninf))
    cpos, cnn = _fold_t(
        it_scr, n_blk, ck,
        lambda a, blk, row: (a[0] + _vreg_fold(jnp.where(blk > 0.0, 1.0, 0.0), jnp.sum),
                             a[1] + _vreg_fold(jnp.where(blk >= 0.0, 1.0, 0.0), jnp.sum)),
        (0.0, 0.0))
    col_min = jnp.min(mn, axis=0, keepdims=True)
    col_max = jnp.max(mx, axis=0, keepdims=True)
    c_pos = jnp.sum(cpos, axis=0, keepdims=True)
    c_nn = jnp.sum(cnn, axis=0, keepdims=True)
    c_top = _count_ge_t(it_scr, n_blk, ck, col_max)

    top_tie = land(c_top >= kf, lnot(trivial))
    zero_tie = land(land(c_pos < kf, c_nn >= kf), lnot(lor(trivial, top_tie)))
    neg_side = c_nn < kf
    lo0 = jnp.where(top_tie, col_max, jnp.where(zero_tie, 0.0, jnp.where(neg_side, col_min, TINY)))
    lo0 = jnp.where(trivial, col_min, lo0)
    hi0 = jnp.where(lor(trivial, top_tie), pinf,
                    jnp.where(zero_tie, jnp.where(c_nn == kf, pinf, TINY),
                              jnp.where(neg_side, 0.0, col_max)))
    pos_exact = land(lnot(neg_side), c_pos == kf)
    hi0 = jnp.where(land(pos_exact, lnot(lor(trivial, lor(top_tie, zero_tie)))), pinf, hi0)
    done0 = lor(lor(trivial, land(top_tie, c_top == kf)), lor(land(zero_tie, c_nn == kf), pos_exact))
    res0 = jnp.where(done0, 1.0, 0.0)
    act0 = jnp.where(lor(done0, lor(top_tie, zero_tie)), 0.0, 1.0)

    def bis_cond(st):
        return jnp.logical_and(st[0] < MAX_BISECT, st[1] > 0.0)

    def bis_body(st):
        it, _, lo, hi, res, act = st
        for _ in range(passes):
            mid = 0.5 * lo + 0.5 * hi
            ok = land(act > 0.0, land(mid > lo, mid < hi))
            c = _count_ge_t(it_scr, n_blk, ck, mid)
            eq = land(ok, c == kf)
            gt = land(ok, c > kf)
            lt = land(ok, c < kf)
            lo = jnp.where(lor(eq, gt), mid, lo)
            hi = jnp.where(eq, pinf, jnp.where(lt, mid, hi))
            res = jnp.where(eq, 1.0, res)
            act = jnp.where(lor(eq, lnot(ok)), 0.0, act)
        return it + passes, jnp.sum(act), lo, hi, res, act

    _, _, lo, hi, res, _ = lax.while_loop(
        bis_cond, bis_body, (jnp.int32(0), jnp.sum(act0), lo0, hi0, res0, act0))

    def cut_band():
        need = kf - _count_ge_t(it_scr, n_blk, ck, hi)

        def count_band(j):
            def fn(a, blk, row):
                idx = (lax.broadcasted_iota(jnp.int32, blk.shape, 0) + row).astype(F32)
                in_band = land(land(blk >= lo, blk < hi), idx <= j)
                return (a[0] + _vreg_fold(jnp.where(in_band, 1.0, 0.0), jnp.sum),)

            (acc,) = _fold_t(it_scr, n_blk, ck, fn, (0.0,))
            return jnp.sum(acc, axis=0, keepdims=True)

        def j_cond(st):
            return jnp.logical_and(st[0] < 32, st[1] > 0.0)

        def j_body(st):
            it, _, jl, jh, jc, open_ = st
            jm = jnp.floor((jl + jh + 1.0) * 0.5)
            c = count_band(jm)
            is_open = open_ > 0.0
            eq = land(is_open, c == need)
            jl = jnp.where(land(is_open, c < need), jm, jl)
            jh = jnp.where(land(is_open, c > need), jm, jh)
            jc = jnp.where(eq, jm, jc)
            open_ = jnp.where(eq, 0.0, open_)
            return it + 1, jnp.sum(open_), jl, jh, jc, open_

        open0 = 1.0 - res
        full = lambda v: jnp.full((1, LANES), v, F32)
        st0 = (jnp.int32(0), jnp.sum(open0), full(-1.0), full(float(it_scr.shape[0] - 1)),
               full(BIG_INDEX), open0)
        return lax.while_loop(j_cond, j_body, st0)[4]

    jcut = lax.cond(jnp.sum(1.0 - res) > 0.0, cut_band, lambda: jnp.full((1, LANES), BIG_INDEX, F32))
    return lo, hi, jcut


def _paired_loop(n, body, group=3):
    def grouped(j, carry):
        for u in range(group):
            carry = body(group * j + u, carry)
        return carry

    lax.fori_loop(0, n // group, grouped, 0)
    lax.fori_loop((n // group) * group, n, body, 0)


def _as_column(row):
    return jnp.broadcast_to(row, (LANES, LANES)).T


def _selected(blk, col, lo, hi, jcut):
    lane = lax.broadcasted_iota(jnp.int32, blk.shape, 1)
    idx = (lane + col).astype(F32)
    return jnp.logical_or(blk >= hi, jnp.logical_and(blk >= lo, idx <= jcut))


def _attn_kernel(q_ref, qi_ref, wk_ref, kT_ref, kiT_ref, v_ref, o_ref,
                 i_scr, it_scr, s_scr, m_scr, l_scr, acc_scr, *, t_real, top_k):
    i = pl.program_id(1)
    n_ch = (i * QB) // CK + 1
    q_pos = i * QB + lax.broadcasted_iota(jnp.int32, (QB, 1), 0)
    lane = lax.broadcasted_iota(jnp.int32, (QB, LANES), 1)
    n_sub = CK // LANES

    qi = qi_ref[...]
    q = q_ref[...]
    qiz, qz = [], []
    per = LANES // D_IDX
    for h in range(N_IDX_HEADS):
        blk = qi[:, (h // per) * LANES:(h // per + 1) * LANES]
        qiz.append(jnp.where(lane // D_IDX == h % per, blk, jnp.zeros_like(blk)))
    for h in range(N_HEADS):
        blk = q[:, (h // 2) * LANES:(h // 2 + 1) * LANES]
        qz.append(jnp.where(lane // HEAD_DIM == h % 2, blk, jnp.zeros_like(blk)))
    w = wk_ref[:, WK_WI:WK_WI + N_IDX_HEADS]

    def score_chunk(c, carry):
        col = pl.multiple_of(c * CK, LANES)
        kc = kiT_ref[:, pl.ds(col, CK)]
        acc = jnp.zeros((QB, CK), F32)
        for h in range(N_IDX_HEADS):
            d = jnp.dot(qiz[h], kc, preferred_element_type=F32)
            acc = acc + w[:, h:h + 1] * jnp.maximum(d, 0.0)
        key_pos = col + lax.broadcasted_iota(jnp.int32, (QB, CK), 1)
        sc = jnp.where(key_pos <= q_pos, acc, -jnp.inf)
        i_scr[:, pl.ds(col, CK)] = sc
        for k in range(n_sub):
            row = pl.multiple_of(col + k * LANES, LANES)
            it_scr[pl.ds(row, LANES), :] = sc[:, k * LANES:(k + 1) * LANES].T
        return carry

    _paired_loop(n_ch, score_chunk)

    q_row = i * QB + lax.broadcasted_iota(jnp.int32, (1, LANES), 1)
    trivial = jnp.logical_or(q_row + 1 <= top_k, q_row >= t_real)
    lo, hi, jcut = (_as_column(r)[:, 0:1] for r in _select_bounds_t(it_scr, i + 1, CK, top_k, trivial))

    m_scr[...] = jnp.full(m_scr.shape, MASKED, F32)

    def logits_chunk(c, carry):
        col = pl.multiple_of(c * CK, LANES)
        sel = _selected(i_scr[:, pl.ds(col, CK)], col, lo, hi, jcut)
        bias = jnp.where(sel, 0.0, MASKED)
        for h in range(N_HEADS):
            pr = h // 2
            kc = kT_ref[pr * LANES:(pr + 1) * LANES, pl.ds(col, CK)]
            s = jnp.dot(qz[h], kc, preferred_element_type=F32) + bias
            s_scr[h, :, pl.ds(col, CK)] = s
            mx = s[:, 0:LANES]
            for k in range(1, n_sub):
                mx = jnp.maximum(mx, s[:, k * LANES:(k + 1) * LANES])
            m_scr[h] = jnp.maximum(m_scr[h], mx)
        return carry

    _paired_loop(n_ch, logits_chunk)

    for h in range(N_HEADS):
        m_scr[h] = jnp.broadcast_to(jnp.max(m_scr[h], axis=1, keepdims=True), (QB, LANES))
    l_scr[...] = jnp.zeros(l_scr.shape, F32)
    acc_scr[...] = jnp.zeros(acc_scr.shape, F32)

    def value_chunk(c, carry):
        col = pl.multiple_of(c * CK, LANES)
        for h in range(N_HEADS):
            pr = h // 2
            m = m_scr[h]
            ps = [jnp.exp2(s_scr[h, :, pl.ds(pl.multiple_of(col + k * LANES, LANES), LANES)] - m)
                  for k in range(n_sub)]
            lsum = ps[0]
            for k in range(1, n_sub):
                lsum = lsum + ps[k]
            l_scr[h] = l_scr[h] + lsum
            p = jnp.concatenate(ps, axis=1).astype(BF16)
            vc = v_ref[pl.ds(col, CK), pr * LANES:(pr + 1) * LANES]
            acc_scr[h] = acc_scr[h] + jnp.dot(p, vc, preferred_element_type=F32)
        return carry

    _paired_loop(n_ch, value_chunk)

    for pr in range(N_HEADS // 2):
        first = lane < HEAD_DIM
        num = jnp.where(first, acc_scr[2 * pr], acc_scr[2 * pr + 1])
        den = jnp.where(first, jnp.sum(l_scr[2 * pr], axis=1, keepdims=True),
                        jnp.sum(l_scr[2 * pr + 1], axis=1, keepdims=True))
        o_ref[:, pr * LANES:(pr + 1) * LANES] = num / den


def _prompt_attention(q, qi, wk, kT, kiT, vb, n_batch, t_pad, t_real, top_k):
    nq = t_pad // QB
    row_map = lambda b, i: (b * nq + i, 0)
    kern = functools.partial(_attn_kernel, t_real=t_real, top_k=top_k)
    return pl.pallas_call(
        kern,
        out_shape=jax.ShapeDtypeStruct((n_batch * t_pad, D_A), F32),
        grid=(n_batch, nq),
        in_specs=[
            pl.BlockSpec((QB, D_A), row_map),
            pl.BlockSpec((QB, D_QI), row_map),
            pl.BlockSpec((QB, LANES), row_map),
            pl.BlockSpec((None, D_A, t_pad), lambda b, i: (b, 0, 0)),
            pl.BlockSpec((None, LANES, t_pad), lambda b, i: (b, 0, 0)),
            pl.BlockSpec((t_pad, D_A), lambda b, i: (b, 0)),
        ],
        out_specs=pl.BlockSpec((QB, D_A), row_map),
        scratch_shapes=[
            pltpu.VMEM((QB, t_pad), F32),
            pltpu.VMEM((t_pad, QB), F32),
            pltpu.VMEM((N_HEADS, QB, t_pad), F32),
            pltpu.VMEM((N_HEADS, QB, LANES), F32),
            pltpu.VMEM((N_HEADS, QB, LANES), F32),
            pltpu.VMEM((N_HEADS, QB, LANES), F32),
        ],
        compiler_params=pltpu.CompilerParams(
            dimension_semantics=("parallel", "arbitrary"), vmem_limit_bytes=VMEM_LIMIT),
        name="prompt_attn",
    )(q, qi, wk, kT, kiT, vb)


def _gate_project(att, aux_ref, pool, conv, wout_ref, resid):
    ya = att * _silu(aux_ref[:, A_GA:A_UB])
    yb = pool * _silu(aux_ref[:, A_GB:A_UC])
    yc = conv * _silu(aux_ref[:, A_GC:N_AUX])
    y = jnp.concatenate([ya, yb, yc], axis=1).astype(BF16)
    return resid + jnp.dot(y, wout_ref[...], preferred_element_type=F32)


def _pool_mix(sums, u, cnts, poolw_ref, pscale_ref):
    lane = lax.broadcasted_iota(jnp.int32, u.shape, 1)
    d = sums[-1] / cnts[-1]
    for g in range(len(POOL_WINDOWS) - 2, -1, -1):
        d = jnp.where(lane < (g + 1) * POOL_GROUP, sums[g] / cnts[g], d)
    d = (d - u).astype(BF16)
    return jnp.dot(d, poolw_ref[...], preferred_element_type=F32) * pscale_ref[...]


def _conv_norm(y, cb_ref, cg_ref, cbeta_ref):
    y = y + cb_ref[...]
    mu = jnp.mean(y, axis=-1, keepdims=True)
    yc = y - mu
    var = jnp.mean(yc * yc, axis=-1, keepdims=True)
    return _silu(yc * lax.rsqrt(var + EPS) * cg_ref[...] + cbeta_ref[...])


def _row_windows(ext_scr, rot_scr, tm):
    n = rot_scr.shape[1]
    for r in range(1, SUBLANES):
        rot_scr[r - 1] = ext_scr[r:r + n, :]

    def window(off):
        r = off % SUBLANES
        a = off - r
        return ext_scr[a:a + tm, :] if r == 0 else rot_scr[r - 1, a:a + tm, :]

    return window


def _merge_kernel(att_ref, aux_ref, hub_ref, huc_ref, hp_ref, wout_ref, poolw_ref, pscale_ref,
                  cw_ref, cb_ref, cg_ref, cbeta_ref, o_ref, ub_scr, uc_scr, ub_rot, uc_rot):
    i = pl.program_id(1)
    tm = att_ref.shape[0]
    first = i == 0
    ub = aux_ref[:, A_UB:A_GB]
    uc = aux_ref[:, A_UC:A_GC]
    ub_scr[0:HALO, :] = jnp.where(first, 0.0, hub_ref[...])
    uc_scr[0:HALO, :] = jnp.where(first, 0.0, huc_ref[...])
    ub_scr[HALO:HALO + tm, :] = ub
    uc_scr[HALO:HALO + tm, :] = uc
    ub_win = _row_windows(ub_scr, ub_rot, tm)
    uc_win = _row_windows(uc_scr, uc_rot, tm)

    pos = i * tm + lax.broadcasted_iota(jnp.int32, (tm, 1), 0)
    run = ub
    sums, cnts = [], []
    for j in range(1, POOL_WINDOWS[-1]):
        run = run + ub_win(HALO - j)
        if j + 1 in POOL_WINDOWS:
            sums.append(run)
            cnts.append(jnp.minimum(pos + 1, j + 1).astype(F32))
    pool = _pool_mix(sums, ub, cnts, poolw_ref, pscale_ref)

    y = jnp.zeros((tm, D_C), F32)
    for j in range(CONV_WIDTH):
        y = y + cw_ref[j:j + 1, :] * uc_win(HALO - CONV_STATE + j)
    conv = _conv_norm(y, cb_ref, cg_ref, cbeta_ref)

    o_ref[...] = _gate_project(att_ref[...], aux_ref, pool, conv, wout_ref, hp_ref[...])


def _prompt_merge(att, aux, hp, wout, poolw, pscale, cw, cb, cg, cbeta, n_batch, t_pad):
    nt = t_pad // TM
    hb = TM // HALO
    row_map = lambda b, i: (b * nt + i, 0)
    const = lambda b, i: (0, 0)
    d = hp.shape[1]

    def halo_map(col_block):
        return lambda b, i: (jnp.maximum((b * nt + i) * hb - 1, 0), col_block)

    return pl.pallas_call(
        _merge_kernel,
        out_shape=jax.ShapeDtypeStruct(hp.shape, F32),
        grid=(n_batch, nt),
        in_specs=[
            pl.BlockSpec((TM, D_A), row_map),
            pl.BlockSpec((TM, N_AUX), row_map),
            pl.BlockSpec((HALO, D_B), halo_map(A_UB // D_B)),
            pl.BlockSpec((HALO, D_C), halo_map(A_UC // D_C)),
            pl.BlockSpec((TM, d), row_map),
            pl.BlockSpec(wout.shape, const),
            pl.BlockSpec(poolw.shape, const),
            pl.BlockSpec((1, D_B), const),
            pl.BlockSpec(cw.shape, const),
            pl.BlockSpec((1, D_C), const),
            pl.BlockSpec((1, D_C), const),
            pl.BlockSpec((1, D_C), const),
        ],
        out_specs=pl.BlockSpec((TM, d), row_map),
        scratch_shapes=[pltpu.VMEM((HALO + TM, D_B), F32), pltpu.VMEM((HALO + TM, D_C), F32),
                        pltpu.VMEM((SUBLANES - 1, HALO + TM - SUBLANES, D_B), F32),
                        pltpu.VMEM((SUBLANES - 1, HALO + TM - SUBLANES, D_C), F32)],
        compiler_params=pltpu.CompilerParams(
            dimension_semantics=("parallel", "parallel"), vmem_limit_bytes=VMEM_LIMIT),
        name="prompt_merge",
    )(att, aux, aux, aux, hp, wout, poolw, pscale, cw, cb, cg, cbeta)


def _sample_index_kernel(pt_ref, *refs, n_pages, top_k):
    del pt_ref
    ki_pages = refs[:n_pages]
    qi_ref, wi_ref, kin_ref, mask_ref, i_scr, it_scr = refs[n_pages:]
    b = pl.program_id(0)
    n_q = i_scr.shape[0]
    past = n_pages * PAGE
    qi = qi_ref[...]
    w = wi_ref[...]

    @pl.when(b == 0)
    def _():
        i_scr[...] = jnp.full(i_scr.shape, -jnp.inf, F32)

    row0 = pl.multiple_of((b // SUBLANES) * SUBLANES, SUBLANES)
    mine = lax.broadcasted_iota(jnp.int32, (SUBLANES, LANES), 0) == b % SUBLANES

    def put_row(c0, row):
        grp = i_scr[pl.ds(row0, SUBLANES), c0:c0 + LANES]
        i_scr[pl.ds(row0, SUBLANES), c0:c0 + LANES] = jnp.where(mine, row, grp)

    kp = jnp.concatenate([ki_pages[p][...] for p in range(n_pages)], axis=1).astype(BF16)
    d = jnp.dot(qi, kp, preferred_element_type=F32)
    sc = jnp.sum(w * jnp.maximum(d, 0.0), axis=0, keepdims=True)
    for p in range(n_pages):
        put_row(p * PAGE, sc[:, p * PAGE:(p + 1) * PAGE])
    dn = jnp.sum(qi.astype(F32) * kin_ref[...], axis=1, keepdims=True)
    sn = jnp.sum(w * jnp.maximum(dn, 0.0), axis=0, keepdims=True)
    lane1 = lax.broadcasted_iota(jnp.int32, (1, LANES), 1)
    put_row(past, jnp.where(lane1 == 0, sn, -jnp.inf))

    @pl.when(b == n_q - 1)
    def _():
        width = i_scr.shape[1]
        for k in range(width // LANES):
            it_scr[k * LANES:(k + 1) * LANES, :] = i_scr[:, k * LANES:(k + 1) * LANES].T
        trivial = jnp.full((1, LANES), past + 1 <= top_k)
        lo, hi, jcut = (_as_column(r)[:, 0:1] for r in _select_bounds_t(it_scr, width // LANES, width, top_k, trivial))
        for k in range(width // LANES):
            sel = _selected(i_scr[:, k * LANES:(k + 1) * LANES], k * LANES, lo, hi, jcut)
            mask_ref[:, k * LANES:(k + 1) * LANES] = jnp.where(sel, 1.0, 0.0)


def _sample_index(layer, page_table, cache_kidx_t, qi3, wi3, kin3, top_k):
    n_q, n_pages = page_table.shape
    width = (n_pages + 1) * PAGE
    kern = functools.partial(_sample_index_kernel, n_pages=n_pages, top_k=top_k)

    def page_map(p):
        return lambda b, pt: (layer, pt[b, p], 0, 0)

    in_specs = [pl.BlockSpec((None, None, D_IDX, PAGE), page_map(p)) for p in range(n_pages)]
    in_specs += [
        pl.BlockSpec((None, N_IDX_HEADS, D_IDX), lambda b, pt: (b, 0, 0)),
        pl.BlockSpec((None, N_IDX_HEADS, 1), lambda b, pt: (b, 0, 0)),
        pl.BlockSpec((None, 1, D_IDX), lambda b, pt: (b, 0, 0)),
    ]
    return pl.pallas_call(
        kern,
        out_shape=jax.ShapeDtypeStruct((n_q, width), F32),
        grid_spec=pltpu.PrefetchScalarGridSpec(
            num_scalar_prefetch=1,
            grid=(n_q,),
            in_specs=in_specs,
            out_specs=pl.BlockSpec((n_q, width), lambda b, pt: (0, 0)),
            scratch_shapes=[pltpu.VMEM((n_q, width), F32), pltpu.VMEM((width, n_q), F32)],
        ),
        compiler_params=pltpu.CompilerParams(
            dimension_semantics=("arbitrary",), vmem_limit_bytes=VMEM_LIMIT),
        name="sample_index",
    )(page_table, *([cache_kidx_t] * n_pages), qi3, wi3, kin3)


def _sample_attn_kernel(pt_ref, *refs, n_pages):
    del pt_ref
    k_pages = refs[:n_pages]
    v_pages = refs[n_pages:2 * n_pages]
    (q8_ref, qT_ref, kn_ref, vnT_ref, mask_ref, o_ref, oT_scr) = refs[2 * n_pages:]
    b = pl.program_id(0)
    n_q = pl.num_programs(0)
    lane = lax.broadcasted_iota(jnp.int32, (HEAD_DIM, LANES), 1)
    qT = qT_ref[...]
    vnT = vnT_ref[...]
    s_new = jnp.sum(q8_ref[...] * kn_ref[...], axis=1, keepdims=True)
    hit_new = mask_ref[n_pages:n_pages + 1, 0:1] > 0.5
    hits = [mask_ref[p:p + 1, :] > 0.5 for p in range(n_pages)]

    @pl.when(b == 0)
    def _():
        oT_scr[...] = jnp.zeros(oT_scr.shape, F32)

    for h in range(N_HEADS):
        qcol = jnp.broadcast_to(qT[:, h:h + 1], (HEAD_DIM, LANES))
        s = [jnp.where(hits[p], jnp.sum(k_pages[p][h] * qcol, axis=0, keepdims=True), MASKED)
             for p in range(n_pages)]
        sn = jnp.where(hit_new, s_new[h:h + 1, :], MASKED)
        mx = s[0]
        for p in range(1, n_pages):
            mx = jnp.maximum(mx, s[p])
        m = jnp.maximum(jnp.max(mx, axis=1, keepdims=True), sn)
        pn = jnp.exp2(sn - m)
        acc = jnp.zeros((HEAD_DIM, LANES), F32)
        lsum = jnp.zeros((1, LANES), F32)
        for p in range(n_pages):
            pp = jnp.exp2(s[p] - m)
            lsum = lsum + pp
            acc = acc + v_pages[p][h] * pp
        l = jnp.sum(lsum, axis=1, keepdims=True) + pn
        o_h = (jnp.sum(acc, axis=1, keepdims=True) + vnT[:, h:h + 1] * pn) / l
        rows = slice(h * HEAD_DIM, (h + 1) * HEAD_DIM)
        oT_scr[rows, :] = jnp.where(lane == b, o_h, oT_scr[rows, :])

    @pl.when(b == n_q - 1)
    def _():
        o_ref[...] = oT_scr[...].T


def _sample_attention(layer, page_table, cache_k_t, cache_v_t, q8, qT, kn, vnT, mask3):
    n_q, n_pages = page_table.shape
    assert n_q == LANES, "one output lane per sample query"
    kern = functools.partial(_sample_attn_kernel, n_pages=n_pages)

    def page_map(p):
        return lambda b, pt: (layer, pt[b, p], 0, 0, 0)

    page_spec = lambda p: pl.BlockSpec((None, None, N_HEADS, HEAD_DIM, PAGE), page_map(p))
    per_q = lambda b, pt: (b, 0, 0)
    in_specs = [page_spec(p) for p in range(n_pages)] + [page_spec(p) for p in range(n_pages)]
    in_specs += [
        pl.BlockSpec((None, N_HEADS, HEAD_DIM), per_q),
        pl.BlockSpec((None, HEAD_DIM, N_HEADS), per_q),
        pl.BlockSpec((None, N_HEADS, HEAD_DIM), per_q),
        pl.BlockSpec((None, HEAD_DIM, N_HEADS), per_q),
        pl.BlockSpec((None, n_pages + 1, PAGE), per_q),
    ]
    return pl.pallas_call(
        kern,
        out_shape=jax.ShapeDtypeStruct((n_q, D_A), F32),
        grid_spec=pltpu.PrefetchScalarGridSpec(
            num_scalar_prefetch=1,
            grid=(n_q,),
            in_specs=in_specs,
            out_specs=pl.BlockSpec((n_q, D_A), lambda b, pt: (0, 0)),
            scratch_shapes=[pltpu.VMEM((D_A, n_q), F32)],
        ),
        compiler_params=pltpu.CompilerParams(
            dimension_semantics=("arbitrary",), vmem_limit_bytes=VMEM_LIMIT),
        name="sample_attn",
    )(page_table, *([cache_k_t] * n_pages), *([cache_v_t] * n_pages), q8, qT, kn, vnT, mask3)


def _sample_merge_kernel(att_ref, aux_ref, sp_ref, sc_ref, hs_ref, wout_ref, poolw_ref, pscale_ref,
                         cw_ref, cb_ref, cg_ref, cbeta_ref, o_ref, *, pos):
    ub = aux_ref[:, A_UB:A_GB]
    uc = aux_ref[:, A_UC:A_GC]
    run = ub
    sums, cnts = [], []
    for j in range(1, POOL_WINDOWS[-1]):
        run = run + sp_ref[POOL_STATE - j]
        if j + 1 in POOL_WINDOWS:
            sums.append(run)
            cnts.append(float(min(pos + 1, j + 1)))
    pool = _pool_mix(sums, ub, cnts, poolw_ref, pscale_ref)

    y = cw_ref[CONV_STATE:CONV_WIDTH, :] * uc
    for j in range(CONV_STATE):
        y = y + cw_ref[j:j + 1, :] * sc_ref[j]
    conv = _conv_norm(y, cb_ref, cg_ref, cbeta_ref)
    o_ref[...] = _gate_project(att_ref[...], aux_ref, pool, conv, wout_ref, hs_ref[...])


def _sample_merge(att, aux, sp_t, sc_t, hs, wout, poolw, pscale, cw, cb, cg, cbeta, pos):
    args = (att, aux, sp_t, sc_t, hs, wout, poolw, pscale, cw, cb, cg, cbeta)
    full = lambda a: pl.BlockSpec(a.shape, lambda i, nd=a.ndim: (0,) * nd)
    return pl.pallas_call(
        functools.partial(_sample_merge_kernel, pos=pos),
        out_shape=jax.ShapeDtypeStruct(hs.shape, F32),
        grid=(1,),
        in_specs=[full(a) for a in args],
        out_specs=full(hs),
        compiler_params=pltpu.CompilerParams(
            dimension_semantics=("arbitrary",), vmem_limit_bytes=VMEM_LIMIT),
        name="sample_merge",
    )(*args)


def _rope_table(pos):
    pos = pos.astype(F32)

    def cs(half):
        inv = ROPE_THETA ** (-(jnp.arange(half, dtype=F32) / half))
        ang = pos[:, None] * inv[None, :]
        return jnp.cos(ang).T, jnp.sin(ang).T

    c8, s8 = cs(HEAD_DIM // 8)
    c4, s4 = cs(D_IDX // 8)
    return jnp.concatenate([c8, s8, c4, c4, -s4, s4], axis=0)


def _permute_w_in(w):
    d = w.shape[0]
    o = 0
    parts = {}
    for name, width in (("q", D_A), ("k", D_A), ("v", D_A), ("ga", D_A), ("qi", D_QI),
                        ("wi", N_IDX_HEADS), ("ki", D_IDX), ("ub", D_B), ("gb", D_B),
                        ("cv", D_C), ("cg", D_C), ("gc", D_C)):
        parts[name] = w[:, o:o + width]
        o += width
    pad = jnp.zeros((d, LANES - D_IDX - N_IDX_HEADS), w.dtype)
    order = ["q", "k", "v", "ga", "qi", "ub", "gb", "cv", "cg", "gc", "ki", "wi"]
    return jnp.concatenate([parts[n] for n in order] + [pad], axis=1).astype(BF16)


def _block_diag(pw):
    n_g, g, _ = pw.shape
    out = jnp.zeros((n_g * g, n_g * g), pw.dtype)
    for i in range(n_g):
        out = out.at[i * g:(i + 1) * g, i * g:(i + 1) * g].set(pw[i])
    return out.astype(BF16)


def kernel(x_prompt, x_sample, cache_k, cache_v, cache_kidx, state_pool, state_conv, page_table,
           meta_tokens, norm_g, w_in, q_norm_g, k_norm_g, pool_w, pool_scale, conv_w, conv_b,
           conv_norm_g, conv_norm_b, w_out):
    n_b, seq, d_model = x_prompt.shape
    n_q = x_sample.shape[0]
    depth = w_in.shape[0]
    n_pages = page_table.shape[1]
    past = n_pages * PAGE
    t_real = seq + N_META
    t_pad = -(-t_real // TM) * TM
    top_p = min(TOPK_MAX, t_real // 4)
    top_s = min(TOPK_MAX, (past + 1) // 4)

    meta = jnp.broadcast_to(meta_tokens[None].astype(F32), (n_b, N_META, d_model))
    tail = jnp.zeros((n_b, t_pad - t_real, d_model), F32)
    hp = jnp.concatenate([meta, x_prompt, tail], axis=1).reshape(n_b * t_pad, d_model)
    hs = x_sample.reshape(n_q, d_model)

    rope_p = _rope_table(jnp.arange(t_pad))
    rope_s = _rope_table(jnp.full((n_q,), past))
    cache_k_t = jnp.transpose(cache_k, (0, 1, 3, 4, 2))
    cache_v_t = jnp.transpose(cache_v, (0, 1, 3, 4, 2))
    cache_kidx_t = jnp.transpose(cache_kidx, (0, 1, 3, 2))

    outs = {name: [] for name in ("kp", "vp", "kip", "plp", "cvp", "ks", "vs", "kis", "pls", "cvs")}
    for l in range(depth):
        w_l = _permute_w_in(w_in[l])
        g_l = norm_g[l].reshape(1, d_model)
        qg = q_norm_g[l].reshape(HEAD_DIM, 1)
        kg = k_norm_g[l].reshape(HEAD_DIM, 1)
        wout = w_out[l].astype(BF16)
        poolw = _block_diag(pool_w[l])
        pscale = pool_scale[l].reshape(1, D_B)
        cw = conv_w[l]
        cb = conv_b[l].reshape(1, D_C)
        cg = conv_norm_g[l].reshape(1, D_C)
        cbeta = conv_norm_b[l].reshape(1, D_C)

        q, kTf, kT, vTf, vb, aux, qi, wk, kiTf, kiT = _project(
            hp, n_b, t_real, TM, g_l, w_l, rope_p, qg, kg)
        att = _prompt_attention(q, qi, wk, kT, kiT, vb, n_b, t_pad, t_real, top_p)
        hp = _prompt_merge(att, aux, hp, wout, poolw, pscale, cw, cb, cg, cbeta, n_b, t_pad)
        aux3 = aux.reshape(n_b, t_pad, N_AUX)
        heads_last = lambda xT: jnp.transpose(xT.reshape(xT.shape[0], N_HEADS, HEAD_DIM, -1), (0, 3, 1, 2))
        outs["kp"].append(heads_last(kTf))
        outs["vp"].append(heads_last(vTf))
        outs["kip"].append(jnp.swapaxes(kiTf, 1, 2))
        outs["plp"].append(aux3[:, t_real - POOL_STATE:t_real, A_UB:A_GB])
        outs["cvp"].append(aux3[:, t_real - CONV_STATE:t_real, A_UC:A_GC])

        q, kTf, _, vTf, _, aux, qi, wk, kiTf, _ = _project(hs, 1, n_q, n_q, g_l, w_l, rope_s, qg, kg)
        qi3 = qi.reshape(n_q, N_IDX_HEADS, D_IDX)
        wi3 = wk[:, WK_WI:WK_WI + N_IDX_HEADS].reshape(n_q, N_IDX_HEADS, 1)
        kin3 = wk[:, WK_KI:WK_KI + D_IDX].reshape(n_q, 1, D_IDX)
        mask = _sample_index(l, page_table, cache_kidx_t, qi3, wi3, kin3, top_s)
        q8 = q.astype(F32).reshape(n_q, N_HEADS, HEAD_DIM)
        k_new = heads_last(kTf).reshape(n_q, 1, N_HEADS, HEAD_DIM)
        v_new = heads_last(vTf).reshape(n_q, 1, N_HEADS, HEAD_DIM)
        att = _sample_attention(l, page_table, cache_k_t, cache_v_t, q8, jnp.swapaxes(q8, 1, 2),
                                k_new[:, 0], jnp.swapaxes(v_new[:, 0], 1, 2),
                                mask.reshape(n_q, n_pages + 1, PAGE))
        sp_t = jnp.swapaxes(state_pool[l], 0, 1)
        sc_t = jnp.swapaxes(state_conv[l], 0, 1)
        hs = _sample_merge(att, aux, sp_t, sc_t, hs, wout, poolw, pscale, cw, cb, cg, cbeta, past)
        outs["ks"].append(k_new)
        outs["vs"].append(v_new)
        outs["kis"].append(jnp.swapaxes(kiTf, 1, 2).reshape(n_q, 1, D_IDX))
        outs["pls"].append(jnp.concatenate([state_pool[l][:, 1:], aux[:, None, A_UB:A_GB]], axis=1))
        outs["cvs"].append(jnp.concatenate([state_conv[l][:, 1:], aux[:, None, A_UC:A_GC]], axis=1))

    y_prompt = hp.reshape(n_b, t_pad, d_model)[:, N_META:t_real]
    y_sample = hs.reshape(n_q, 1, d_model)
    st = lambda name: jnp.stack(outs[name], axis=0)
    return (y_prompt, y_sample, st("kp"), st("vp"), st("kip"), st("plp"), st("cvp"),
            st("ks"), st("vs"), st("kis"), st("pls"), st("cvs"))
```

```python
import functools

import jax
import jax.numpy as jnp
from jax import lax
from jax.experimental import pallas as pl
from jax.experimental.pallas import tpu as pltpu

F32 = jnp.float32
BF16 = jnp.bfloat16

LANES = 128
SUBLANES = 8
N_META = 16
HEAD_DIM = 64
N_HEADS = 8
D_A = N_HEADS * HEAD_DIM
N_IDX_HEADS = 8
D_IDX = 32
D_QI = N_IDX_HEADS * D_IDX
D_B = 256
D_C = 256
POOL_WINDOWS = (2, 4, 8, 16)
POOL_GROUP = D_B // len(POOL_WINDOWS)
POOL_STATE = 15
CONV_WIDTH = 31
CONV_STATE = CONV_WIDTH - 1
HALO = 32
TOPK_MAX = 256
ROPE_THETA = 500000.0
EPS = 1e-6
PAGE = 128
LOG2E = 1.4426950408889634
Q_SCALE = HEAD_DIM ** -0.5 * LOG2E
IDX_SCALE = (D_IDX ** -0.5) * (N_IDX_HEADS ** -0.5)

QB = 128
CK = 384
TM = 384
MAX_BISECT = 64
BIG_INDEX = 1.0e9
MASKED = -1.0e30
TINY = 1.1754944e-38
ACC_ROWS = 64
VMEM_LIMIT = 56 * 1024 * 1024

C_Q, C_K, C_V, C_GA, C_QI = 0, 512, 1024, 1536, 2048
C_UB, C_GB, C_CV, C_CG, C_GC, C_KI = 2304, 2560, 2816, 3072, 3328, 3584
N_PERM = C_KI + LANES
A_GA, A_UB, A_GB, A_UC, A_GC, N_AUX = 0, 512, 768, 1024, 1280, 1536
WK_KI, WK_WI = 0, D_IDX


def _sigmoid(x):
    return 0.5 * jnp.tanh(0.5 * x) + 0.5


def _silu(x):
    return x * _sigmoid(x)


def _proj_kernel(x_ref, g_ref, w_ref, rope_ref, qg_ref, kg_ref,
                 q_ref, kTf_ref, kT_ref, vTf_ref, vb_ref, aux_ref, qi_ref, wk_ref, kiTf_ref, kiT_ref,
                 t_scr):
    x = x_ref[...]
    ms = jnp.mean(x * x, axis=-1, keepdims=True)
    h = (x * lax.rsqrt(ms + EPS) * g_ref[...]).astype(BF16)

    def proj(c0, c1):
        return jnp.dot(h, w_ref[:, c0:c1], preferred_element_type=F32)

    cos8 = rope_ref[0:8, :]
    sin8 = rope_ref[8:16, :]
    cc4 = rope_ref[16:24, :]
    ss4 = rope_ref[24:32, :]

    zT = proj(C_Q, C_V).T
    for j in range(2 * N_HEADS):
        xh = zT[j * HEAD_DIM:(j + 1) * HEAD_DIM, :]
        g = qg_ref[...] if j < N_HEADS else kg_ref[...]
        xn = xh * lax.rsqrt(jnp.mean(xh * xh, axis=0, keepdims=True) + EPS) * g
        x1 = xn[0:8, :]
        x2 = xn[8:16, :]
        r0 = j * HEAD_DIM
        t_scr[r0:r0 + 8, :] = x1 * cos8 - x2 * sin8
        t_scr[r0 + 8:r0 + 16, :] = x2 * cos8 + x1 * sin8
        t_scr[r0 + 16:r0 + HEAD_DIM, :] = xn[16:HEAD_DIM, :]
    q_ref[...] = (t_scr[0:D_A, :].T * Q_SCALE).astype(BF16)
    kT = t_scr[D_A:2 * D_A, :]
    kTf_ref[...] = kT
    kT_ref[...] = kT.astype(BF16)

    zv = proj(C_V, C_GA)
    vTf_ref[...] = zv.T
    vb_ref[...] = zv.astype(BF16)

    aux_ref[:, A_GA:A_UB] = proj(C_GA, C_QI)
    aux_ref[:, A_UB:A_UC] = proj(C_UB, C_CV)
    zc = proj(C_CV, C_KI)
    aux_ref[:, A_UC:A_GC] = zc[:, 0:D_C] * _sigmoid(zc[:, D_C:2 * D_C])
    aux_ref[:, A_GC:N_AUX] = zc[:, 2 * D_C:3 * D_C]

    ziT = proj(C_QI, C_UB).T
    parts = []
    for hh in range(N_IDX_HEADS):
        r0 = hh * D_IDX
        x8 = ziT[r0:r0 + 8, :]
        parts.append(x8 * cc4 + pltpu.roll(x8, 4, 0) * ss4)
        parts.append(ziT[r0 + 8:r0 + D_IDX, :])
    qi_ref[...] = jnp.concatenate(parts, axis=0).T.astype(BF16)

    zwT = proj(C_KI, N_PERM).T
    x8 = zwT[0:8, :]
    ki = jnp.concatenate([x8 * cc4 + pltpu.roll(x8, 4, 0) * ss4, zwT[8:D_IDX, :]], axis=0)
    wi = zwT[D_IDX:D_IDX + N_IDX_HEADS, :] * IDX_SCALE
    wk_ref[...] = jnp.concatenate([ki, wi, zwT[D_IDX + N_IDX_HEADS:, :]], axis=0).T
    kiTf_ref[...] = ki
    kiT_ref[...] = jnp.concatenate([ki] * (LANES // D_IDX), axis=0).astype(BF16)


def _project(x2d, n_batch, t_real, tm, g, w, rope, qg, kg):
    rows = x2d.shape[0]
    t = rows // n_batch
    nt = t // tm
    d = x2d.shape[1]
    row_map = lambda b, i: (b * nt + i, 0)
    col_map = lambda b, i: (b, 0, i)
    const = lambda b, i: (0, 0)
    out_shape = (
        jax.ShapeDtypeStruct((rows, D_A), BF16),
        jax.ShapeDtypeStruct((n_batch, D_A, t_real), F32),
        jax.ShapeDtypeStruct((n_batch, D_A, t), BF16),
        jax.ShapeDtypeStruct((n_batch, D_A, t_real), F32),
        jax.ShapeDtypeStruct((rows, D_A), BF16),
        jax.ShapeDtypeStruct((rows, N_AUX), F32),
        jax.ShapeDtypeStruct((rows, D_QI), BF16),
        jax.ShapeDtypeStruct((rows, LANES), F32),
        jax.ShapeDtypeStruct((n_batch, D_IDX, t_real), F32),
        jax.ShapeDtypeStruct((n_batch, LANES, t), BF16),
    )
    out_specs = (
        pl.BlockSpec((tm, D_A), row_map),
        pl.BlockSpec((None, D_A, tm), col_map),
        pl.BlockSpec((None, D_A, tm), col_map),
        pl.BlockSpec((None, D_A, tm), col_map),
        pl.BlockSpec((tm, D_A), row_map),
        pl.BlockSpec((tm, N_AUX), row_map),
        pl.BlockSpec((tm, D_QI), row_map),
        pl.BlockSpec((tm, LANES), row_map),
        pl.BlockSpec((None, D_IDX, tm), col_map),
        pl.BlockSpec((None, LANES, tm), col_map),
    )
    in_specs = [
        pl.BlockSpec((tm, d), row_map),
        pl.BlockSpec((1, d), const),
        pl.BlockSpec((d, N_PERM), const),
        pl.BlockSpec((32, tm), lambda b, i: (0, i)),
        pl.BlockSpec((HEAD_DIM, 1), const),
        pl.BlockSpec((HEAD_DIM, 1), const),
    ]
    return pl.pallas_call(
        _proj_kernel,
        out_shape=out_shape,
        grid=(n_batch, nt),
        in_specs=in_specs,
        out_specs=out_specs,
        scratch_shapes=[pltpu.VMEM((2 * D_A, tm), F32)],
        compiler_params=pltpu.CompilerParams(
            dimension_semantics=("parallel", "parallel"), vmem_limit_bytes=VMEM_LIMIT),
        name="proj",
    )(x2d, g, w, rope, qg, kg)


def _chunk_fold(i_scr, n_ch, ck, fn, inits):
    rows = i_scr.shape[0]

    def body(c, accs):
        for k in range(ck // LANES):
            col = pl.multiple_of(c * ck + k * LANES, LANES)
            accs = fn(accs, i_scr[:, pl.ds(col, LANES)], col)
        return accs

    return lax.fori_loop(0, n_ch, body, tuple(jnp.full((rows, LANES), v, F32) for v in inits))


def _lane_total(acc):
    return jnp.dot(acc.astype(BF16), jnp.ones((LANES, LANES), BF16), preferred_element_type=F32)


def _replicate(col):
    return jnp.broadcast_to(col, (col.shape[0], LANES))


def _count_ge(i_scr, n_ch, ck, midb):
    (acc,) = _chunk_fold(i_scr, n_ch, ck,
                         lambda a, blk, col: (a[0] + jnp.where(blk >= midb, 1.0, 0.0),), (0.0,))
    return _lane_total(acc)


def _select_bounds(i_scr, n_ch, ck, top_k, trivial, jcut_scr):
    rows = i_scr.shape[0]
    kf = float(top_k)
    ninf = -jnp.inf
    pinf = jnp.inf
    land, lor, lnot = jnp.logical_and, jnp.logical_or, jnp.logical_not

    trivial = _replicate(trivial)
    mn, mx = _chunk_fold(
        i_scr, n_ch, ck,
        lambda a, blk, col: (jnp.minimum(a[0], jnp.where(blk == ninf, pinf, blk)), jnp.maximum(a[1], blk)),
        (pinf, ninf))
    cpos, cnn = _chunk_fold(
        i_scr, n_ch, ck,
        lambda a, blk, col: (a[0] + jnp.where(blk > 0.0, 1.0, 0.0), a[1] + jnp.where(blk >= 0.0, 1.0, 0.0)),
        (0.0, 0.0))
    row_min = _replicate(jnp.min(mn, axis=1, keepdims=True))
    row_max = _replicate(jnp.max(mx, axis=1, keepdims=True))
    c_pos = _lane_total(cpos)
    c_nn = _lane_total(cnn)
    c_top = _count_ge(i_scr, n_ch, ck, row_max)

    top_tie = land(c_top >= kf, lnot(trivial))
    zero_tie = land(land(c_pos < kf, c_nn >= kf), lnot(lor(trivial, top_tie)))
    neg_side = c_nn < kf
    lo0 = jnp.where(top_tie, row_max, jnp.where(zero_tie, 0.0, jnp.where(neg_side, row_min, TINY)))
    lo0 = jnp.where(trivial, row_min, lo0)
    hi0 = jnp.where(lor(trivial, top_tie), pinf,
                    jnp.where(zero_tie, jnp.where(c_nn == kf, pinf, TINY),
                              jnp.where(neg_side, 0.0, row_max)))
    pos_exact = land(lnot(neg_side), c_pos == kf)
    hi0 = jnp.where(land(pos_exact, lnot(lor(trivial, lor(top_tie, zero_tie)))), pinf, hi0)
    done0 = lor(lor(trivial, land(top_tie, c_top == kf)), lor(land(zero_tie, c_nn == kf), pos_exact))
    res0 = jnp.where(done0, 1.0, 0.0)
    act0 = jnp.where(lor(done0, lor(top_tie, zero_tie)), 0.0, 1.0)

    def bis_cond(st):
        return jnp.logical_and(st[0] < MAX_BISECT, st[1] > 0.0)

    def bis_body(st):
        it, _, lo, hi, res, act = st
        n_act = jnp.sum(act)
        mid = 0.5 * lo + 0.5 * hi
        ok = land(act > 0.0, land(mid > lo, mid < hi))
        c = _count_ge(i_scr, n_ch, ck, mid)
        eq = land(ok, c == kf)
        gt = land(ok, c > kf)
        lt = land(ok, c < kf)
        lo = jnp.where(lor(eq, gt), mid, lo)
        hi = jnp.where(eq, pinf, jnp.where(lt, mid, hi))
        res = jnp.where(eq, 1.0, res)
        act = jnp.where(lor(eq, lnot(ok)), 0.0, act)
        return it + 1, n_act, lo, hi, res, act

    _, _, lo, hi, res, _ = lax.while_loop(
        bis_cond, bis_body, (jnp.int32(0), jnp.sum(act0), lo0, hi0, res0, act0))

    jcut_scr[...] = jnp.full((rows, LANES), BIG_INDEX, F32)
    n_open = jnp.sum(1.0 - res)

    @pl.when(n_open > 0.0)
    def _():
        need = kf - _count_ge(i_scr, n_ch, ck, hi)
        lane = lax.broadcasted_iota(jnp.int32, (rows, LANES), 1)

        def count_band(jb):
            def fn(a, blk, col):
                idx = (lane + col).astype(F32)
                in_band = land(land(blk >= lo, blk < hi), idx <= jb)
                return (a[0] + jnp.where(in_band, 1.0, 0.0),)

            (acc,) = _chunk_fold(i_scr, n_ch, ck, fn, (0.0,))
            return _lane_total(acc)

        def j_cond(st):
            return jnp.logical_and(st[0] < 32, st[1] > 0.0)

        def j_body(st):
            it, _, jl, jh, jc, open_ = st
            n_still = jnp.sum(open_)
            jm = jnp.floor((jl + jh + 1.0) * 0.5)
            c = count_band(jm)
            is_open = open_ > 0.0
            eq = land(is_open, c == need)
            jl = jnp.where(land(is_open, c < need), jm, jl)
            jh = jnp.where(land(is_open, c > need), jm, jh)
            jc = jnp.where(eq, jm, jc)
            open_ = jnp.where(eq, 0.0, open_)
            return it + 1, n_still, jl, jh, jc, open_

        width = i_scr.shape[1]
        open0 = 1.0 - res
        full = lambda v: jnp.full((rows, LANES), v, F32)
        st0 = (jnp.int32(0), jnp.sum(open0), full(-1.0), full(float(width - 1)), full(BIG_INDEX), open0)
        jcut_scr[...] = lax.while_loop(j_cond, j_body, st0)[4]

    return lo[:, 0:1], hi[:, 0:1], jcut_scr[:, 0:1]


def _fold_t(it_scr, n_blk, ck, fn, inits):
    per = ck // LANES

    def body(c, accs):
        row = pl.multiple_of(c * ck, LANES)
        return fn(accs, it_scr[pl.ds(row, ck), :], row)

    def tail(c, accs):
        row = pl.multiple_of(c * LANES, LANES)
        return fn(accs, it_scr[pl.ds(row, LANES), :], row)

    accs = tuple(jnp.full((ACC_ROWS, LANES), v, F32) for v in inits)
    accs = lax.fori_loop(0, n_blk // per, body, accs)
    return lax.fori_loop((n_blk // per) * per, n_blk, tail, accs)


def _map_t(it_scr, n_blk, ck, fn):
    per = ck // LANES

    def body(c, carry):
        row = pl.multiple_of(c * ck, LANES)
        it_scr[pl.ds(row, ck), :] = fn(it_scr[pl.ds(row, ck), :], row)
        return carry

    def tail(c, carry):
        row = pl.multiple_of(c * LANES, LANES)
        it_scr[pl.ds(row, LANES), :] = fn(it_scr[pl.ds(row, LANES), :], row)
        return carry

    lax.fori_loop(0, n_blk // per, body, 0)
    lax.fori_loop((n_blk // per) * per, n_blk, tail, 0)


def _vreg_fold(x, op):
    return op(x.reshape(x.shape[0] // ACC_ROWS, ACC_ROWS, LANES), axis=0)


def _count_ge_t(it_scr, n_blk, ck, mid):
    (acc,) = _fold_t(
        it_scr, n_blk, ck,
        lambda a, blk, row: (a[0] + _vreg_fold(jnp.where(blk >= mid, 1.0, 0.0), jnp.sum),), (0.0,))
    return jnp.sum(acc, axis=0, keepdims=True)


def _select_bounds_t(it_scr, n_blk, ck, top_k, trivial):
    kf = float(top_k)
    ninf = -jnp.inf
    pinf = jnp.inf
    land, lor, lnot = jnp.logical_and, jnp.logical_or, jnp.logical_not
    passes = 4

    mn, mx = _fold_t(
        it_scr, n_blk, ck,
        lambda a, blk, row: (jnp.minimum(a[0], _vreg_fold(jnp.where(blk == ninf, pinf, blk), jnp.min)),
                             jnp.maximum(a[1], _vreg_fold(blk, jnp.max))),
        (pinf, ninf))
    cpos, cnn = _fold_t(
        it_scr, n_blk, ck,
        lambda a, blk, row: (a[0] + _vreg_fold(jnp.where(blk > 0.0, 1.0, 0.0), jnp.sum),
                             a[1] + _vreg_fold(jnp.where(blk >= 0.0, 1.0, 0.0), jnp.sum)),
        (0.0, 0.0))
    col_min = jnp.min(mn, axis=0, keepdims=True)
    col_max = jnp.max(mx, axis=0, keepdims=True)
    c_pos = jnp.sum(cpos, axis=0, keepdims=True)
    c_nn = jnp.sum(cnn, axis=0, keepdims=True)
    c_top = _count_ge_t(it_scr, n_blk, ck, col_max)

    top_tie = land(c_top >= kf, lnot(trivial))
    zero_tie = land(land(c_pos < kf, c_nn >= kf), lnot(lor(trivial, top_tie)))
    neg_side = c_nn < kf
    lo0 = jnp.where(top_tie, col_max, jnp.where(zero_tie, 0.0, jnp.where(neg_side, col_min, TINY)))
    lo0 = jnp.where(trivial, col_min, lo0)
    hi0 = jnp.where(lor(trivial, top_tie), pinf,
                    jnp.where(zero_tie, jnp.where(c_nn == kf, pinf, TINY),
                              jnp.where(neg_side, 0.0, col_max)))
    pos_exact = land(lnot(neg_side), c_pos == kf)
    hi0 = jnp.where(land(pos_exact, lnot(lor(trivial, lor(top_tie, zero_tie)))), pinf, hi0)
    done0 = lor(lor(trivial, land(top_tie, c_top == kf)), lor(land(zero_tie, c_nn == kf), pos_exact))
    res0 = jnp.where(done0, 1.0, 0.0)
    act0 = jnp.where(lor(done0, lor(top_tie, zero_tie)), 0.0, 1.0)

    def bis_cond(st):
        return jnp.logical_and(st[0] < MAX_BISECT, st[1] > 0.0)

    def bis_body(st):
        it, _, lo, hi, res, act = st
        for _ in range(passes):
            mid = 0.5 * lo + 0.5 * hi
            ok = land(act > 0.0, land(mid > lo, mid < hi))
            c = _count_ge_t(it_scr, n_blk, ck, mid)
            eq = land(ok, c == kf)
            gt = land(ok, c > kf)
            lt = land(ok, c < kf)
            lo = jnp.where(lor(eq, gt), mid, lo)
            hi = jnp.where(eq, pinf, jnp.where(lt, mid, hi))
            res = jnp.where(eq, 1.0, res)
            act = jnp.where(lor(eq, lnot(ok)), 0.0, act)
        return it + passes, jnp.sum(act), lo, hi, res, act

    _, _, lo, hi, res, _ = lax.while_loop(
        bis_cond, bis_body, (jnp.int32(0), jnp.sum(act0), lo0, hi0, res0, act0))

    def cut_band():
        need = kf - _count_ge_t(it_scr, n_blk, ck, hi)

        def to_index(blk, row):
            idx = (lax.broadcasted_iota(jnp.int32, blk.shape, 0) + row).astype(F32)
            return jnp.where(land(blk >= lo, blk < hi), idx, BIG_INDEX)

        _map_t(it_scr, n_blk, ck, to_index)

        def count_band(j):
            (acc,) = _fold_t(
                it_scr, n_blk, ck,
                lambda a, blk, row: (a[0] + _vreg_fold(jnp.where(blk <= j, 1.0, 0.0), jnp.sum),), (0.0,))
            return jnp.sum(acc, axis=0, keepdims=True)

        def j_cond(st):
            return jnp.logical_and(st[0] < 32, st[1] > 0.0)

        def j_body(st):
            it, _, jl, jh, jc, open_ = st
            jm = jnp.floor((jl + jh + 1.0) * 0.5)
            c = count_band(jm)
            is_open = open_ > 0.0
            eq = land(is_open, c == need)
            jl = jnp.where(land(is_open, c < need), jm, jl)
            jh = jnp.where(land(is_open, c > need), jm, jh)
            jc = jnp.where(eq, jm, jc)
            open_ = jnp.where(eq, 0.0, open_)
            return it + 1, jnp.sum(open_), jl, jh, jc, open_

        open0 = 1.0 - res
        full = lambda v: jnp.full((1, LANES), v, F32)
        st0 = (jnp.int32(0), jnp.sum(open0), full(-1.0), full(float(it_scr.shape[0] - 1)),
               full(BIG_INDEX), open0)
        return lax.while_loop(j_cond, j_body, st0)[4]

    jcut = lax.cond(jnp.sum(1.0 - res) > 0.0, cut_band, lambda: jnp.full((1, LANES), BIG_INDEX, F32))
    return lo, hi, jcut


def _paired_loop(n, body, group=3):
    def grouped(j, carry):
        for u in range(group):
            carry = body(group * j + u, carry)
        return carry

    lax.fori_loop(0, n // group, grouped, 0)
    lax.fori_loop((n // group) * group, n, body, 0)


def _as_column(row):
    return jnp.broadcast_to(row, (LANES, LANES)).T


def _selected(blk, col, lo, hi, jcut):
    lane = lax.broadcasted_iota(jnp.int32, blk.shape, 1)
    idx = (lane + col).astype(F32)
    return jnp.logical_or(blk >= hi, jnp.logical_and(blk >= lo, idx <= jcut))


def _attn_kernel(pt_ref, q_ref, qi_ref, wk_ref, kT_ref, kiT_ref, v_ref, *rest, t_real, top_k, n_pages):
    del pt_ref
    k_pages = rest[:n_pages]
    v_pages = rest[n_pages:2 * n_pages]
    (q8_ref, qT_ref, kn_ref, vnT_ref, mask_ref, o_ref, os_ref,
     i_scr, it_scr, s_scr, m_scr, l_scr, acc_scr, oT_scr) = rest[2 * n_pages:]
    i = pl.program_id(1)
    n_ch = (i * QB) // CK + 1
    q_pos = i * QB + lax.broadcasted_iota(jnp.int32, (QB, 1), 0)
    lane = lax.broadcasted_iota(jnp.int32, (QB, LANES), 1)
    n_sub = CK // LANES

    qi = qi_ref[...]
    q = q_ref[...]
    qiz, qz = [], []
    per = LANES // D_IDX
    for h in range(N_IDX_HEADS):
        blk = qi[:, (h // per) * LANES:(h // per + 1) * LANES]
        qiz.append(jnp.where(lane // D_IDX == h % per, blk, jnp.zeros_like(blk)))
    for h in range(N_HEADS):
        blk = q[:, (h // 2) * LANES:(h // 2 + 1) * LANES]
        qz.append(jnp.where(lane // HEAD_DIM == h % 2, blk, jnp.zeros_like(blk)))
    w = wk_ref[:, WK_WI:WK_WI + N_IDX_HEADS]

    def score_chunk(c, carry):
        col = pl.multiple_of(c * CK, LANES)
        kc = kiT_ref[:, pl.ds(col, CK)]
        acc = jnp.zeros((QB, CK), F32)
        for h in range(N_IDX_HEADS):
            d = jnp.dot(qiz[h], kc, preferred_element_type=F32)
            acc = acc + w[:, h:h + 1] * jnp.maximum(d, 0.0)
        key_pos = col + lax.broadcasted_iota(jnp.int32, (QB, CK), 1)
        sc = jnp.where(key_pos <= q_pos, acc, -jnp.inf)
        i_scr[:, pl.ds(col, CK)] = sc
        for k in range(n_sub):
            row = pl.multiple_of(col + k * LANES, LANES)
            it_scr[pl.ds(row, LANES), :] = sc[:, k * LANES:(k + 1) * LANES].T
        return carry

    _paired_loop(n_ch, score_chunk)

    q_row = i * QB + lax.broadcasted_iota(jnp.int32, (1, LANES), 1)
    trivial = jnp.logical_or(q_row + 1 <= top_k, q_row >= t_real)
    lo, hi, jcut = (_as_column(r)[:, 0:1] for r in _select_bounds_t(it_scr, i + 1, CK, top_k, trivial))

    m_scr[...] = jnp.full(m_scr.shape, MASKED, F32)

    def logits_chunk(c, carry):
        col = pl.multiple_of(c * CK, LANES)
        sel = _selected(i_scr[:, pl.ds(col, CK)], col, lo, hi, jcut)
        bias = jnp.where(sel, 0.0, MASKED)
        for h in range(N_HEADS):
            pr = h // 2
            kc = kT_ref[pr * LANES:(pr + 1) * LANES, pl.ds(col, CK)]
            s = jnp.dot(qz[h], kc, preferred_element_type=F32) + bias
            s_scr[h, :, pl.ds(col, CK)] = s
            mx = s[:, 0:LANES]
            for k in range(1, n_sub):
                mx = jnp.maximum(mx, s[:, k * LANES:(k + 1) * LANES])
            m_scr[h] = jnp.maximum(m_scr[h], mx)
        return carry

    _paired_loop(n_ch, logits_chunk)

    for h in range(N_HEADS):
        m_scr[h] = jnp.broadcast_to(jnp.max(m_scr[h], axis=1, keepdims=True), (QB, LANES))
    l_scr[...] = jnp.zeros(l_scr.shape, F32)
    acc_scr[...] = jnp.zeros(acc_scr.shape, F32)

    def value_chunk(c, carry):
        col = pl.multiple_of(c * CK, LANES)
        for h in range(N_HEADS):
            pr = h // 2
            m = m_scr[h]
            ps = [jnp.exp2(s_scr[h, :, pl.ds(pl.multiple_of(col + k * LANES, LANES), LANES)] - m)
                  for k in range(n_sub)]
            lsum = ps[0]
            for k in range(1, n_sub):
                lsum = lsum + ps[k]
            l_scr[h] = l_scr[h] + lsum
            p = jnp.concatenate(ps, axis=1).astype(BF16)
            vc = v_ref[pl.ds(col, CK), pr * LANES:(pr + 1) * LANES]
            acc_scr[h] = acc_scr[h] + jnp.dot(p, vc, preferred_element_type=F32)
        return carry

    _paired_loop(n_ch, value_chunk)

    for pr in range(N_HEADS // 2):
        first = lane < HEAD_DIM
        num = jnp.where(first, acc_scr[2 * pr], acc_scr[2 * pr + 1])
        den = jnp.where(first, jnp.sum(l_scr[2 * pr], axis=1, keepdims=True),
                        jnp.sum(l_scr[2 * pr + 1], axis=1, keepdims=True))
        o_ref[:, pr * LANES:(pr + 1) * LANES] = num / den

    step = pl.program_id(0) * pl.num_programs(1) + i
    n_steps = pl.num_programs(0) * pl.num_programs(1)

    @pl.when(step == 0)
    def _():
        oT_scr[...] = jnp.zeros(oT_scr.shape, F32)

    @pl.when(step < os_ref.shape[0])
    def _():
        _sample_attend(step, k_pages, v_pages, q8_ref, qT_ref, kn_ref, vnT_ref, mask_ref, oT_scr)

    @pl.when(step == n_steps - 1)
    def _():
        os_ref[...] = oT_scr[...].T


def _attention(layer, page_table, q, qi, wk, kT, kiT, vb, cache_k_t, cache_v_t, q8, qT, kn, vnT, mask3,
               n_batch, t_pad, t_real, top_k):
    nq = t_pad // QB
    n_q, n_pages = page_table.shape
    assert n_q == LANES, "one output lane per sample query"
    assert n_batch * nq >= n_q, "one sample query per grid step"
    kern = functools.partial(_attn_kernel, t_real=t_real, top_k=top_k, n_pages=n_pages)
    row_map = lambda b, i, pt: (b * nq + i, 0)
    query = lambda b, i: jnp.minimum(b * nq + i, n_q - 1)
    per_q = lambda b, i, pt: (query(b, i), 0, 0)
    once = pl.Buffered(1)

    def page_spec(p):
        return pl.BlockSpec((None, None, N_HEADS, HEAD_DIM, PAGE),
                            lambda b, i, pt: (layer, pt[query(b, i), p], 0, 0, 0))

    in_specs = [
        pl.BlockSpec((QB, D_A), row_map),
        pl.BlockSpec((QB, D_QI), row_map),
        pl.BlockSpec((QB, LANES), row_map),
        pl.BlockSpec((None, D_A, t_pad), lambda b, i, pt: (b, 0, 0), pipeline_mode=once),
        pl.BlockSpec((None, LANES, t_pad), lambda b, i, pt: (b, 0, 0), pipeline_mode=once),
        pl.BlockSpec((t_pad, D_A), lambda b, i, pt: (b, 0), pipeline_mode=once),
    ]
    in_specs += [page_spec(p) for p in range(n_pages)] + [page_spec(p) for p in range(n_pages)]
    in_specs += [
        pl.BlockSpec((None, N_HEADS, HEAD_DIM), per_q),
        pl.BlockSpec((None, HEAD_DIM, N_HEADS), per_q),
        pl.BlockSpec((None, N_HEADS, HEAD_DIM), per_q),
        pl.BlockSpec((None, HEAD_DIM, N_HEADS), per_q),
        pl.BlockSpec((None, n_pages + 1, PAGE), per_q),
    ]
    return pl.pallas_call(
        kern,
        out_shape=(jax.ShapeDtypeStruct((n_batch * t_pad, D_A), F32),
                   jax.ShapeDtypeStruct((n_q, D_A), F32)),
        grid_spec=pltpu.PrefetchScalarGridSpec(
            num_scalar_prefetch=1,
            grid=(n_batch, nq),
            in_specs=in_specs,
            out_specs=(pl.BlockSpec((QB, D_A), row_map),
                       pl.BlockSpec((n_q, D_A), lambda b, i, pt: (0, 0))),
            scratch_shapes=[
                pltpu.VMEM((QB, t_pad), F32),
                pltpu.VMEM((t_pad, QB), F32),
                pltpu.VMEM((N_HEADS, QB, t_pad), F32),
                pltpu.VMEM((N_HEADS, QB, LANES), F32),
                pltpu.VMEM((N_HEADS, QB, LANES), F32),
                pltpu.VMEM((N_HEADS, QB, LANES), F32),
                pltpu.VMEM((D_A, n_q), F32),
            ],
        ),
        compiler_params=pltpu.CompilerParams(
            dimension_semantics=("arbitrary", "arbitrary"), vmem_limit_bytes=VMEM_LIMIT),
        name="attention",
    )(page_table, q, qi, wk, kT, kiT, vb, *([cache_k_t] * n_pages), *([cache_v_t] * n_pages),
      q8, qT, kn, vnT, mask3)


def _gate_project(att, aux_ref, pool, conv, wout_ref, resid):
    ya = att * _silu(aux_ref[:, A_GA:A_UB])
    yb = pool * _silu(aux_ref[:, A_GB:A_UC])
    yc = conv * _silu(aux_ref[:, A_GC:N_AUX])
    y = jnp.concatenate([ya, yb, yc], axis=1).astype(BF16)
    return resid + jnp.dot(y, wout_ref[...], preferred_element_type=F32)


def _pool_mix(sums, u, cnts, poolw_ref, pscale_ref):
    lane = lax.broadcasted_iota(jnp.int32, u.shape, 1)
    d = sums[-1] / cnts[-1]
    for g in range(len(POOL_WINDOWS) - 2, -1, -1):
        d = jnp.where(lane < (g + 1) * POOL_GROUP, sums[g] / cnts[g], d)
    d = (d - u).astype(BF16)
    return jnp.dot(d, poolw_ref[...], preferred_element_type=F32) * pscale_ref[...]


def _conv_norm(y, cb_ref, cg_ref, cbeta_ref):
    y = y + cb_ref[...]
    mu = jnp.mean(y, axis=-1, keepdims=True)
    yc = y - mu
    var = jnp.mean(yc * yc, axis=-1, keepdims=True)
    return _silu(yc * lax.rsqrt(var + EPS) * cg_ref[...] + cbeta_ref[...])


def _row_windows(ext_scr, rot_scr, tm):
    n = rot_scr.shape[1]
    for r in range(1, SUBLANES):
        rot_scr[r - 1] = ext_scr[r:r + n, :]

    def window(off):
        r = off % SUBLANES
        a = off - r
        return ext_scr[a:a + tm, :] if r == 0 else rot_scr[r - 1, a:a + tm, :]

    return window


def _merge_kernel(att_ref, aux_ref, hub_ref, huc_ref, hp_ref, wout_ref, poolw_ref, pscale_ref,
                  cw_ref, cb_ref, cg_ref, cbeta_ref, o_ref, ub_scr, uc_scr, ub_rot, uc_rot):
    i = pl.program_id(1)
    tm = att_ref.shape[0]
    first = i == 0
    ub = aux_ref[:, A_UB:A_GB]
    uc = aux_ref[:, A_UC:A_GC]
    ub_scr[0:HALO, :] = jnp.where(first, 0.0, hub_ref[...])
    uc_scr[0:HALO, :] = jnp.where(first, 0.0, huc_ref[...])
    ub_scr[HALO:HALO + tm, :] = ub
    uc_scr[HALO:HALO + tm, :] = uc
    ub_win = _row_windows(ub_scr, ub_rot, tm)
    uc_win = _row_windows(uc_scr, uc_rot, tm)

    pos = i * tm + lax.broadcasted_iota(jnp.int32, (tm, 1), 0)
    run = ub
    sums, cnts = [], []
    for j in range(1, POOL_WINDOWS[-1]):
        run = run + ub_win(HALO - j)
        if j + 1 in POOL_WINDOWS:
            sums.append(run)
            cnts.append(jnp.minimum(pos + 1, j + 1).astype(F32))
    pool = _pool_mix(sums, ub, cnts, poolw_ref, pscale_ref)

    y = jnp.zeros((tm, D_C), F32)
    for j in range(CONV_WIDTH):
        y = y + cw_ref[j:j + 1, :] * uc_win(HALO - CONV_STATE + j)
    conv = _conv_norm(y, cb_ref, cg_ref, cbeta_ref)

    o_ref[...] = _gate_project(att_ref[...], aux_ref, pool, conv, wout_ref, hp_ref[...])


def _prompt_merge(att, aux, hp, wout, poolw, pscale, cw, cb, cg, cbeta, n_batch, t_pad):
    nt = t_pad // TM
    hb = TM // HALO
    row_map = lambda b, i: (b * nt + i, 0)
    const = lambda b, i: (0, 0)
    d = hp.shape[1]

    def halo_map(col_block):
        return lambda b, i: (jnp.maximum((b * nt + i) * hb - 1, 0), col_block)

    return pl.pallas_call(
        _merge_kernel,
        out_shape=jax.ShapeDtypeStruct(hp.shape, F32),
        grid=(n_batch, nt),
        in_specs=[
            pl.BlockSpec((TM, D_A), row_map),
            pl.BlockSpec((TM, N_AUX), row_map),
            pl.BlockSpec((HALO, D_B), halo_map(A_UB // D_B)),
            pl.BlockSpec((HALO, D_C), halo_map(A_UC // D_C)),
            pl.BlockSpec((TM, d), row_map),
            pl.BlockSpec(wout.shape, const),
            pl.BlockSpec(poolw.shape, const),
            pl.BlockSpec((1, D_B), const),
            pl.BlockSpec(cw.shape, const),
            pl.BlockSpec((1, D_C), const),
            pl.BlockSpec((1, D_C), const),
            pl.BlockSpec((1, D_C), const),
        ],
        out_specs=pl.BlockSpec((TM, d), row_map),
        scratch_shapes=[pltpu.VMEM((HALO + TM, D_B), F32), pltpu.VMEM((HALO + TM, D_C), F32),
                        pltpu.VMEM((SUBLANES - 1, HALO + TM - SUBLANES, D_B), F32),
                        pltpu.VMEM((SUBLANES - 1, HALO + TM - SUBLANES, D_C), F32)],
        compiler_params=pltpu.CompilerParams(
            dimension_semantics=("parallel", "parallel"), vmem_limit_bytes=VMEM_LIMIT),
        name="prompt_merge",
    )(att, aux, aux, aux, hp, wout, poolw, pscale, cw, cb, cg, cbeta)


def _sample_index_kernel(pt_ref, *refs, n_pages, top_k):
    del pt_ref
    ki_pages = refs[:n_pages]
    qi_ref, wi_ref, kin_ref, mask_ref, i_scr, it_scr = refs[n_pages:]
    b = pl.program_id(0)
    n_q = i_scr.shape[0]
    past = n_pages * PAGE
    qi = qi_ref[...]
    w = wi_ref[...]

    @pl.when(b == 0)
    def _():
        i_scr[...] = jnp.full(i_scr.shape, -jnp.inf, F32)

    row0 = pl.multiple_of((b // SUBLANES) * SUBLANES, SUBLANES)
    mine = lax.broadcasted_iota(jnp.int32, (SUBLANES, LANES), 0) == b % SUBLANES

    def put_row(c0, row):
        grp = i_scr[pl.ds(row0, SUBLANES), c0:c0 + LANES]
        i_scr[pl.ds(row0, SUBLANES), c0:c0 + LANES] = jnp.where(mine, row, grp)

    kp = jnp.concatenate([ki_pages[p][...] for p in range(n_pages)], axis=1).astype(BF16)
    d = jnp.dot(qi, kp, preferred_element_type=F32)
    sc = jnp.sum(w * jnp.maximum(d, 0.0), axis=0, keepdims=True)
    for p in range(n_pages):
        put_row(p * PAGE, sc[:, p * PAGE:(p + 1) * PAGE])
    dn = jnp.sum(qi.astype(F32) * kin_ref[...], axis=1, keepdims=True)
    sn = jnp.sum(w * jnp.maximum(dn, 0.0), axis=0, keepdims=True)
    lane1 = lax.broadcasted_iota(jnp.int32, (1, LANES), 1)
    put_row(past, jnp.where(lane1 == 0, sn, -jnp.inf))

    @pl.when(b == n_q - 1)
    def _():
        width = i_scr.shape[1]
        for k in range(width // LANES):
            it_scr[k * LANES:(k + 1) * LANES, :] = i_scr[:, k * LANES:(k + 1) * LANES].T
        trivial = jnp.full((1, LANES), past + 1 <= top_k)
        lo, hi, jcut = (_as_column(r)[:, 0:1] for r in _select_bounds_t(it_scr, width // LANES, width, top_k, trivial))
        for k in range(width // LANES):
            sel = _selected(i_scr[:, k * LANES:(k + 1) * LANES], k * LANES, lo, hi, jcut)
            mask_ref[:, k * LANES:(k + 1) * LANES] = jnp.where(sel, 1.0, 0.0)


def _sample_index(layer, page_table, cache_kidx_t, qi3, wi3, kin3, top_k):
    n_q, n_pages = page_table.shape
    width = (n_pages + 1) * PAGE
    kern = functools.partial(_sample_index_kernel, n_pages=n_pages, top_k=top_k)

    def page_map(p):
        return lambda b, pt: (layer, pt[b, p], 0, 0)

    in_specs = [pl.BlockSpec((None, None, D_IDX, PAGE), page_map(p)) for p in range(n_pages)]
    in_specs += [
        pl.BlockSpec((None, N_IDX_HEADS, D_IDX), lambda b, pt: (b, 0, 0)),
        pl.BlockSpec((None, N_IDX_HEADS, 1), lambda b, pt: (b, 0, 0)),
        pl.BlockSpec((None, 1, D_IDX), lambda b, pt: (b, 0, 0)),
    ]
    return pl.pallas_call(
        kern,
        out_shape=jax.ShapeDtypeStruct((n_q, width), F32),
        grid_spec=pltpu.PrefetchScalarGridSpec(
            num_scalar_prefetch=1,
            grid=(n_q,),
            in_specs=in_specs,
            out_specs=pl.BlockSpec((n_q, width), lambda b, pt: (0, 0)),
            scratch_shapes=[pltpu.VMEM((n_q, width), F32), pltpu.VMEM((width, n_q), F32)],
        ),
        compiler_params=pltpu.CompilerParams(
            dimension_semantics=("arbitrary",), vmem_limit_bytes=VMEM_LIMIT),
        name="sample_index",
    )(page_table, *([cache_kidx_t] * n_pages), qi3, wi3, kin3)


def _sample_attend(b, k_pages, v_pages, q8_ref, qT_ref, kn_ref, vnT_ref, mask_ref, oT_scr):
    n_pages = len(k_pages)
    lane = lax.broadcasted_iota(jnp.int32, (HEAD_DIM, LANES), 1)
    qT = qT_ref[...]
    vnT = vnT_ref[...]
    s_new = jnp.sum(q8_ref[...] * kn_ref[...], axis=1, keepdims=True)
    hit_new = mask_ref[n_pages:n_pages + 1, 0:1] > 0.5
    hits = [mask_ref[p:p + 1, :] > 0.5 for p in range(n_pages)]

    for h in range(N_HEADS):
        qcol = jnp.broadcast_to(qT[:, h:h + 1], (HEAD_DIM, LANES))
        s = [jnp.where(hits[p], jnp.sum(k_pages[p][h] * qcol, axis=0, keepdims=True), MASKED)
             for p in range(n_pages)]
        sn = jnp.where(hit_new, s_new[h:h + 1, :], MASKED)
        mx = s[0]
        for p in range(1, n_pages):
            mx = jnp.maximum(mx, s[p])
        m = jnp.maximum(jnp.max(mx, axis=1, keepdims=True), sn)
        pn = jnp.exp2(sn - m)
        acc = jnp.zeros((HEAD_DIM, LANES), F32)
        lsum = jnp.zeros((1, LANES), F32)
        for p in range(n_pages):
            pp = jnp.exp2(s[p] - m)
            lsum = lsum + pp
            acc = acc + v_pages[p][h] * pp
        l = jnp.sum(lsum, axis=1, keepdims=True) + pn
        o_h = (jnp.sum(acc, axis=1, keepdims=True) + vnT[:, h:h + 1] * pn) / l
        rows = slice(h * HEAD_DIM, (h + 1) * HEAD_DIM)
        oT_scr[rows, :] = jnp.where(lane == b, o_h, oT_scr[rows, :])


def _sample_merge_kernel(att_ref, aux_ref, sp_ref, sc_ref, hs_ref, wout_ref, poolw_ref, pscale_ref,
                         cw_ref, cb_ref, cg_ref, cbeta_ref, o_ref, *, pos):
    ub = aux_ref[:, A_UB:A_GB]
    uc = aux_ref[:, A_UC:A_GC]
    run = ub
    sums, cnts = [], []
    for j in range(1, POOL_WINDOWS[-1]):
        run = run + sp_ref[POOL_STATE - j]
        if j + 1 in POOL_WINDOWS:
            sums.append(run)
            cnts.append(float(min(pos + 1, j + 1)))
    pool = _pool_mix(sums, ub, cnts, poolw_ref, pscale_ref)

    y = cw_ref[CONV_STATE:CONV_WIDTH, :] * uc
    for j in range(CONV_STATE):
        y = y + cw_ref[j:j + 1, :] * sc_ref[j]
    conv = _conv_norm(y, cb_ref, cg_ref, cbeta_ref)
    o_ref[...] = _gate_project(att_ref[...], aux_ref, pool, conv, wout_ref, hs_ref[...])


def _sample_merge(att, aux, sp_t, sc_t, hs, wout, poolw, pscale, cw, cb, cg, cbeta, pos):
    args = (att, aux, sp_t, sc_t, hs, wout, poolw, pscale, cw, cb, cg, cbeta)
    full = lambda a: pl.BlockSpec(a.shape, lambda i, nd=a.ndim: (0,) * nd)
    return pl.pallas_call(
        functools.partial(_sample_merge_kernel, pos=pos),
        out_shape=jax.ShapeDtypeStruct(hs.shape, F32),
        grid=(1,),
        in_specs=[full(a) for a in args],
        out_specs=full(hs),
        compiler_params=pltpu.CompilerParams(
            dimension_semantics=("arbitrary",), vmem_limit_bytes=VMEM_LIMIT),
        name="sample_merge",
    )(*args)


def _rope_table(pos):
    pos = pos.astype(F32)

    def cs(half):
        inv = ROPE_THETA ** (-(jnp.arange(half, dtype=F32) / half))
        ang = pos[:, None] * inv[None, :]
        return jnp.cos(ang).T, jnp.sin(ang).T

    c8, s8 = cs(HEAD_DIM // 8)
    c4, s4 = cs(D_IDX // 8)
    return jnp.concatenate([c8, s8, c4, c4, -s4, s4], axis=0)


def _permute_w_in(w):
    d = w.shape[0]
    o = 0
    parts = {}
    for name, width in (("q", D_A), ("k", D_A), ("v", D_A), ("ga", D_A), ("qi", D_QI),
                        ("wi", N_IDX_HEADS), ("ki", D_IDX), ("ub", D_B), ("gb", D_B),
                        ("cv", D_C), ("cg", D_C), ("gc", D_C)):
        parts[name] = w[:, o:o + width]
        o += width
    pad = jnp.zeros((d, LANES - D_IDX - N_IDX_HEADS), w.dtype)
    order = ["q", "k", "v", "ga", "qi", "ub", "gb", "cv", "cg", "gc", "ki", "wi"]
    return jnp.concatenate([parts[n] for n in order] + [pad], axis=1).astype(BF16)


def _block_diag(pw):
    n_g, g, _ = pw.shape
    out = jnp.zeros((n_g * g, n_g * g), pw.dtype)
    for i in range(n_g):
        out = out.at[i * g:(i + 1) * g, i * g:(i + 1) * g].set(pw[i])
    return out.astype(BF16)


def kernel(x_prompt, x_sample, cache_k, cache_v, cache_kidx, state_pool, state_conv, page_table,
           meta_tokens, norm_g, w_in, q_norm_g, k_norm_g, pool_w, pool_scale, conv_w, conv_b,
           conv_norm_g, conv_norm_b, w_out):
    n_b, seq, d_model = x_prompt.shape
    n_q = x_sample.shape[0]
    depth = w_in.shape[0]
    n_pages = page_table.shape[1]
    past = n_pages * PAGE
    t_real = seq + N_META
    t_pad = -(-t_real // TM) * TM
    top_p = min(TOPK_MAX, t_real // 4)
    top_s = min(TOPK_MAX, (past + 1) // 4)

    meta = jnp.broadcast_to(meta_tokens[None].astype(F32), (n_b, N_META, d_model))
    tail = jnp.zeros((n_b, t_pad - t_real, d_model), F32)
    hp = jnp.concatenate([meta, x_prompt, tail], axis=1).reshape(n_b * t_pad, d_model)
    hs = x_sample.reshape(n_q, d_model)

    rope_p = _rope_table(jnp.arange(t_pad))
    rope_s = _rope_table(jnp.full((n_q,), past))
    cache_k_t = jnp.transpose(cache_k, (0, 1, 3, 4, 2))
    cache_v_t = jnp.transpose(cache_v, (0, 1, 3, 4, 2))
    cache_kidx_t = jnp.transpose(cache_kidx, (0, 1, 3, 2))

    outs = {name: [] for name in ("kp", "vp", "kip", "plp", "cvp", "ks", "vs", "kis", "pls", "cvs")}
    for l in range(depth):
        w_l = _permute_w_in(w_in[l])
        g_l = norm_g[l].reshape(1, d_model)
        qg = q_norm_g[l].reshape(HEAD_DIM, 1)
        kg = k_norm_g[l].reshape(HEAD_DIM, 1)
        wout = w_out[l].astype(BF16)
        poolw = _block_diag(pool_w[l])
        pscale = pool_scale[l].reshape(1, D_B)
        cw = conv_w[l]
        cb = conv_b[l].reshape(1, D_C)
        cg = conv_norm_g[l].reshape(1, D_C)
        cbeta = conv_norm_b[l].reshape(1, D_C)

        heads_last = lambda xT: jnp.transpose(xT.reshape(xT.shape[0], N_HEADS, HEAD_DIM, -1), (0, 3, 1, 2))

        q, kTf, kT, vTf, vb, aux, qi, wk, kiTf, kiT = _project(
            hp, n_b, t_real, TM, g_l, w_l, rope_p, qg, kg)
        q_s, kTf_s, _, vTf_s, _, aux_s, qi_s, wk_s, kiTf_s, _ = _project(
            hs, 1, n_q, n_q, g_l, w_l, rope_s, qg, kg)
        qi3 = qi_s.reshape(n_q, N_IDX_HEADS, D_IDX)
        wi3 = wk_s[:, WK_WI:WK_WI + N_IDX_HEADS].reshape(n_q, N_IDX_HEADS, 1)
        kin3 = wk_s[:, WK_KI:WK_KI + D_IDX].reshape(n_q, 1, D_IDX)
        mask = _sample_index(l, page_table, cache_kidx_t, qi3, wi3, kin3, top_s)
        q8 = q_s.astype(F32).reshape(n_q, N_HEADS, HEAD_DIM)
        k_new = heads_last(kTf_s).reshape(n_q, 1, N_HEADS, HEAD_DIM)
        v_new = heads_last(vTf_s).reshape(n_q, 1, N_HEADS, HEAD_DIM)

        att, att_s = _attention(
            l, page_table, q, qi, wk, kT, kiT, vb, cache_k_t, cache_v_t, q8, jnp.swapaxes(q8, 1, 2),
            k_new[:, 0], jnp.swapaxes(v_new[:, 0], 1, 2), mask.reshape(n_q, n_pages + 1, PAGE),
            n_b, t_pad, t_real, top_p)
        hp = _prompt_merge(att, aux, hp, wout, poolw, pscale, cw, cb, cg, cbeta, n_b, t_pad)
        aux3 = aux.reshape(n_b, t_pad, N_AUX)
        outs["kp"].append(heads_last(kTf))
        outs["vp"].append(heads_last(vTf))
        outs["kip"].append(jnp.swapaxes(kiTf, 1, 2))
        outs["plp"].append(aux3[:, t_real - POOL_STATE:t_real, A_UB:A_GB])
        outs["cvp"].append(aux3[:, t_real - CONV_STATE:t_real, A_UC:A_GC])

        sp_t = jnp.swapaxes(state_pool[l], 0, 1)
        sc_t = jnp.swapaxes(state_conv[l], 0, 1)
        hs = _sample_merge(att_s, aux_s, sp_t, sc_t, hs, wout, poolw, pscale, cw, cb, cg, cbeta, past)
        outs["ks"].append(k_new)
        outs["vs"].append(v_new)
        outs["kis"].append(jnp.swapaxes(kiTf_s, 1, 2).reshape(n_q, 1, D_IDX))
        outs["pls"].append(jnp.concatenate([state_pool[l][:, 1:], aux_s[:, None, A_UB:A_GB]], axis=1))
        outs["cvs"].append(jnp.concatenate([state_conv[l][:, 1:], aux_s[:, None, A_UC:A_GC]], axis=1))

    y_prompt = hp.reshape(n_b, t_pad, d_model)[:, N_META:t_real]
    y_sample = hs.reshape(n_q, 1, d_model)
    st = lambda name: jnp.stack(outs[name], axis=0)
    return (y_prompt, y_sample, st("kp"), st("vp"), st("kip"), st("plp"), st("cvp"),
            st("ks"), st("vs"), st("kis"), st("pls"), st("cvs"))
```

```python
import functools

import jax
import jax.numpy as jnp
from jax import lax
from jax.experimental import pallas as pl
from jax.experimental.pallas import tpu as pltpu

F32 = jnp.float32
BF16 = jnp.bfloat16

LANES = 128
SUBLANES = 8
N_META = 16
HEAD_DIM = 64
N_HEADS = 8
D_A = N_HEADS * HEAD_DIM
N_IDX_HEADS = 8
D_IDX = 32
D_QI = N_IDX_HEADS * D_IDX
D_B = 256
D_C = 256
POOL_WINDOWS = (2, 4, 8, 16)
POOL_GROUP = D_B // len(POOL_WINDOWS)
POOL_STATE = 15
CONV_WIDTH = 31
CONV_STATE = CONV_WIDTH - 1
HALO = 32
TOPK_MAX = 256
ROPE_THETA = 500000.0
EPS = 1e-6
PAGE = 128
LOG2E = 1.4426950408889634
Q_SCALE = HEAD_DIM ** -0.5 * LOG2E
IDX_SCALE = (D_IDX ** -0.5) * (N_IDX_HEADS ** -0.5)

QB = 128
CK = 384
TM = 384
MAX_BISECT = 64
BIG_INDEX = 1.0e9
MASKED = -1.0e30
TINY = 1.1754944e-38
ACC_ROWS = 64
VMEM_LIMIT = 56 * 1024 * 1024

C_Q, C_K, C_V, C_GA, C_QI = 0, 512, 1024, 1536, 2048
C_UB, C_GB, C_CV, C_CG, C_GC, C_KI = 2304, 2560, 2816, 3072, 3328, 3584
N_PERM = C_KI + LANES
A_GA, A_UB, A_GB, A_UC, A_GC, N_AUX = 0, 512, 768, 1024, 1280, 1536
WK_KI, WK_WI = 0, D_IDX


def _sigmoid(x):
    return 0.5 * jnp.tanh(0.5 * x) + 0.5


def _silu(x):
    return x * _sigmoid(x)


def _proj_kernel(x_ref, g_ref, w_ref, rope_ref, qg_ref, kg_ref,
                 q_ref, kTf_ref, kT_ref, vTf_ref, vb_ref, aux_ref, qi_ref, wk_ref, kiTf_ref, kiT_ref,
                 t_scr):
    x = x_ref[...]
    ms = jnp.mean(x * x, axis=-1, keepdims=True)
    h = (x * lax.rsqrt(ms + EPS) * g_ref[...]).astype(BF16)

    def proj(c0, c1):
        return jnp.dot(h, w_ref[:, c0:c1], preferred_element_type=F32)

    cos8 = rope_ref[0:8, :]
    sin8 = rope_ref[8:16, :]
    cc4 = rope_ref[16:24, :]
    ss4 = rope_ref[24:32, :]

    zT = proj(C_Q, C_V).T
    for j in range(2 * N_HEADS):
        xh = zT[j * HEAD_DIM:(j + 1) * HEAD_DIM, :]
        g = qg_ref[...] if j < N_HEADS else kg_ref[...]
        xn = xh * lax.rsqrt(jnp.mean(xh * xh, axis=0, keepdims=True) + EPS) * g
        x1 = xn[0:8, :]
        x2 = xn[8:16, :]
        r0 = j * HEAD_DIM
        t_scr[r0:r0 + 8, :] = x1 * cos8 - x2 * sin8
        t_scr[r0 + 8:r0 + 16, :] = x2 * cos8 + x1 * sin8
        t_scr[r0 + 16:r0 + HEAD_DIM, :] = xn[16:HEAD_DIM, :]
    q_ref[...] = (t_scr[0:D_A, :].T * Q_SCALE).astype(BF16)
    kT = t_scr[D_A:2 * D_A, :]
    kTf_ref[...] = kT
    kT_ref[...] = kT.astype(BF16)

    zv = proj(C_V, C_GA)
    vTf_ref[...] = zv.T
    vb_ref[...] = zv.astype(BF16)

    aux_ref[:, A_GA:A_UB] = proj(C_GA, C_QI)
    aux_ref[:, A_UB:A_UC] = proj(C_UB, C_CV)
    zc = proj(C_CV, C_KI)
    aux_ref[:, A_UC:A_GC] = zc[:, 0:D_C] * _sigmoid(zc[:, D_C:2 * D_C])
    aux_ref[:, A_GC:N_AUX] = zc[:, 2 * D_C:3 * D_C]

    ziT = proj(C_QI, C_UB).T
    parts = []
    for hh in range(N_IDX_HEADS):
        r0 = hh * D_IDX
        x8 = ziT[r0:r0 + 8, :]
        parts.append(x8 * cc4 + pltpu.roll(x8, 4, 0) * ss4)
        parts.append(ziT[r0 + 8:r0 + D_IDX, :])
    qi_ref[...] = jnp.concatenate(parts, axis=0).T.astype(BF16)

    zwT = proj(C_KI, N_PERM).T
    x8 = zwT[0:8, :]
    ki = jnp.concatenate([x8 * cc4 + pltpu.roll(x8, 4, 0) * ss4, zwT[8:D_IDX, :]], axis=0)
    wi = zwT[D_IDX:D_IDX + N_IDX_HEADS, :] * IDX_SCALE
    wk_ref[...] = jnp.concatenate([ki, wi, zwT[D_IDX + N_IDX_HEADS:, :]], axis=0).T
    kiTf_ref[...] = ki
    kiT_ref[...] = jnp.concatenate([ki] * (LANES // D_IDX), axis=0).astype(BF16)


def _project(x2d, n_batch, t_real, tm, g, w, rope, qg, kg):
    rows = x2d.shape[0]
    t = rows // n_batch
    nt = t // tm
    d = x2d.shape[1]
    row_map = lambda b, i: (b * nt + i, 0)
    col_map = lambda b, i: (b, 0, i)
    const = lambda b, i: (0, 0)
    out_shape = (
        jax.ShapeDtypeStruct((rows, D_A), BF16),
        jax.ShapeDtypeStruct((n_batch, D_A, t_real), F32),
        jax.ShapeDtypeStruct((n_batch, D_A, t), BF16),
        jax.ShapeDtypeStruct((n_batch, D_A, t_real), F32),
        jax.ShapeDtypeStruct((rows, D_A), BF16),
        jax.ShapeDtypeStruct((rows, N_AUX), F32),
        jax.ShapeDtypeStruct((rows, D_QI), BF16),
        jax.ShapeDtypeStruct((rows, LANES), F32),
        jax.ShapeDtypeStruct((n_batch, D_IDX, t_real), F32),
        jax.ShapeDtypeStruct((n_batch, LANES, t), BF16),
    )
    out_specs = (
        pl.BlockSpec((tm, D_A), row_map),
        pl.BlockSpec((None, D_A, tm), col_map),
        pl.BlockSpec((None, D_A, tm), col_map),
        pl.BlockSpec((None, D_A, tm), col_map),
        pl.BlockSpec((tm, D_A), row_map),
        pl.BlockSpec((tm, N_AUX), row_map),
        pl.BlockSpec((tm, D_QI), row_map),
        pl.BlockSpec((tm, LANES), row_map),
        pl.BlockSpec((None, D_IDX, tm), col_map),
        pl.BlockSpec((None, LANES, tm), col_map),
    )
    in_specs = [
        pl.BlockSpec((tm, d), row_map),
        pl.BlockSpec((1, d), const),
        pl.BlockSpec((d, N_PERM), const),
        pl.BlockSpec((32, tm), lambda b, i: (0, i)),
        pl.BlockSpec((HEAD_DIM, 1), const),
        pl.BlockSpec((HEAD_DIM, 1), const),
    ]
    return pl.pallas_call(
        _proj_kernel,
        out_shape=out_shape,
        grid=(n_batch, nt),
        in_specs=in_specs,
        out_specs=out_specs,
        scratch_shapes=[pltpu.VMEM((2 * D_A, tm), F32)],
        compiler_params=pltpu.CompilerParams(
            dimension_semantics=("parallel", "parallel"), vmem_limit_bytes=VMEM_LIMIT),
        name="proj",
    )(x2d, g, w, rope, qg, kg)


def _fold_t(it_scr, n_blk, ck, fn, inits):
    per = ck // LANES

    def body(c, accs):
        row = pl.multiple_of(c * ck, LANES)
        return fn(accs, it_scr[pl.ds(row, ck), :], row)

    def tail(c, accs):
        row = pl.multiple_of(c * LANES, LANES)
        return fn(accs, it_scr[pl.ds(row, LANES), :], row)

    accs = tuple(jnp.full((ACC_ROWS, LANES), v, F32) for v in inits)
    accs = lax.fori_loop(0, n_blk // per, body, accs)
    return lax.fori_loop((n_blk // per) * per, n_blk, tail, accs)


def _map_t(it_scr, n_blk, ck, fn):
    per = ck // LANES

    def body(c, carry):
        row = pl.multiple_of(c * ck, LANES)
        it_scr[pl.ds(row, ck), :] = fn(it_scr[pl.ds(row, ck), :], row)
        return carry

    def tail(c, carry):
        row = pl.multiple_of(c * LANES, LANES)
        it_scr[pl.ds(row, LANES), :] = fn(it_scr[pl.ds(row, LANES), :], row)
        return carry

    lax.fori_loop(0, n_blk // per, body, 0)
    lax.fori_loop((n_blk // per) * per, n_blk, tail, 0)


def _vreg_fold(x, op):
    return op(x.reshape(x.shape[0] // ACC_ROWS, ACC_ROWS, LANES), axis=0)


def _count_ge_t(it_scr, n_blk, ck, mid):
    (acc,) = _fold_t(
        it_scr, n_blk, ck,
        lambda a, blk, row: (a[0] + _vreg_fold(jnp.where(blk >= mid, 1.0, 0.0), jnp.sum),), (0.0,))
    return jnp.sum(acc, axis=0, keepdims=True)


def _select_bounds_t(it_scr, n_blk, ck, top_k, trivial):
    kf = float(top_k)
    ninf = -jnp.inf
    pinf = jnp.inf
    land, lor, lnot = jnp.logical_and, jnp.logical_or, jnp.logical_not
    passes = 4

    mn, mx = _fold_t(
        it_scr, n_blk, ck,
        lambda a, blk, row: (jnp.minimum(a[0], _vreg_fold(jnp.where(blk == ninf, pinf, blk), jnp.min)),
                             jnp.maximum(a[1], _vreg_fold(blk, jnp.max))),
        (pinf, ninf))
    cpos, cnn = _fold_t(
        it_scr, n_blk, ck,
        lambda a, blk, row: (a[0] + _vreg_fold(jnp.where(blk > 0.0, 1.0, 0.0), jnp.sum),
                             a[1] + _vreg_fold(jnp.where(blk >= 0.0, 1.0, 0.0), jnp.sum)),
        (0.0, 0.0))
    col_min = jnp.min(mn, axis=0, keepdims=True)
    col_max = jnp.max(mx, axis=0, keepdims=True)
    c_pos = jnp.sum(cpos, axis=0, keepdims=True)
    c_nn = jnp.sum(cnn, axis=0, keepdims=True)
    c_top = _count_ge_t(it_scr, n_blk, ck, col_max)

    top_tie = land(c_top >= kf, lnot(trivial))
    zero_tie = land(land(c_pos < kf, c_nn >= kf), lnot(lor(trivial, top_tie)))
    neg_side = c_nn < kf
    lo0 = jnp.where(top_tie, col_max, jnp.where(zero_tie, 0.0, jnp.where(neg_side, col_min, TINY)))
    lo0 = jnp.where(trivial, col_min, lo0)
    hi0 = jnp.where(lor(trivial, top_tie), pinf,
                    jnp.where(zero_tie, jnp.where(c_nn == kf, pinf, TINY),
                              jnp.where(neg_side, 0.0, col_max)))
    pos_exact = land(lnot(neg_side), c_pos == kf)
    hi0 = jnp.where(land(pos_exact, lnot(lor(trivial, lor(top_tie, zero_tie)))), pinf, hi0)
    done0 = lor(lor(trivial, land(top_tie, c_top == kf)), lor(land(zero_tie, c_nn == kf), pos_exact))
    res0 = jnp.where(done0, 1.0, 0.0)
    act0 = jnp.where(lor(done0, lor(top_tie, zero_tie)), 0.0, 1.0)

    def bis_cond(st):
        return jnp.logical_and(st[0] < MAX_BISECT, st[1] > 0.0)

    def bis_body(st):
        it, _, lo, hi, res, act = st
        for _ in range(passes):
            mid = 0.5 * lo + 0.5 * hi
            ok = land(act > 0.0, land(mid > lo, mid < hi))
            c = _count_ge_t(it_scr, n_blk, ck, mid)
            eq = land(ok, c == kf)
            gt = land(ok, c > kf)
            lt = land(ok, c < kf)
            lo = jnp.where(lor(eq, gt), mid, lo)
            hi = jnp.where(eq, pinf, jnp.where(lt, mid, hi))
            res = jnp.where(eq, 1.0, res)
            act = jnp.where(lor(eq, lnot(ok)), 0.0, act)
        return it + passes, jnp.sum(act), lo, hi, res, act

    _, _, lo, hi, res, _ = lax.while_loop(
        bis_cond, bis_body, (jnp.int32(0), jnp.sum(act0), lo0, hi0, res0, act0))

    def cut_band():
        need = kf - _count_ge_t(it_scr, n_blk, ck, hi)

        def to_index(blk, row):
            idx = (lax.broadcasted_iota(jnp.int32, blk.shape, 0) + row).astype(F32)
            return jnp.where(land(blk >= lo, blk < hi), idx, BIG_INDEX)

        _map_t(it_scr, n_blk, ck, to_index)

        def count_band(j):
            (acc,) = _fold_t(
                it_scr, n_blk, ck,
                lambda a, blk, row: (a[0] + _vreg_fold(jnp.where(blk <= j, 1.0, 0.0), jnp.sum),), (0.0,))
            return jnp.sum(acc, axis=0, keepdims=True)

        def j_cond(st):
            return jnp.logical_and(st[0] < 32, st[1] > 0.0)

        def j_body(st):
            it, _, jl, jh, jc, open_ = st
            jm = jnp.floor((jl + jh + 1.0) * 0.5)
            c = count_band(jm)
            is_open = open_ > 0.0
            eq = land(is_open, c == need)
            jl = jnp.where(land(is_open, c < need), jm, jl)
            jh = jnp.where(land(is_open, c > need), jm, jh)
            jc = jnp.where(eq, jm, jc)
            open_ = jnp.where(eq, 0.0, open_)
            return it + 1, jnp.sum(open_), jl, jh, jc, open_

        open0 = 1.0 - res
        full = lambda v: jnp.full((1, LANES), v, F32)
        st0 = (jnp.int32(0), jnp.sum(open0), full(-1.0), full(float(it_scr.shape[0] - 1)),
               full(BIG_INDEX), open0)
        return lax.while_loop(j_cond, j_body, st0)[4]

    jcut = lax.cond(jnp.sum(1.0 - res) > 0.0, cut_band, lambda: jnp.full((1, LANES), BIG_INDEX, F32))
    return lo, hi, jcut


def _paired_loop(n, body, group=3):
    def grouped(j, carry):
        for u in range(group):
            carry = body(group * j + u, carry)
        return carry

    lax.fori_loop(0, n // group, grouped, 0)
    lax.fori_loop((n // group) * group, n, body, 0)


def _as_column(row):
    return jnp.broadcast_to(row, (LANES, LANES)).T


def _selected(blk, col, lo, hi, jcut):
    lane = lax.broadcasted_iota(jnp.int32, blk.shape, 1)
    idx = (lane + col).astype(F32)
    return jnp.logical_or(blk >= hi, jnp.logical_and(blk >= lo, idx <= jcut))


def _attn_kernel(pt_ref, q_ref, qi_ref, wk_ref, kT_ref, kiT_ref, v_ref, *rest, t_real, top_k, n_pages):
    del pt_ref
    k_pages = rest[:n_pages]
    v_pages = rest[n_pages:2 * n_pages]
    (q8_ref, qT_ref, kn_ref, vnT_ref, mask_ref, o_ref, os_ref,
     i_scr, it_scr, s_scr, m_scr, l_scr, acc_scr, oT_scr) = rest[2 * n_pages:]
    i = pl.program_id(1)
    n_ch = (i * QB) // CK + 1
    q_pos = i * QB + lax.broadcasted_iota(jnp.int32, (QB, 1), 0)
    lane = lax.broadcasted_iota(jnp.int32, (QB, LANES), 1)
    n_sub = CK // LANES

    qi = qi_ref[...]
    q = q_ref[...]
    qiz, qz = [], []
    per = LANES // D_IDX
    for h in range(N_IDX_HEADS):
        blk = qi[:, (h // per) * LANES:(h // per + 1) * LANES]
        qiz.append(jnp.where(lane // D_IDX == h % per, blk, jnp.zeros_like(blk)))
    for h in range(N_HEADS):
        blk = q[:, (h // 2) * LANES:(h // 2 + 1) * LANES]
        qz.append(jnp.where(lane // HEAD_DIM == h % 2, blk, jnp.zeros_like(blk)))
    w = wk_ref[:, WK_WI:WK_WI + N_IDX_HEADS]

    def score_chunk(c, carry):
        col = pl.multiple_of(c * CK, LANES)
        kc = kiT_ref[:, pl.ds(col, CK)]
        acc = jnp.zeros((QB, CK), F32)
        for h in range(N_IDX_HEADS):
            d = jnp.dot(qiz[h], kc, preferred_element_type=F32)
            acc = acc + w[:, h:h + 1] * jnp.maximum(d, 0.0)
        key_pos = col + lax.broadcasted_iota(jnp.int32, (QB, CK), 1)
        sc = jnp.where(key_pos <= q_pos, acc, -jnp.inf)
        i_scr[:, pl.ds(col, CK)] = sc
        for k in range(n_sub):
            row = pl.multiple_of(col + k * LANES, LANES)
            it_scr[pl.ds(row, LANES), :] = sc[:, k * LANES:(k + 1) * LANES].T
        return carry

    _paired_loop(n_ch, score_chunk)

    q_row = i * QB + lax.broadcasted_iota(jnp.int32, (1, LANES), 1)
    trivial = jnp.logical_or(q_row + 1 <= top_k, q_row >= t_real)
    lo, hi, jcut = (_as_column(r)[:, 0:1] for r in _select_bounds_t(it_scr, i + 1, CK, top_k, trivial))

    m_scr[...] = jnp.full(m_scr.shape, MASKED, F32)

    def logits_chunk(c, carry):
        col = pl.multiple_of(c * CK, LANES)
        sel = _selected(i_scr[:, pl.ds(col, CK)], col, lo, hi, jcut)
        bias = jnp.where(sel, 0.0, MASKED)
        for h in range(N_HEADS):
            pr = h // 2
            kc = kT_ref[pr * LANES:(pr + 1) * LANES, pl.ds(col, CK)]
            s = jnp.dot(qz[h], kc, preferred_element_type=F32) + bias
            s_scr[h, :, pl.ds(col, CK)] = s
            mx = s[:, 0:LANES]
            for k in range(1, n_sub):
                mx = jnp.maximum(mx, s[:, k * LANES:(k + 1) * LANES])
            m_scr[h] = jnp.maximum(m_scr[h], mx)
        return carry

    _paired_loop(n_ch, logits_chunk)

    for h in range(N_HEADS):
        m_scr[h] = jnp.broadcast_to(jnp.max(m_scr[h], axis=1, keepdims=True), (QB, LANES))
    l_scr[...] = jnp.zeros(l_scr.shape, F32)
    acc_scr[...] = jnp.zeros(acc_scr.shape, F32)

    def value_chunk(c, carry):
        col = pl.multiple_of(c * CK, LANES)
        for h in range(N_HEADS):
            pr = h // 2
            m = m_scr[h]
            ps = [jnp.exp2(s_scr[h, :, pl.ds(pl.multiple_of(col + k * LANES, LANES), LANES)] - m)
                  for k in range(n_sub)]
            lsum = ps[0]
            for k in range(1, n_sub):
                lsum = lsum + ps[k]
            l_scr[h] = l_scr[h] + lsum
            p = jnp.concatenate(ps, axis=1).astype(BF16)
            vc = v_ref[pl.ds(col, CK), pr * LANES:(pr + 1) * LANES]
            acc_scr[h] = acc_scr[h] + jnp.dot(p, vc, preferred_element_type=F32)
        return carry

    _paired_loop(n_ch, value_chunk)

    for pr in range(N_HEADS // 2):
        first = lane < HEAD_DIM
        num = jnp.where(first, acc_scr[2 * pr], acc_scr[2 * pr + 1])
        den = jnp.where(first, jnp.sum(l_scr[2 * pr], axis=1, keepdims=True),
                        jnp.sum(l_scr[2 * pr + 1], axis=1, keepdims=True))
        o_ref[:, pr * LANES:(pr + 1) * LANES] = num / den

    step = pl.program_id(0) * pl.num_programs(1) + i
    n_steps = pl.num_programs(0) * pl.num_programs(1)

    @pl.when(step == 0)
    def _():
        oT_scr[...] = jnp.zeros(oT_scr.shape, F32)

    @pl.when(step < os_ref.shape[0])
    def _():
        _sample_attend(step, k_pages, v_pages, q8_ref, qT_ref, kn_ref, vnT_ref, mask_ref, oT_scr)

    @pl.when(step == n_steps - 1)
    def _():
        os_ref[...] = oT_scr[...].T


def _attention(layer, page_table, q, qi, wk, kT, kiT, vb, cache_k_t, cache_v_t, q8, qT, kn, vnT, mask3,
               n_batch, t_pad, t_real, top_k):
    nq = t_pad // QB
    n_q, n_pages = page_table.shape
    assert n_q == LANES, "one output lane per sample query"
    assert n_batch * nq >= n_q, "one sample query per grid step"
    kern = functools.partial(_attn_kernel, t_real=t_real, top_k=top_k, n_pages=n_pages)
    row_map = lambda b, i, pt: (b * nq + i, 0)
    query = lambda b, i: jnp.minimum(b * nq + i, n_q - 1)
    per_q = lambda b, i, pt: (query(b, i), 0, 0)
    once = pl.Buffered(1)

    def page_spec(p):
        return pl.BlockSpec((None, None, N_HEADS, HEAD_DIM, PAGE),
                            lambda b, i, pt: (layer, pt[query(b, i), p], 0, 0, 0))

    in_specs = [
        pl.BlockSpec((QB, D_A), row_map),
        pl.BlockSpec((QB, D_QI), row_map),
        pl.BlockSpec((QB, LANES), row_map),
        pl.BlockSpec((None, D_A, t_pad), lambda b, i, pt: (b, 0, 0), pipeline_mode=once),
        pl.BlockSpec((None, LANES, t_pad), lambda b, i, pt: (b, 0, 0), pipeline_mode=once),
        pl.BlockSpec((t_pad, D_A), lambda b, i, pt: (b, 0), pipeline_mode=once),
    ]
    in_specs += [page_spec(p) for p in range(n_pages)] + [page_spec(p) for p in range(n_pages)]
    in_specs += [
        pl.BlockSpec((None, N_HEADS, HEAD_DIM), per_q),
        pl.BlockSpec((None, HEAD_DIM, N_HEADS), per_q),
        pl.BlockSpec((None, N_HEADS, HEAD_DIM), per_q),
        pl.BlockSpec((None, HEAD_DIM, N_HEADS), per_q),
        pl.BlockSpec((None, n_pages + 1, PAGE), per_q),
    ]
    return pl.pallas_call(
        kern,
        out_shape=(jax.ShapeDtypeStruct((n_batch * t_pad, D_A), F32),
                   jax.ShapeDtypeStruct((n_q, D_A), F32)),
        grid_spec=pltpu.PrefetchScalarGridSpec(
            num_scalar_prefetch=1,
            grid=(n_batch, nq),
            in_specs=in_specs,
            out_specs=(pl.BlockSpec((QB, D_A), row_map),
                       pl.BlockSpec((n_q, D_A), lambda b, i, pt: (0, 0))),
            scratch_shapes=[
                pltpu.VMEM((QB, t_pad), F32),
                pltpu.VMEM((t_pad, QB), F32),
                pltpu.VMEM((N_HEADS, QB, t_pad), F32),
                pltpu.VMEM((N_HEADS, QB, LANES), F32),
                pltpu.VMEM((N_HEADS, QB, LANES), F32),
                pltpu.VMEM((N_HEADS, QB, LANES), F32),
                pltpu.VMEM((D_A, n_q), F32),
            ],
        ),
        compiler_params=pltpu.CompilerParams(
            dimension_semantics=("arbitrary", "arbitrary"), vmem_limit_bytes=VMEM_LIMIT),
        name="attention",
    )(page_table, q, qi, wk, kT, kiT, vb, *([cache_k_t] * n_pages), *([cache_v_t] * n_pages),
      q8, qT, kn, vnT, mask3)


def _gate_project(att, aux_ref, pool, conv, wout_ref, resid):
    ya = att * _silu(aux_ref[:, A_GA:A_UB])
    yb = pool * _silu(aux_ref[:, A_GB:A_UC])
    yc = conv * _silu(aux_ref[:, A_GC:N_AUX])
    y = jnp.concatenate([ya, yb, yc], axis=1).astype(BF16)
    return resid + jnp.dot(y, wout_ref[...], preferred_element_type=F32)


def _pool_mix(sums, u, cnts, poolw_ref, pscale_ref):
    lane = lax.broadcasted_iota(jnp.int32, u.shape, 1)
    d = sums[-1] / cnts[-1]
    for g in range(len(POOL_WINDOWS) - 2, -1, -1):
        d = jnp.where(lane < (g + 1) * POOL_GROUP, sums[g] / cnts[g], d)
    d = (d - u).astype(BF16)
    return jnp.dot(d, poolw_ref[...], preferred_element_type=F32) * pscale_ref[...]


def _conv_norm(y, cb_ref, cg_ref, cbeta_ref):
    y = y + cb_ref[...]
    mu = jnp.mean(y, axis=-1, keepdims=True)
    yc = y - mu
    var = jnp.mean(yc * yc, axis=-1, keepdims=True)
    return _silu(yc * lax.rsqrt(var + EPS) * cg_ref[...] + cbeta_ref[...])


def _row_windows(ext_scr, rot_scr, tm):
    n = rot_scr.shape[1]
    for r in range(1, SUBLANES):
        rot_scr[r - 1] = ext_scr[r:r + n, :]

    def window(off):
        r = off % SUBLANES
        a = off - r
        return ext_scr[a:a + tm, :] if r == 0 else rot_scr[r - 1, a:a + tm, :]

    return window


def _merge_kernel(att_ref, aux_ref, hub_ref, huc_ref, hp_ref, wout_ref, poolw_ref, pscale_ref,
                  cw_ref, cb_ref, cg_ref, cbeta_ref, o_ref, ub_scr, uc_scr, ub_rot, uc_rot):
    i = pl.program_id(1)
    tm = att_ref.shape[0]
    first = i == 0
    ub = aux_ref[:, A_UB:A_GB]
    uc = aux_ref[:, A_UC:A_GC]
    ub_scr[0:HALO, :] = jnp.where(first, 0.0, hub_ref[...])
    uc_scr[0:HALO, :] = jnp.where(first, 0.0, huc_ref[...])
    ub_scr[HALO:HALO + tm, :] = ub
    uc_scr[HALO:HALO + tm, :] = uc
    ub_win = _row_windows(ub_scr, ub_rot, tm)
    uc_win = _row_windows(uc_scr, uc_rot, tm)

    pos = i * tm + lax.broadcasted_iota(jnp.int32, (tm, 1), 0)
    run = ub
    sums, cnts = [], []
    for j in range(1, POOL_WINDOWS[-1]):
        run = run + ub_win(HALO - j)
        if j + 1 in POOL_WINDOWS:
            sums.append(run)
            cnts.append(jnp.minimum(pos + 1, j + 1).astype(F32))
    pool = _pool_mix(sums, ub, cnts, poolw_ref, pscale_ref)

    y = jnp.zeros((tm, D_C), F32)
    for j in range(CONV_WIDTH):
        y = y + cw_ref[j:j + 1, :] * uc_win(HALO - CONV_STATE + j)
    conv = _conv_norm(y, cb_ref, cg_ref, cbeta_ref)

    o_ref[...] = _gate_project(att_ref[...], aux_ref, pool, conv, wout_ref, hp_ref[...])


def _prompt_merge(att, aux, hp, wout, poolw, pscale, cw, cb, cg, cbeta, n_batch, t_pad):
    nt = t_pad // TM
    hb = TM // HALO
    row_map = lambda b, i: (b * nt + i, 0)
    const = lambda b, i: (0, 0)
    d = hp.shape[1]

    def halo_map(col_block):
        return lambda b, i: (jnp.maximum((b * nt + i) * hb - 1, 0), col_block)

    return pl.pallas_call(
        _merge_kernel,
        out_shape=jax.ShapeDtypeStruct(hp.shape, F32),
        grid=(n_batch, nt),
        in_specs=[
            pl.BlockSpec((TM, D_A), row_map),
            pl.BlockSpec((TM, N_AUX), row_map),
            pl.BlockSpec((HALO, D_B), halo_map(A_UB // D_B)),
            pl.BlockSpec((HALO, D_C), halo_map(A_UC // D_C)),
            pl.BlockSpec((TM, d), row_map),
            pl.BlockSpec(wout.shape, const),
            pl.BlockSpec(poolw.shape, const),
            pl.BlockSpec((1, D_B), const),
            pl.BlockSpec(cw.shape, const),
            pl.BlockSpec((1, D_C), const),
            pl.BlockSpec((1, D_C), const),
            pl.BlockSpec((1, D_C), const),
        ],
        out_specs=pl.BlockSpec((TM, d), row_map),
        scratch_shapes=[pltpu.VMEM((HALO + TM, D_B), F32), pltpu.VMEM((HALO + TM, D_C), F32),
                        pltpu.VMEM((SUBLANES - 1, HALO + TM - SUBLANES, D_B), F32),
                        pltpu.VMEM((SUBLANES - 1, HALO + TM - SUBLANES, D_C), F32)],
        compiler_params=pltpu.CompilerParams(
            dimension_semantics=("parallel", "parallel"), vmem_limit_bytes=VMEM_LIMIT),
        name="prompt_merge",
    )(att, aux, aux, aux, hp, wout, poolw, pscale, cw, cb, cg, cbeta)


def _sample_index_kernel(pt_ref, *refs, n_pages, top_k):
    del pt_ref
    ki_pages = refs[:SUBLANES * n_pages]
    qi_ref, wi_ref, kin_ref, mask_ref, i_scr, it_scr = refs[SUBLANES * n_pages:]
    g = pl.program_id(0)
    past = n_pages * PAGE
    width = i_scr.shape[1]
    sub = lax.broadcasted_iota(jnp.int32, (SUBLANES, width), 0)
    lane1 = lax.broadcasted_iota(jnp.int32, (1, LANES), 1)

    rows = jnp.full((SUBLANES, width), -jnp.inf, F32)
    for j in range(SUBLANES):
        qi = qi_ref[j]
        w = wi_ref[j]
        pages = ki_pages[j * n_pages:(j + 1) * n_pages]
        kp = jnp.concatenate([pg[...] for pg in pages], axis=1).astype(BF16)
        d = jnp.dot(qi, kp, preferred_element_type=F32)
        sc = jnp.sum(w * jnp.maximum(d, 0.0), axis=0, keepdims=True)
        dn = jnp.sum(qi.astype(F32) * kin_ref[j], axis=1, keepdims=True)
        sn = jnp.sum(w * jnp.maximum(dn, 0.0), axis=0, keepdims=True)
        row = jnp.concatenate([sc, jnp.where(lane1 == 0, sn, -jnp.inf)], axis=1)
        rows = jnp.where(sub == j, row, rows)
    i_scr[pl.ds(pl.multiple_of(g * SUBLANES, SUBLANES), SUBLANES), :] = rows

    @pl.when(g == pl.num_programs(0) - 1)
    def _():
        width = i_scr.shape[1]
        for k in range(width // LANES):
            it_scr[k * LANES:(k + 1) * LANES, :] = i_scr[:, k * LANES:(k + 1) * LANES].T
        trivial = jnp.full((1, LANES), past + 1 <= top_k)
        lo, hi, jcut = (_as_column(r)[:, 0:1] for r in _select_bounds_t(it_scr, width // LANES, width, top_k, trivial))
        for k in range(width // LANES):
            sel = _selected(i_scr[:, k * LANES:(k + 1) * LANES], k * LANES, lo, hi, jcut)
            mask_ref[:, k * LANES:(k + 1) * LANES] = jnp.where(sel, 1.0, 0.0)


def _sample_index(layer, page_table, cache_kidx_t, qi3, wi3, kin3, top_k):
    n_q, n_pages = page_table.shape
    width = (n_pages + 1) * PAGE
    kern = functools.partial(_sample_index_kernel, n_pages=n_pages, top_k=top_k)

    def page_map(j, p):
        return lambda g, pt: (layer, pt[g * SUBLANES + j, p], 0, 0)

    group = lambda g, pt: (g, 0, 0)
    in_specs = [pl.BlockSpec((None, None, D_IDX, PAGE), page_map(j, p))
                for j in range(SUBLANES) for p in range(n_pages)]
    in_specs += [
        pl.BlockSpec((SUBLANES, N_IDX_HEADS, D_IDX), group),
        pl.BlockSpec((SUBLANES, N_IDX_HEADS, 1), group),
        pl.BlockSpec((SUBLANES, 1, D_IDX), group),
    ]
    return pl.pallas_call(
        kern,
        out_shape=jax.ShapeDtypeStruct((n_q, width), F32),
        grid_spec=pltpu.PrefetchScalarGridSpec(
            num_scalar_prefetch=1,
            grid=(n_q // SUBLANES,),
            in_specs=in_specs,
            out_specs=pl.BlockSpec((n_q, width), lambda b, pt: (0, 0)),
            scratch_shapes=[pltpu.VMEM((n_q, width), F32), pltpu.VMEM((width, n_q), F32)],
        ),
        compiler_params=pltpu.CompilerParams(
            dimension_semantics=("arbitrary",), vmem_limit_bytes=VMEM_LIMIT),
        name="sample_index",
    )(page_table, *([cache_kidx_t] * (SUBLANES * n_pages)), qi3, wi3, kin3)


def _sample_attend(b, k_pages, v_pages, q8_ref, qT_ref, kn_ref, vnT_ref, mask_ref, oT_scr):
    n_pages = len(k_pages)
    lane = lax.broadcasted_iota(jnp.int32, (HEAD_DIM, LANES), 1)
    qT = qT_ref[...]
    vnT = vnT_ref[...]
    s_new = jnp.sum(q8_ref[...] * kn_ref[...], axis=1, keepdims=True)
    hit_new = mask_ref[n_pages:n_pages + 1, 0:1] > 0.5
    hits = [mask_ref[p:p + 1, :] > 0.5 for p in range(n_pages)]

    for h in range(N_HEADS):
        qcol = jnp.broadcast_to(qT[:, h:h + 1], (HEAD_DIM, LANES))
        s = [jnp.where(hits[p], jnp.sum(k_pages[p][h] * qcol, axis=0, keepdims=True), MASKED)
             for p in range(n_pages)]
        sn = jnp.where(hit_new, s_new[h:h + 1, :], MASKED)
        mx = s[0]
        for p in range(1, n_pages):
            mx = jnp.maximum(mx, s[p])
        m = jnp.maximum(jnp.max(mx, axis=1, keepdims=True), sn)
        pn = jnp.exp2(sn - m)
        acc = jnp.zeros((HEAD_DIM, LANES), F32)
        lsum = jnp.zeros((1, LANES), F32)
        for p in range(n_pages):
            pp = jnp.exp2(s[p] - m)
            lsum = lsum + pp
            acc = acc + v_pages[p][h] * pp
        l = jnp.sum(lsum, axis=1, keepdims=True) + pn
        o_h = (jnp.sum(acc, axis=1, keepdims=True) + vnT[:, h:h + 1] * pn) / l
        rows = slice(h * HEAD_DIM, (h + 1) * HEAD_DIM)
        oT_scr[rows, :] = jnp.where(lane == b, o_h, oT_scr[rows, :])


def _sample_merge_kernel(att_ref, aux_ref, sp_ref, sc_ref, hs_ref, wout_ref, poolw_ref, pscale_ref,
                         cw_ref, cb_ref, cg_ref, cbeta_ref, o_ref, *, pos):
    ub = aux_ref[:, A_UB:A_GB]
    uc = aux_ref[:, A_UC:A_GC]
    run = ub
    sums, cnts = [], []
    for j in range(1, POOL_WINDOWS[-1]):
        run = run + sp_ref[POOL_STATE - j]
        if j + 1 in POOL_WINDOWS:
            sums.append(run)
            cnts.append(float(min(pos + 1, j + 1)))
    pool = _pool_mix(sums, ub, cnts, poolw_ref, pscale_ref)

    y = cw_ref[CONV_STATE:CONV_WIDTH, :] * uc
    for j in range(CONV_STATE):
        y = y + cw_ref[j:j + 1, :] * sc_ref[j]
    conv = _conv_norm(y, cb_ref, cg_ref, cbeta_ref)
    o_ref[...] = _gate_project(att_ref[...], aux_ref, pool, conv, wout_ref, hs_ref[...])


def _sample_merge(att, aux, sp_t, sc_t, hs, wout, poolw, pscale, cw, cb, cg, cbeta, pos):
    args = (att, aux, sp_t, sc_t, hs, wout, poolw, pscale, cw, cb, cg, cbeta)
    full = lambda a: pl.BlockSpec(a.shape, lambda i, nd=a.ndim: (0,) * nd)
    return pl.pallas_call(
        functools.partial(_sample_merge_kernel, pos=pos),
        out_shape=jax.ShapeDtypeStruct(hs.shape, F32),
        grid=(1,),
        in_specs=[full(a) for a in args],
        out_specs=full(hs),
        compiler_params=pltpu.CompilerParams(
            dimension_semantics=("arbitrary",), vmem_limit_bytes=VMEM_LIMIT),
        name="sample_merge",
    )(*args)


def _rope_table(pos):
    pos = pos.astype(F32)

    def cs(half):
        inv = ROPE_THETA ** (-(jnp.arange(half, dtype=F32) / half))
        ang = pos[:, None] * inv[None, :]
        return jnp.cos(ang).T, jnp.sin(ang).T

    c8, s8 = cs(HEAD_DIM // 8)
    c4, s4 = cs(D_IDX // 8)
    return jnp.concatenate([c8, s8, c4, c4, -s4, s4], axis=0)


def _permute_w_in(w):
    d = w.shape[0]
    o = 0
    parts = {}
    for name, width in (("q", D_A), ("k", D_A), ("v", D_A), ("ga", D_A), ("qi", D_QI),
                        ("wi", N_IDX_HEADS), ("ki", D_IDX), ("ub", D_B), ("gb", D_B),
                        ("cv", D_C), ("cg", D_C), ("gc", D_C)):
        parts[name] = w[:, o:o + width]
        o += width
    pad = jnp.zeros((d, LANES - D_IDX - N_IDX_HEADS), w.dtype)
    order = ["q", "k", "v", "ga", "qi", "ub", "gb", "cv", "cg", "gc", "ki", "wi"]
    return jnp.concatenate([parts[n] for n in order] + [pad], axis=1).astype(BF16)


def _block_diag(pw):
    n_g, g, _ = pw.shape
    out = jnp.zeros((n_g * g, n_g * g), pw.dtype)
    for i in range(n_g):
        out = out.at[i * g:(i + 1) * g, i * g:(i + 1) * g].set(pw[i])
    return out.astype(BF16)


def kernel(x_prompt, x_sample, cache_k, cache_v, cache_kidx, state_pool, state_conv, page_table,
           meta_tokens, norm_g, w_in, q_norm_g, k_norm_g, pool_w, pool_scale, conv_w, conv_b,
           conv_norm_g, conv_norm_b, w_out):
    n_b, seq, d_model = x_prompt.shape
    n_q = x_sample.shape[0]
    depth = w_in.shape[0]
    n_pages = page_table.shape[1]
    past = n_pages * PAGE
    t_real = seq + N_META
    t_pad = -(-t_real // TM) * TM
    top_p = min(TOPK_MAX, t_real // 4)
    top_s = min(TOPK_MAX, (past + 1) // 4)

    meta = jnp.broadcast_to(meta_tokens[None].astype(F32), (n_b, N_META, d_model))
    tail = jnp.zeros((n_b, t_pad - t_real, d_model), F32)
    hp = jnp.concatenate([meta, x_prompt, tail], axis=1).reshape(n_b * t_pad, d_model)
    hs = x_sample.reshape(n_q, d_model)

    rope_p = _rope_table(jnp.arange(t_pad))
    rope_s = _rope_table(jnp.full((n_q,), past))
    cache_k_t = jnp.transpose(cache_k, (0, 1, 3, 4, 2))
    cache_v_t = jnp.transpose(cache_v, (0, 1, 3, 4, 2))
    cache_kidx_t = jnp.transpose(cache_kidx, (0, 1, 3, 2))

    outs = {name: [] for name in ("kp", "vp", "kip", "plp", "cvp", "ks", "vs", "kis", "pls", "cvs")}
    for l in range(depth):
        w_l = _permute_w_in(w_in[l])
        g_l = norm_g[l].reshape(1, d_model)
        qg = q_norm_g[l].reshape(HEAD_DIM, 1)
        kg = k_norm_g[l].reshape(HEAD_DIM, 1)
        wout = w_out[l].astype(BF16)
        poolw = _block_diag(pool_w[l])
        pscale = pool_scale[l].reshape(1, D_B)
        cw = conv_w[l]
        cb = conv_b[l].reshape(1, D_C)
        cg = conv_norm_g[l].reshape(1, D_C)
        cbeta = conv_norm_b[l].reshape(1, D_C)

        heads_last = lambda xT: jnp.transpose(xT.reshape(xT.shape[0], N_HEADS, HEAD_DIM, -1), (0, 3, 1, 2))

        q, kTf, kT, vTf, vb, aux, qi, wk, kiTf, kiT = _project(
            hp, n_b, t_real, TM, g_l, w_l, rope_p, qg, kg)
        q_s, kTf_s, _, vTf_s, _, aux_s, qi_s, wk_s, kiTf_s, _ = _project(
            hs, 1, n_q, n_q, g_l, w_l, rope_s, qg, kg)
        qi3 = qi_s.reshape(n_q, N_IDX_HEADS, D_IDX)
        wi3 = wk_s[:, WK_WI:WK_WI + N_IDX_HEADS].reshape(n_q, N_IDX_HEADS, 1)
        kin3 = wk_s[:, WK_KI:WK_KI + D_IDX].reshape(n_q, 1, D_IDX)
        mask = _sample_index(l, page_table, cache_kidx_t, qi3, wi3, kin3, top_s)
        q8 = q_s.astype(F32).reshape(n_q, N_HEADS, HEAD_DIM)
        k_new = heads_last(kTf_s).reshape(n_q, 1, N_HEADS, HEAD_DIM)
        v_new = heads_last(vTf_s).reshape(n_q, 1, N_HEADS, HEAD_DIM)

        att, att_s = _attention(
            l, page_table, q, qi, wk, kT, kiT, vb, cache_k_t, cache_v_t, q8, jnp.swapaxes(q8, 1, 2),
            k_new[:, 0], jnp.swapaxes(v_new[:, 0], 1, 2), mask.reshape(n_q, n_pages + 1, PAGE),
            n_b, t_pad, t_real, top_p)
        hp = _prompt_merge(att, aux, hp, wout, poolw, pscale, cw, cb, cg, cbeta, n_b, t_pad)
        aux3 = aux.reshape(n_b, t_pad, N_AUX)
        outs["kp"].append(heads_last(kTf))
        outs["vp"].append(heads_last(vTf))
        outs["kip"].append(jnp.swapaxes(kiTf, 1, 2))
        outs["plp"].append(aux3[:, t_real - POOL_STATE:t_real, A_UB:A_GB])
        outs["cvp"].append(aux3[:, t_real - CONV_STATE:t_real, A_UC:A_GC])

        sp_t = jnp.swapaxes(state_pool[l], 0, 1)
        sc_t = jnp.swapaxes(state_conv[l], 0, 1)
        hs = _sample_merge(att_s, aux_s, sp_t, sc_t, hs, wout, poolw, pscale, cw, cb, cg, cbeta, past)
        outs["ks"].append(k_new)
        outs["vs"].append(v_new)
        outs["kis"].append(jnp.swapaxes(kiTf_s, 1, 2).reshape(n_q, 1, D_IDX))
        outs["pls"].append(jnp.concatenate([state_pool[l][:, 1:], aux_s[:, None, A_UB:A_GB]], axis=1))
        outs["cvs"].append(jnp.concatenate([state_conv[l][:, 1:], aux_s[:, None, A_UC:A_GC]], axis=1))

    y_prompt = hp.reshape(n_b, t_pad, d_model)[:, N_META:t_real]
    y_sample = hs.reshape(n_q, 1, d_model)
    st = lambda name: jnp.stack(outs[name], axis=0)
    return (y_prompt, y_sample, st("kp"), st("vp"), st("kip"), st("plp"), st("cvp"),
            st("ks"), st("vs"), st("kis"), st("pls"), st("cvs"))
```

```python
import functools

import jax
import jax.numpy as jnp
from jax import lax
from jax.experimental import pallas as pl
from jax.experimental.pallas import tpu as pltpu

F32 = jnp.float32
BF16 = jnp.bfloat16

LANES = 128
SUBLANES = 8
N_META = 16
HEAD_DIM = 64
N_HEADS = 8
D_A = N_HEADS * HEAD_DIM
N_IDX_HEADS = 8
D_IDX = 32
D_QI = N_IDX_HEADS * D_IDX
D_B = 256
D_C = 256
POOL_WINDOWS = (2, 4, 8, 16)
POOL_GROUP = D_B // len(POOL_WINDOWS)
POOL_STATE = 15
CONV_WIDTH = 31
CONV_STATE = CONV_WIDTH - 1
HALO = 32
TOPK_MAX = 256
ROPE_THETA = 500000.0
EPS = 1e-6
PAGE = 128
LOG2E = 1.4426950408889634
Q_SCALE = HEAD_DIM ** -0.5 * LOG2E
IDX_SCALE = (D_IDX ** -0.5) * (N_IDX_HEADS ** -0.5)

QB = 128
CK = 384
TM = 384
MAX_BISECT = 64
BIG_INDEX = 1.0e9
MASKED = -1.0e30
TINY = 1.1754944e-38
ACC_ROWS = 64
VMEM_LIMIT = 56 * 1024 * 1024

C_Q, C_K, C_V, C_GA, C_QI = 0, 512, 1024, 1536, 2048
C_UB, C_GB, C_CV, C_CG, C_GC, C_KI = 2304, 2560, 2816, 3072, 3328, 3584
N_PERM = C_KI + LANES
A_GA, A_UB, A_GB, A_UC, A_GC, N_AUX = 0, 512, 768, 1024, 1280, 1536
WK_KI, WK_WI = 0, D_IDX


def _sigmoid(x):
    return 0.5 * jnp.tanh(0.5 * x) + 0.5


def _silu(x):
    return x * _sigmoid(x)


def _proj_kernel(x_ref, g_ref, w_ref, rope_ref, qg_ref, kg_ref,
                 q_ref, kTf_ref, kT_ref, vTf_ref, vb_ref, aux_ref, qi_ref, wk_ref, kiTf_ref, kiT_ref,
                 t_scr):
    x = x_ref[...]
    ms = jnp.mean(x * x, axis=-1, keepdims=True)
    h = (x * lax.rsqrt(ms + EPS) * g_ref[...]).astype(BF16)

    def proj(c0, c1):
        return jnp.dot(h, w_ref[:, c0:c1], preferred_element_type=F32)

    cos8 = rope_ref[0:8, :]
    sin8 = rope_ref[8:16, :]
    cc4 = rope_ref[16:24, :]
    ss4 = rope_ref[24:32, :]

    zT = proj(C_Q, C_V).T
    for j in range(2 * N_HEADS):
        xh = zT[j * HEAD_DIM:(j + 1) * HEAD_DIM, :]
        g = qg_ref[...] if j < N_HEADS else kg_ref[...]
        xn = xh * lax.rsqrt(jnp.mean(xh * xh, axis=0, keepdims=True) + EPS) * g
        x1 = xn[0:8, :]
        x2 = xn[8:16, :]
        r0 = j * HEAD_DIM
        t_scr[r0:r0 + 8, :] = x1 * cos8 - x2 * sin8
        t_scr[r0 + 8:r0 + 16, :] = x2 * cos8 + x1 * sin8
        t_scr[r0 + 16:r0 + HEAD_DIM, :] = xn[16:HEAD_DIM, :]
    q_ref[...] = (t_scr[0:D_A, :].T * Q_SCALE).astype(BF16)
    kT = t_scr[D_A:2 * D_A, :]
    kTf_ref[...] = kT
    kT_ref[...] = kT.astype(BF16)

    zv = proj(C_V, C_GA)
    vTf_ref[...] = zv.T
    vb_ref[...] = zv.astype(BF16)

    aux_ref[:, A_GA:A_UB] = proj(C_GA, C_QI)
    aux_ref[:, A_UB:A_UC] = proj(C_UB, C_CV)
    zc = proj(C_CV, C_KI)
    aux_ref[:, A_UC:A_GC] = zc[:, 0:D_C] * _sigmoid(zc[:, D_C:2 * D_C])
    aux_ref[:, A_GC:N_AUX] = zc[:, 2 * D_C:3 * D_C]

    ziT = proj(C_QI, C_UB).T
    parts = []
    for hh in range(N_IDX_HEADS):
        r0 = hh * D_IDX
        x8 = ziT[r0:r0 + 8, :]
        parts.append(x8 * cc4 + pltpu.roll(x8, 4, 0) * ss4)
        parts.append(ziT[r0 + 8:r0 + D_IDX, :])
    qi_ref[...] = jnp.concatenate(parts, axis=0).T.astype(BF16)

    zwT = proj(C_KI, N_PERM).T
    x8 = zwT[0:8, :]
    ki = jnp.concatenate([x8 * cc4 + pltpu.roll(x8, 4, 0) * ss4, zwT[8:D_IDX, :]], axis=0)
    wi = zwT[D_IDX:D_IDX + N_IDX_HEADS, :] * IDX_SCALE
    wk_ref[...] = jnp.concatenate([ki, wi, zwT[D_IDX + N_IDX_HEADS:, :]], axis=0).T
    kiTf_ref[...] = ki
    kiT_ref[...] = jnp.concatenate([ki] * (LANES // D_IDX), axis=0).astype(BF16)


def _project(x2d, n_batch, t_real, tm, g, w, rope, qg, kg):
    rows = x2d.shape[0]
    t = rows // n_batch
    nt = t // tm
    d = x2d.shape[1]
    row_map = lambda b, i: (b * nt + i, 0)
    col_map = lambda b, i: (b, 0, i)
    const = lambda b, i: (0, 0)
    out_shape = (
        jax.ShapeDtypeStruct((rows, D_A), BF16),
        jax.ShapeDtypeStruct((n_batch, D_A, t_real), F32),
        jax.ShapeDtypeStruct((n_batch, D_A, t), BF16),
        jax.ShapeDtypeStruct((n_batch, D_A, t_real), F32),
        jax.ShapeDtypeStruct((rows, D_A), BF16),
        jax.ShapeDtypeStruct((rows, N_AUX), F32),
        jax.ShapeDtypeStruct((rows, D_QI), BF16),
        jax.ShapeDtypeStruct((rows, LANES), F32),
        jax.ShapeDtypeStruct((n_batch, D_IDX, t_real), F32),
        jax.ShapeDtypeStruct((n_batch, LANES, t), BF16),
    )
    out_specs = (
        pl.BlockSpec((tm, D_A), row_map),
        pl.BlockSpec((None, D_A, tm), col_map),
        pl.BlockSpec((None, D_A, tm), col_map),
        pl.BlockSpec((None, D_A, tm), col_map),
        pl.BlockSpec((tm, D_A), row_map),
        pl.BlockSpec((tm, N_AUX), row_map),
        pl.BlockSpec((tm, D_QI), row_map),
        pl.BlockSpec((tm, LANES), row_map),
        pl.BlockSpec((None, D_IDX, tm), col_map),
        pl.BlockSpec((None, LANES, tm), col_map),
    )
    in_specs = [
        pl.BlockSpec((tm, d), row_map),
        pl.BlockSpec((1, d), const),
        pl.BlockSpec((d, N_PERM), const),
        pl.BlockSpec((32, tm), lambda b, i: (0, i)),
        pl.BlockSpec((HEAD_DIM, 1), const),
        pl.BlockSpec((HEAD_DIM, 1), const),
    ]
    return pl.pallas_call(
        _proj_kernel,
        out_shape=out_shape,
        grid=(n_batch, nt),
        in_specs=in_specs,
        out_specs=out_specs,
        scratch_shapes=[pltpu.VMEM((2 * D_A, tm), F32)],
        compiler_params=pltpu.CompilerParams(
            dimension_semantics=("parallel", "parallel"), vmem_limit_bytes=VMEM_LIMIT),
        name="proj",
    )(x2d, g, w, rope, qg, kg)


def _fold_t(it_scr, n_blk, ck, fn, inits):
    per = ck // LANES

    def body(c, accs):
        row = pl.multiple_of(c * ck, LANES)
        return fn(accs, it_scr[pl.ds(row, ck), :], row)

    def tail(c, accs):
        row = pl.multiple_of(c * LANES, LANES)
        return fn(accs, it_scr[pl.ds(row, LANES), :], row)

    accs = tuple(jnp.full((ACC_ROWS, LANES), v, F32) for v in inits)
    accs = lax.fori_loop(0, n_blk // per, body, accs)
    return lax.fori_loop((n_blk // per) * per, n_blk, tail, accs)


def _map_t(it_scr, n_blk, ck, fn):
    per = ck // LANES

    def body(c, carry):
        row = pl.multiple_of(c * ck, LANES)
        it_scr[pl.ds(row, ck), :] = fn(it_scr[pl.ds(row, ck), :], row)
        return carry

    def tail(c, carry):
        row = pl.multiple_of(c * LANES, LANES)
        it_scr[pl.ds(row, LANES), :] = fn(it_scr[pl.ds(row, LANES), :], row)
        return carry

    lax.fori_loop(0, n_blk // per, body, 0)
    lax.fori_loop((n_blk // per) * per, n_blk, tail, 0)


def _vreg_fold(x, op):
    return op(x.reshape(x.shape[0] // ACC_ROWS, ACC_ROWS, LANES), axis=0)


def _count_ge_t(it_scr, n_blk, ck, mid):
    (acc,) = _fold_t(
        it_scr, n_blk, ck,
        lambda a, blk, row: (a[0] + _vreg_fold(jnp.where(blk >= mid, 1.0, 0.0), jnp.sum),), (0.0,))
    return jnp.sum(acc, axis=0, keepdims=True)


def _select_bounds_t(it_scr, n_blk, ck, top_k, trivial):
    kf = float(top_k)
    ninf = -jnp.inf
    pinf = jnp.inf
    land, lor, lnot = jnp.logical_and, jnp.logical_or, jnp.logical_not
    passes = 4

    mn, mx = _fold_t(
        it_scr, n_blk, ck,
        lambda a, blk, row: (jnp.minimum(a[0], _vreg_fold(jnp.where(blk == ninf, pinf, blk), jnp.min)),
                             jnp.maximum(a[1], _vreg_fold(blk, jnp.max))),
        (pinf, ninf))
    cpos, cnn = _fold_t(
        it_scr, n_blk, ck,
        lambda a, blk, row: (a[0] + _vreg_fold(jnp.where(blk > 0.0, 1.0, 0.0), jnp.sum),
                             a[1] + _vreg_fold(jnp.where(blk >= 0.0, 1.0, 0.0), jnp.sum)),
        (0.0, 0.0))
    col_min = jnp.min(mn, axis=0, keepdims=True)
    col_max = jnp.max(mx, axis=0, keepdims=True)
    c_pos = jnp.sum(cpos, axis=0, keepdims=True)
    c_nn = jnp.sum(cnn, axis=0, keepdims=True)
    c_top = _count_ge_t(it_scr, n_blk, ck, col_max)

    top_tie = land(c_top >= kf, lnot(trivial))
    zero_tie = land(land(c_pos < kf, c_nn >= kf), lnot(lor(trivial, top_tie)))
    neg_side = c_nn < kf
    lo0 = jnp.where(top_tie, col_max, jnp.where(zero_tie, 0.0, jnp.where(neg_side, col_min, TINY)))
    lo0 = jnp.where(trivial, col_min, lo0)
    hi0 = jnp.where(lor(trivial, top_tie), pinf,
                    jnp.where(zero_tie, jnp.where(c_nn == kf, pinf, TINY),
                              jnp.where(neg_side, 0.0, col_max)))
    pos_exact = land(lnot(neg_side), c_pos == kf)
    hi0 = jnp.where(land(pos_exact, lnot(lor(trivial, lor(top_tie, zero_tie)))), pinf, hi0)
    done0 = lor(lor(trivial, land(top_tie, c_top == kf)), lor(land(zero_tie, c_nn == kf), pos_exact))
    res0 = jnp.where(done0, 1.0, 0.0)
    act0 = jnp.where(lor(done0, lor(top_tie, zero_tie)), 0.0, 1.0)

    def bis_cond(st):
        return jnp.logical_and(st[0] < MAX_BISECT, st[1] > 0.0)

    def bis_body(st):
        it, _, lo, hi, res, act = st
        for _ in range(passes):
            mid = 0.5 * lo + 0.5 * hi
            ok = land(act > 0.0, land(mid > lo, mid < hi))
            c = _count_ge_t(it_scr, n_blk, ck, mid)
            eq = land(ok, c == kf)
            gt = land(ok, c > kf)
            lt = land(ok, c < kf)
            lo = jnp.where(lor(eq, gt), mid, lo)
            hi = jnp.where(eq, pinf, jnp.where(lt, mid, hi))
            res = jnp.where(eq, 1.0, res)
            act = jnp.where(lor(eq, lnot(ok)), 0.0, act)
        return it + passes, jnp.sum(act), lo, hi, res, act

    _, _, lo, hi, res, _ = lax.while_loop(
        bis_cond, bis_body, (jnp.int32(0), jnp.sum(act0), lo0, hi0, res0, act0))

    def cut_band():
        need = kf - _count_ge_t(it_scr, n_blk, ck, hi)

        def to_index(blk, row):
            idx = (lax.broadcasted_iota(jnp.int32, blk.shape, 0) + row).astype(F32)
            return jnp.where(land(blk >= lo, blk < hi), idx, BIG_INDEX)

        _map_t(it_scr, n_blk, ck, to_index)

        def count_band(j):
            (acc,) = _fold_t(
                it_scr, n_blk, ck,
                lambda a, blk, row: (a[0] + _vreg_fold(jnp.where(blk <= j, 1.0, 0.0), jnp.sum),), (0.0,))
            return jnp.sum(acc, axis=0, keepdims=True)

        def j_cond(st):
            return jnp.logical_and(st[0] < 32, st[1] > 0.0)

        def j_body(st):
            it, _, jl, jh, jc, open_ = st
            jm = jnp.floor((jl + jh + 1.0) * 0.5)
            c = count_band(jm)
            is_open = open_ > 0.0
            eq = land(is_open, c == need)
            jl = jnp.where(land(is_open, c < need), jm, jl)
            jh = jnp.where(land(is_open, c > need), jm, jh)
            jc = jnp.where(eq, jm, jc)
            open_ = jnp.where(eq, 0.0, open_)
            return it + 1, jnp.sum(open_), jl, jh, jc, open_

        open0 = 1.0 - res
        full = lambda v: jnp.full((1, LANES), v, F32)
        st0 = (jnp.int32(0), jnp.sum(open0), full(-1.0), full(float(it_scr.shape[0] - 1)),
               full(BIG_INDEX), open0)
        return lax.while_loop(j_cond, j_body, st0)[4]

    jcut = lax.cond(jnp.sum(1.0 - res) > 0.0, cut_band, lambda: jnp.full((1, LANES), BIG_INDEX, F32))
    return lo, hi, jcut


def _chunked(n_blk, body):
    per = CK // LANES
    n_full = n_blk // per
    rem = n_blk - n_full * per
    base = (n_full // 3) * 3

    def run(c0, count):
        for u in range(count):
            body(pl.multiple_of((c0 + u) * CK, LANES), CK)

    def triple(j, carry):
        run(3 * j, 3)
        return carry

    lax.fori_loop(0, n_full // 3, triple, 0)
    for count in (2, 1):
        pl.when(n_full - base == count)(functools.partial(run, base, count))
    for r in range(1, per):
        pl.when(rem == r)(functools.partial(body, pl.multiple_of(n_full * CK, LANES), r * LANES))


def _as_column(row):
    return jnp.broadcast_to(row, (LANES, LANES)).T


def _selected(blk, col, lo, hi, jcut):
    lane = lax.broadcasted_iota(jnp.int32, blk.shape, 1)
    idx = (lane + col).astype(F32)
    return jnp.logical_or(blk >= hi, jnp.logical_and(blk >= lo, idx <= jcut))


def _attn_kernel(pt_ref, q_ref, qi_ref, wk_ref, kT_ref, kiT_ref, v_ref, *rest, t_real, top_k, n_pages):
    del pt_ref
    k_pages = rest[:n_pages]
    v_pages = rest[n_pages:2 * n_pages]
    (q8_ref, qT_ref, kn_ref, vnT_ref, mask_ref, o_ref, os_ref,
     i_scr, it_scr, s_scr, m_scr, l_scr, acc_scr, oT_scr) = rest[2 * n_pages:]
    i = pl.program_id(1)
    q_pos = i * QB + lax.broadcasted_iota(jnp.int32, (QB, 1), 0)
    lane = lax.broadcasted_iota(jnp.int32, (QB, LANES), 1)

    qi = qi_ref[...]
    q = q_ref[...]
    qiz, qz = [], []
    per = LANES // D_IDX
    for h in range(N_IDX_HEADS):
        blk = qi[:, (h // per) * LANES:(h // per + 1) * LANES]
        qiz.append(jnp.where(lane // D_IDX == h % per, blk, jnp.zeros_like(blk)))
    for h in range(N_HEADS):
        blk = q[:, (h // 2) * LANES:(h // 2 + 1) * LANES]
        qz.append(jnp.where(lane // HEAD_DIM == h % 2, blk, jnp.zeros_like(blk)))
    w = wk_ref[:, WK_WI:WK_WI + N_IDX_HEADS]

    def score_chunk(col, width):
        kc = kiT_ref[:, pl.ds(col, width)]
        acc = jnp.zeros((QB, width), F32)
        for h in range(N_IDX_HEADS):
            d = jnp.dot(qiz[h], kc, preferred_element_type=F32)
            acc = acc + w[:, h:h + 1] * jnp.maximum(d, 0.0)
        key_pos = col + lax.broadcasted_iota(jnp.int32, (QB, width), 1)
        sc = jnp.where(key_pos <= q_pos, acc, -jnp.inf)
        i_scr[:, pl.ds(col, width)] = sc
        for k in range(width // LANES):
            row = pl.multiple_of(col + k * LANES, LANES)
            it_scr[pl.ds(row, LANES), :] = sc[:, k * LANES:(k + 1) * LANES].T

    _chunked(i + 1, score_chunk)

    q_row = i * QB + lax.broadcasted_iota(jnp.int32, (1, LANES), 1)
    trivial = jnp.logical_or(q_row + 1 <= top_k, q_row >= t_real)
    lo, hi, jcut = (_as_column(r)[:, 0:1] for r in _select_bounds_t(it_scr, i + 1, CK, top_k, trivial))

    m_scr[...] = jnp.full(m_scr.shape, MASKED, F32)

    def logits_chunk(col, width):
        sel = _selected(i_scr[:, pl.ds(col, width)], col, lo, hi, jcut)
        bias = jnp.where(sel, 0.0, MASKED)
        for h in range(N_HEADS):
            pr = h // 2
            kc = kT_ref[pr * LANES:(pr + 1) * LANES, pl.ds(col, width)]
            s = jnp.dot(qz[h], kc, preferred_element_type=F32) + bias
            s_scr[h, :, pl.ds(col, width)] = s
            mx = s[:, 0:LANES]
            for k in range(1, width // LANES):
                mx = jnp.maximum(mx, s[:, k * LANES:(k + 1) * LANES])
            m_scr[h] = jnp.maximum(m_scr[h], mx)

    _chunked(i + 1, logits_chunk)

    for h in range(N_HEADS):
        m_scr[h] = jnp.broadcast_to(jnp.max(m_scr[h], axis=1, keepdims=True), (QB, LANES))
    l_scr[...] = jnp.zeros(l_scr.shape, F32)
    acc_scr[...] = jnp.zeros(acc_scr.shape, F32)

    def value_chunk(col, width):
        for h in range(N_HEADS):
            pr = h // 2
            m = m_scr[h]
            ps = [jnp.exp2(s_scr[h, :, pl.ds(pl.multiple_of(col + k * LANES, LANES), LANES)] - m)
                  for k in range(width // LANES)]
            lsum = ps[0]
            for pk in ps[1:]:
                lsum = lsum + pk
            l_scr[h] = l_scr[h] + lsum
            p = (ps[0] if len(ps) == 1 else jnp.concatenate(ps, axis=1)).astype(BF16)
            vc = v_ref[pl.ds(col, width), pr * LANES:(pr + 1) * LANES]
            acc_scr[h] = acc_scr[h] + jnp.dot(p, vc, preferred_element_type=F32)

    _chunked(i + 1, value_chunk)

    for pr in range(N_HEADS // 2):
        first = lane < HEAD_DIM
        num = jnp.where(first, acc_scr[2 * pr], acc_scr[2 * pr + 1])
        den = jnp.where(first, jnp.sum(l_scr[2 * pr], axis=1, keepdims=True),
                        jnp.sum(l_scr[2 * pr + 1], axis=1, keepdims=True))
        o_ref[:, pr * LANES:(pr + 1) * LANES] = num / den

    step = pl.program_id(0) * pl.num_programs(1) + i
    n_steps = pl.num_programs(0) * pl.num_programs(1)

    @pl.when(step == 0)
    def _():
        oT_scr[...] = jnp.zeros(oT_scr.shape, F32)

    @pl.when(step < os_ref.shape[0])
    def _():
        _sample_attend(step, k_pages, v_pages, q8_ref, qT_ref, kn_ref, vnT_ref, mask_ref, oT_scr)

    @pl.when(step == n_steps - 1)
    def _():
        os_ref[...] = oT_scr[...].T


def _attention(layer, page_table, q, qi, wk, kT, kiT, vb, cache_k_t, cache_v_t, q8, qT, kn, vnT, mask3,
               n_batch, t_pad, t_real, top_k):
    nq = t_pad // QB
    n_q, n_pages = page_table.shape
    assert n_q == LANES, "one output lane per sample query"
    assert n_batch * nq >= n_q, "one sample query per grid step"
    kern = functools.partial(_attn_kernel, t_real=t_real, top_k=top_k, n_pages=n_pages)
    row_map = lambda b, i, pt: (b * nq + i, 0)
    query = lambda b, i: jnp.minimum(b * nq + i, n_q - 1)
    per_q = lambda b, i, pt: (query(b, i), 0, 0)
    once = pl.Buffered(1)

    def page_spec(p):
        return pl.BlockSpec((None, None, N_HEADS, HEAD_DIM, PAGE),
                            lambda b, i, pt: (layer, pt[query(b, i), p], 0, 0, 0))

    in_specs = [
        pl.BlockSpec((QB, D_A), row_map),
        pl.BlockSpec((QB, D_QI), row_map),
        pl.BlockSpec((QB, LANES), row_map),
        pl.BlockSpec((None, D_A, t_pad), lambda b, i, pt: (b, 0, 0), pipeline_mode=once),
        pl.BlockSpec((None, LANES, t_pad), lambda b, i, pt: (b, 0, 0), pipeline_mode=once),
        pl.BlockSpec((t_pad, D_A), lambda b, i, pt: (b, 0), pipeline_mode=once),
    ]
    in_specs += [page_spec(p) for p in range(n_pages)] + [page_spec(p) for p in range(n_pages)]
    in_specs += [
        pl.BlockSpec((None, N_HEADS, HEAD_DIM), per_q),
        pl.BlockSpec((None, HEAD_DIM, N_HEADS), per_q),
        pl.BlockSpec((None, N_HEADS, HEAD_DIM), per_q),
        pl.BlockSpec((None, HEAD_DIM, N_HEADS), per_q),
        pl.BlockSpec((None, n_pages + 1, PAGE), per_q),
    ]
    return pl.pallas_call(
        kern,
        out_shape=(jax.ShapeDtypeStruct((n_batch * t_pad, D_A), F32),
                   jax.ShapeDtypeStruct((n_q, D_A), F32)),
        grid_spec=pltpu.PrefetchScalarGridSpec(
            num_scalar_prefetch=1,
            grid=(n_batch, nq),
            in_specs=in_specs,
            out_specs=(pl.BlockSpec((QB, D_A), row_map),
                       pl.BlockSpec((n_q, D_A), lambda b, i, pt: (0, 0))),
            scratch_shapes=[
                pltpu.VMEM((QB, t_pad), F32),
                pltpu.VMEM((t_pad, QB), F32),
                pltpu.VMEM((N_HEADS, QB, t_pad), F32),
                pltpu.VMEM((N_HEADS, QB, LANES), F32),
                pltpu.VMEM((N_HEADS, QB, LANES), F32),
                pltpu.VMEM((N_HEADS, QB, LANES), F32),
                pltpu.VMEM((D_A, n_q), F32),
            ],
        ),
        compiler_params=pltpu.CompilerParams(
            dimension_semantics=("arbitrary", "arbitrary"), vmem_limit_bytes=VMEM_LIMIT),
        name="attention",
    )(page_table, q, qi, wk, kT, kiT, vb, *([cache_k_t] * n_pages), *([cache_v_t] * n_pages),
      q8, qT, kn, vnT, mask3)


def _gate_project(att, aux_ref, pool, conv, wout_ref, resid):
    ya = att * _silu(aux_ref[:, A_GA:A_UB])
    yb = pool * _silu(aux_ref[:, A_GB:A_UC])
    yc = conv * _silu(aux_ref[:, A_GC:N_AUX])
    y = jnp.concatenate([ya, yb, yc], axis=1).astype(BF16)
    return resid + jnp.dot(y, wout_ref[...], preferred_element_type=F32)


def _pool_mix(sums, u, cnts, poolw_ref, pscale_ref):
    lane = lax.broadcasted_iota(jnp.int32, u.shape, 1)
    d = sums[-1] / cnts[-1]
    for g in range(len(POOL_WINDOWS) - 2, -1, -1):
        d = jnp.where(lane < (g + 1) * POOL_GROUP, sums[g] / cnts[g], d)
    d = (d - u).astype(BF16)
    return jnp.dot(d, poolw_ref[...], preferred_element_type=F32) * pscale_ref[...]


def _conv_norm(y, cb_ref, cg_ref, cbeta_ref):
    y = y + cb_ref[...]
    mu = jnp.mean(y, axis=-1, keepdims=True)
    yc = y - mu
    var = jnp.mean(yc * yc, axis=-1, keepdims=True)
    return _silu(yc * lax.rsqrt(var + EPS) * cg_ref[...] + cbeta_ref[...])


def _row_windows(ext_scr, rot_scr, tm):
    n = rot_scr.shape[1]
    for r in range(1, SUBLANES):
        rot_scr[r - 1] = ext_scr[r:r + n, :]

    def window(off):
        r = off % SUBLANES
        a = off - r
        return ext_scr[a:a + tm, :] if r == 0 else rot_scr[r - 1, a:a + tm, :]

    return window


def _merge_kernel(att_ref, aux_ref, hub_ref, huc_ref, hp_ref, wout_ref, poolw_ref, pscale_ref,
                  cw_ref, cb_ref, cg_ref, cbeta_ref, o_ref, ub_scr, uc_scr, ub_rot, uc_rot):
    i = pl.program_id(1)
    tm = att_ref.shape[0]
    first = i == 0
    ub = aux_ref[:, A_UB:A_GB]
    uc = aux_ref[:, A_UC:A_GC]
    ub_scr[0:HALO, :] = jnp.where(first, 0.0, hub_ref[...])
    uc_scr[0:HALO, :] = jnp.where(first, 0.0, huc_ref[...])
    ub_scr[HALO:HALO + tm, :] = ub
    uc_scr[HALO:HALO + tm, :] = uc
    ub_win = _row_windows(ub_scr, ub_rot, tm)
    uc_win = _row_windows(uc_scr, uc_rot, tm)

    pos = i * tm + lax.broadcasted_iota(jnp.int32, (tm, 1), 0)
    run = ub
    sums, cnts = [], []
    for j in range(1, POOL_WINDOWS[-1]):
        run = run + ub_win(HALO - j)
        if j + 1 in POOL_WINDOWS:
            sums.append(run)
            cnts.append(jnp.minimum(pos + 1, j + 1).astype(F32))
    pool = _pool_mix(sums, ub, cnts, poolw_ref, pscale_ref)

    y = jnp.zeros((tm, D_C), F32)
    for j in range(CONV_WIDTH):
        y = y + cw_ref[j:j + 1, :] * uc_win(HALO - CONV_STATE + j)
    conv = _conv_norm(y, cb_ref, cg_ref, cbeta_ref)

    o_ref[...] = _gate_project(att_ref[...], aux_ref, pool, conv, wout_ref, hp_ref[...])


def _prompt_merge(att, aux, hp, wout, poolw, pscale, cw, cb, cg, cbeta, n_batch, t_pad):
    nt = t_pad // TM
    hb = TM // HALO
    row_map = lambda b, i: (b * nt + i, 0)
    const = lambda b, i: (0, 0)
    d = hp.shape[1]

    def halo_map(col_block):
        return lambda b, i: (jnp.maximum((b * nt + i) * hb - 1, 0), col_block)

    return pl.pallas_call(
        _merge_kernel,
        out_shape=jax.ShapeDtypeStruct(hp.shape, F32),
        grid=(n_batch, nt),
        in_specs=[
            pl.BlockSpec((TM, D_A), row_map),
            pl.BlockSpec((TM, N_AUX), row_map),
            pl.BlockSpec((HALO, D_B), halo_map(A_UB // D_B)),
            pl.BlockSpec((HALO, D_C), halo_map(A_UC // D_C)),
            pl.BlockSpec((TM, d), row_map),
            pl.BlockSpec(wout.shape, const),
            pl.BlockSpec(poolw.shape, const),
            pl.BlockSpec((1, D_B), const),
            pl.BlockSpec(cw.shape, const),
            pl.BlockSpec((1, D_C), const),
            pl.BlockSpec((1, D_C), const),
            pl.BlockSpec((1, D_C), const),
        ],
        out_specs=pl.BlockSpec((TM, d), row_map),
        scratch_shapes=[pltpu.VMEM((HALO + TM, D_B), F32), pltpu.VMEM((HALO + TM, D_C), F32),
                        pltpu.VMEM((SUBLANES - 1, HALO + TM - SUBLANES, D_B), F32),
                        pltpu.VMEM((SUBLANES - 1, HALO + TM - SUBLANES, D_C), F32)],
        compiler_params=pltpu.CompilerParams(
            dimension_semantics=("parallel", "parallel"), vmem_limit_bytes=VMEM_LIMIT),
        name="prompt_merge",
    )(att, aux, aux, aux, hp, wout, poolw, pscale, cw, cb, cg, cbeta)


def _sample_index_kernel(pt_ref, *refs, n_pages, top_k):
    del pt_ref
    ki_pages = refs[:SUBLANES * n_pages]
    qi_ref, wi_ref, kin_ref, mask_ref, i_scr, it_scr = refs[SUBLANES * n_pages:]
    g = pl.program_id(0)
    past = n_pages * PAGE
    width = i_scr.shape[1]
    sub = lax.broadcasted_iota(jnp.int32, (SUBLANES, width), 0)
    lane1 = lax.broadcasted_iota(jnp.int32, (1, LANES), 1)

    rows = jnp.full((SUBLANES, width), -jnp.inf, F32)
    for j in range(SUBLANES):
        qi = qi_ref[j]
        w = wi_ref[j]
        pages = ki_pages[j * n_pages:(j + 1) * n_pages]
        kp = jnp.concatenate([pg[...] for pg in pages], axis=1).astype(BF16)
        d = jnp.dot(qi, kp, preferred_element_type=F32)
        sc = jnp.sum(w * jnp.maximum(d, 0.0), axis=0, keepdims=True)
        dn = jnp.sum(qi.astype(F32) * kin_ref[j], axis=1, keepdims=True)
        sn = jnp.sum(w * jnp.maximum(dn, 0.0), axis=0, keepdims=True)
        row = jnp.concatenate([sc, jnp.where(lane1 == 0, sn, -jnp.inf)], axis=1)
        rows = jnp.where(sub == j, row, rows)
    i_scr[pl.ds(pl.multiple_of(g * SUBLANES, SUBLANES), SUBLANES), :] = rows

    @pl.when(g == pl.num_programs(0) - 1)
    def _():
        width = i_scr.shape[1]
        for k in range(width // LANES):
            it_scr[k * LANES:(k + 1) * LANES, :] = i_scr[:, k * LANES:(k + 1) * LANES].T
        trivial = jnp.full((1, LANES), past + 1 <= top_k)
        lo, hi, jcut = (_as_column(r)[:, 0:1] for r in _select_bounds_t(it_scr, width // LANES, width, top_k, trivial))
        for k in range(width // LANES):
            sel = _selected(i_scr[:, k * LANES:(k + 1) * LANES], k * LANES, lo, hi, jcut)
            mask_ref[:, k * LANES:(k + 1) * LANES] = jnp.where(sel, 1.0, 0.0)


def _sample_index(layer, page_table, cache_kidx_t, qi3, wi3, kin3, top_k):
    n_q, n_pages = page_table.shape
    width = (n_pages + 1) * PAGE
    kern = functools.partial(_sample_index_kernel, n_pages=n_pages, top_k=top_k)

    def page_map(j, p):
        return lambda g, pt: (layer, pt[g * SUBLANES + j, p], 0, 0)

    group = lambda g, pt: (g, 0, 0)
    in_specs = [pl.BlockSpec((None, None, D_IDX, PAGE), page_map(j, p))
                for j in range(SUBLANES) for p in range(n_pages)]
    in_specs += [
        pl.BlockSpec((SUBLANES, N_IDX_HEADS, D_IDX), group),
        pl.BlockSpec((SUBLANES, N_IDX_HEADS, 1), group),
        pl.BlockSpec((SUBLANES, 1, D_IDX), group),
    ]
    return pl.pallas_call(
        kern,
        out_shape=jax.ShapeDtypeStruct((n_q, width), F32),
        grid_spec=pltpu.PrefetchScalarGridSpec(
            num_scalar_prefetch=1,
            grid=(n_q // SUBLANES,),
            in_specs=in_specs,
            out_specs=pl.BlockSpec((n_q, width), lambda b, pt: (0, 0)),
            scratch_shapes=[pltpu.VMEM((n_q, width), F32), pltpu.VMEM((width, n_q), F32)],
        ),
        compiler_params=pltpu.CompilerParams(
            dimension_semantics=("arbitrary",), vmem_limit_bytes=VMEM_LIMIT),
        name="sample_index",
    )(page_table, *([cache_kidx_t] * (SUBLANES * n_pages)), qi3, wi3, kin3)


def _sample_attend(b, k_pages, v_pages, q8_ref, qT_ref, kn_ref, vnT_ref, mask_ref, oT_scr):
    n_pages = len(k_pages)
    lane = lax.broadcasted_iota(jnp.int32, (HEAD_DIM, LANES), 1)
    qT = qT_ref[...]
    vnT = vnT_ref[...]
    s_new = jnp.sum(q8_ref[...] * kn_ref[...], axis=1, keepdims=True)
    hit_new = mask_ref[n_pages:n_pages + 1, 0:1] > 0.5
    hits = [mask_ref[p:p + 1, :] > 0.5 for p in range(n_pages)]

    for h in range(N_HEADS):
        qcol = jnp.broadcast_to(qT[:, h:h + 1], (HEAD_DIM, LANES))
        s = [jnp.where(hits[p], jnp.sum(k_pages[p][h] * qcol, axis=0, keepdims=True), MASKED)
             for p in range(n_pages)]
        sn = jnp.where(hit_new, s_new[h:h + 1, :], MASKED)
        mx = s[0]
        for p in range(1, n_pages):
            mx = jnp.maximum(mx, s[p])
        m = jnp.maximum(jnp.max(mx, axis=1, keepdims=True), sn)
        pn = jnp.exp2(sn - m)
        acc = jnp.zeros((HEAD_DIM, LANES), F32)
        lsum = jnp.zeros((1, LANES), F32)
        for p in range(n_pages):
            pp = jnp.exp2(s[p] - m)
            lsum = lsum + pp
            acc = acc + v_pages[p][h] * pp
        l = jnp.sum(lsum, axis=1, keepdims=True) + pn
        o_h = (jnp.sum(acc, axis=1, keepdims=True) + vnT[:, h:h + 1] * pn) / l
        rows = slice(h * HEAD_DIM, (h + 1) * HEAD_DIM)
        oT_scr[rows, :] = jnp.where(lane == b, o_h, oT_scr[rows, :])


def _sample_merge_kernel(att_ref, aux_ref, sp_ref, sc_ref, hs_ref, wout_ref, poolw_ref, pscale_ref,
                         cw_ref, cb_ref, cg_ref, cbeta_ref, o_ref, *, pos):
    ub = aux_ref[:, A_UB:A_GB]
    uc = aux_ref[:, A_UC:A_GC]
    run = ub
    sums, cnts = [], []
    for j in range(1, POOL_WINDOWS[-1]):
        run = run + sp_ref[POOL_STATE - j]
        if j + 1 in POOL_WINDOWS:
            sums.append(run)
            cnts.append(float(min(pos + 1, j + 1)))
    pool = _pool_mix(sums, ub, cnts, poolw_ref, pscale_ref)

    y = cw_ref[CONV_STATE:CONV_WIDTH, :] * uc
    for j in range(CONV_STATE):
        y = y + cw_ref[j:j + 1, :] * sc_ref[j]
    conv = _conv_norm(y, cb_ref, cg_ref, cbeta_ref)
    o_ref[...] = _gate_project(att_ref[...], aux_ref, pool, conv, wout_ref, hs_ref[...])


def _sample_merge(att, aux, sp_t, sc_t, hs, wout, poolw, pscale, cw, cb, cg, cbeta, pos):
    args = (att, aux, sp_t, sc_t, hs, wout, poolw, pscale, cw, cb, cg, cbeta)
    full = lambda a: pl.BlockSpec(a.shape, lambda i, nd=a.ndim: (0,) * nd)
    return pl.pallas_call(
        functools.partial(_sample_merge_kernel, pos=pos),
        out_shape=jax.ShapeDtypeStruct(hs.shape, F32),
        grid=(1,),
        in_specs=[full(a) for a in args],
        out_specs=full(hs),
        compiler_params=pltpu.CompilerParams(
            dimension_semantics=("arbitrary",), vmem_limit_bytes=VMEM_LIMIT),
        name="sample_merge",
    )(*args)


def _rope_table(pos):
    pos = pos.astype(F32)

    def cs(half):
        inv = ROPE_THETA ** (-(jnp.arange(half, dtype=F32) / half))
        ang = pos[:, None] * inv[None, :]
        return jnp.cos(ang).T, jnp.sin(ang).T

    c8, s8 = cs(HEAD_DIM // 8)
    c4, s4 = cs(D_IDX // 8)
    return jnp.concatenate([c8, s8, c4, c4, -s4, s4], axis=0)


def _permute_w_in(w):
    d = w.shape[0]
    o = 0
    parts = {}
    for name, width in (("q", D_A), ("k", D_A), ("v", D_A), ("ga", D_A), ("qi", D_QI),
                        ("wi", N_IDX_HEADS), ("ki", D_IDX), ("ub", D_B), ("gb", D_B),
                        ("cv", D_C), ("cg", D_C), ("gc", D_C)):
        parts[name] = w[:, o:o + width]
        o += width
    pad = jnp.zeros((d, LANES - D_IDX - N_IDX_HEADS), w.dtype)
    order = ["q", "k", "v", "ga", "qi", "ub", "gb", "cv", "cg", "gc", "ki", "wi"]
    return jnp.concatenate([parts[n] for n in order] + [pad], axis=1).astype(BF16)


def _block_diag(pw):
    n_g, g, _ = pw.shape
    out = jnp.zeros((n_g * g, n_g * g), pw.dtype)
    for i in range(n_g):
        out = out.at[i * g:(i + 1) * g, i * g:(i + 1) * g].set(pw[i])
    return out.astype(BF16)


def kernel(x_prompt, x_sample, cache_k, cache_v, cache_kidx, state_pool, state_conv, page_table,
           meta_tokens, norm_g, w_in, q_norm_g, k_norm_g, pool_w, pool_scale, conv_w, conv_b,
           conv_norm_g, conv_norm_b, w_out):
    n_b, seq, d_model = x_prompt.shape
    n_q = x_sample.shape[0]
    depth = w_in.shape[0]
    n_pages = page_table.shape[1]
    past = n_pages * PAGE
    t_real = seq + N_META
    t_pad = -(-t_real // TM) * TM
    top_p = min(TOPK_MAX, t_real // 4)
    top_s = min(TOPK_MAX, (past + 1) // 4)

    meta = jnp.broadcast_to(meta_tokens[None].astype(F32), (n_b, N_META, d_model))
    tail = jnp.zeros((n_b, t_pad - t_real, d_model), F32)
    hp = jnp.concatenate([meta, x_prompt, tail], axis=1).reshape(n_b * t_pad, d_model)
    hs = x_sample.reshape(n_q, d_model)

    rope_p = _rope_table(jnp.arange(t_pad))
    rope_s = _rope_table(jnp.full((n_q,), past))
    cache_k_t = jnp.transpose(cache_k, (0, 1, 3, 4, 2))
    cache_v_t = jnp.transpose(cache_v, (0, 1, 3, 4, 2))
    cache_kidx_t = jnp.transpose(cache_kidx, (0, 1, 3, 2))

    outs = {name: [] for name in ("kp", "vp", "kip", "plp", "cvp", "ks", "vs", "kis", "pls", "cvs")}
    for l in range(depth):
        w_l = _permute_w_in(w_in[l])
        g_l = norm_g[l].reshape(1, d_model)
        qg = q_norm_g[l].reshape(HEAD_DIM, 1)
        kg = k_norm_g[l].reshape(HEAD_DIM, 1)
        wout = w_out[l].astype(BF16)
        poolw = _block_diag(pool_w[l])
        pscale = pool_scale[l].reshape(1, D_B)
        cw = conv_w[l]
        cb = conv_b[l].reshape(1, D_C)
        cg = conv_norm_g[l].reshape(1, D_C)
        cbeta = conv_norm_b[l].reshape(1, D_C)

        heads_last = lambda xT: jnp.transpose(xT.reshape(xT.shape[0], N_HEADS, HEAD_DIM, -1), (0, 3, 1, 2))

        q, kTf, kT, vTf, vb, aux, qi, wk, kiTf, kiT = _project(
            hp, n_b, t_real, TM, g_l, w_l, rope_p, qg, kg)
        q_s, kTf_s, _, vTf_s, _, aux_s, qi_s, wk_s, kiTf_s, _ = _project(
            hs, 1, n_q, n_q, g_l, w_l, rope_s, qg, kg)
        qi3 = qi_s.reshape(n_q, N_IDX_HEADS, D_IDX)
        wi3 = wk_s[:, WK_WI:WK_WI + N_IDX_HEADS].reshape(n_q, N_IDX_HEADS, 1)
        kin3 = wk_s[:, WK_KI:WK_KI + D_IDX].reshape(n_q, 1, D_IDX)
        mask = _sample_index(l, page_table, cache_kidx_t, qi3, wi3, kin3, top_s)
        q8 = q_s.astype(F32).reshape(n_q, N_HEADS, HEAD_DIM)
        k_new = heads_last(kTf_s).reshape(n_q, 1, N_HEADS, HEAD_DIM)
        v_new = heads_last(vTf_s).reshape(n_q, 1, N_HEADS, HEAD_DIM)

        att, att_s = _attention(
            l, page_table, q, qi, wk, kT, kiT, vb, cache_k_t, cache_v_t, q8, jnp.swapaxes(q8, 1, 2),
            k_new[:, 0], jnp.swapaxes(v_new[:, 0], 1, 2), mask.reshape(n_q, n_pages + 1, PAGE),
            n_b, t_pad, t_real, top_p)
        hp = _prompt_merge(att, aux, hp, wout, poolw, pscale, cw, cb, cg, cbeta, n_b, t_pad)
        aux3 = aux.reshape(n_b, t_pad, N_AUX)
        outs["kp"].append(heads_last(kTf))
        outs["vp"].append(heads_last(vTf))
        outs["kip"].append(jnp.swapaxes(kiTf, 1, 2))
        outs["plp"].append(aux3[:, t_real - POOL_STATE:t_real, A_UB:A_GB])
        outs["cvp"].append(aux3[:, t_real - CONV_STATE:t_real, A_UC:A_GC])

        sp_t = jnp.swapaxes(state_pool[l], 0, 1)
        sc_t = jnp.swapaxes(state_conv[l], 0, 1)
        hs = _sample_merge(att_s, aux_s, sp_t, sc_t, hs, wout, poolw, pscale, cw, cb, cg, cbeta, past)
        outs["ks"].append(k_new)
        outs["vs"].append(v_new)
        outs["kis"].append(jnp.swapaxes(kiTf_s, 1, 2).reshape(n_q, 1, D_IDX))
        outs["pls"].append(jnp.concatenate([state_pool[l][:, 1:], aux_s[:, None, A_UB:A_GB]], axis=1))
        outs["cvs"].append(jnp.concatenate([state_conv[l][:, 1:], aux_s[:, None, A_UC:A_GC]], axis=1))

    y_prompt = hp.reshape(n_b, t_pad, d_model)[:, N_META:t_real]
    y_sample = hs.reshape(n_q, 1, d_model)
    st = lambda name: jnp.stack(outs[name], axis=0)
    return (y_prompt, y_sample, st("kp"), st("vp"), st("kip"), st("plp"), st("cvp"),
            st("ks"), st("vs"), st("kis"), st("pls"), st("cvs"))
```

```python
import functools

import jax
import jax.numpy as jnp
from jax import lax
from jax.experimental import pallas as pl
from jax.experimental.pallas import tpu as pltpu

F32 = jnp.float32
BF16 = jnp.bfloat16

LANES = 128
SUBLANES = 8
N_META = 16
HEAD_DIM = 64
N_HEADS = 8
D_A = N_HEADS * HEAD_DIM
N_IDX_HEADS = 8
D_IDX = 32
D_QI = N_IDX_HEADS * D_IDX
D_B = 256
D_C = 256
POOL_WINDOWS = (2, 4, 8, 16)
POOL_GROUP = D_B // len(POOL_WINDOWS)
POOL_STATE = 15
CONV_WIDTH = 31
CONV_STATE = CONV_WIDTH - 1
HALO = 32
TOPK_MAX = 256
ROPE_THETA = 500000.0
EPS = 1e-6
PAGE = 128
LOG2E = 1.4426950408889634
Q_SCALE = HEAD_DIM ** -0.5 * LOG2E
IDX_SCALE = (D_IDX ** -0.5) * (N_IDX_HEADS ** -0.5)

QB = 128
CK = 384
TM = 384
MAX_BISECT = 64
BIG_INDEX = 1.0e9
MASKED = -1.0e30
TINY = 1.1754944e-38
ACC_ROWS = 64
ABOVE_MAX_STEP = 2.0 ** -20
PASSES_PER_TEST = 4
VMEM_LIMIT = 56 * 1024 * 1024

C_Q, C_K, C_V, C_GA, C_QI = 0, 512, 1024, 1536, 2048
C_UB, C_GB, C_CV, C_CG, C_GC, C_KI = 2304, 2560, 2816, 3072, 3328, 3584
N_PERM = C_KI + LANES
A_GA, A_UB, A_GB, A_UC, A_GC, N_AUX = 0, 512, 768, 1024, 1280, 1536
WK_KI, WK_WI = 0, D_IDX


def _sigmoid(x):
    return 0.5 * jnp.tanh(0.5 * x) + 0.5


def _silu(x):
    return x * _sigmoid(x)


def _proj_kernel(x_ref, g_ref, w_ref, rope_ref, qg_ref, kg_ref,
                 q_ref, kTf_ref, kT_ref, vTf_ref, vb_ref, aux_ref, qi_ref, wk_ref, kiTf_ref, kiT_ref,
                 t_scr):
    x = x_ref[...]
    ms = jnp.mean(x * x, axis=-1, keepdims=True)
    h = (x * lax.rsqrt(ms + EPS) * g_ref[...]).astype(BF16)

    def proj(c0, c1):
        return jnp.dot(h, w_ref[:, c0:c1], preferred_element_type=F32)

    cos8 = rope_ref[0:8, :]
    sin8 = rope_ref[8:16, :]
    cc4 = rope_ref[16:24, :]
    ss4 = rope_ref[24:32, :]

    zT = proj(C_Q, C_V).T
    for j in range(2 * N_HEADS):
        xh = zT[j * HEAD_DIM:(j + 1) * HEAD_DIM, :]
        g = qg_ref[...] if j < N_HEADS else kg_ref[...]
        xn = xh * lax.rsqrt(jnp.mean(xh * xh, axis=0, keepdims=True) + EPS) * g
        x1 = xn[0:8, :]
        x2 = xn[8:16, :]
        r0 = j * HEAD_DIM
        t_scr[r0:r0 + 8, :] = x1 * cos8 - x2 * sin8
        t_scr[r0 + 8:r0 + 16, :] = x2 * cos8 + x1 * sin8
        t_scr[r0 + 16:r0 + HEAD_DIM, :] = xn[16:HEAD_DIM, :]
    q_ref[...] = (t_scr[0:D_A, :].T * Q_SCALE).astype(BF16)
    kT = t_scr[D_A:2 * D_A, :]
    kTf_ref[...] = kT
    kT_ref[...] = kT.astype(BF16)

    zv = proj(C_V, C_GA)
    vTf_ref[...] = zv.T
    vb_ref[...] = zv.astype(BF16)

    aux_ref[:, A_GA:A_UB] = proj(C_GA, C_QI)
    aux_ref[:, A_UB:A_UC] = proj(C_UB, C_CV)
    zc = proj(C_CV, C_KI)
    aux_ref[:, A_UC:A_GC] = zc[:, 0:D_C] * _sigmoid(zc[:, D_C:2 * D_C])
    aux_ref[:, A_GC:N_AUX] = zc[:, 2 * D_C:3 * D_C]

    ziT = proj(C_QI, C_UB).T
    parts = []
    for hh in range(N_IDX_HEADS):
        r0 = hh * D_IDX
        x8 = ziT[r0:r0 + 8, :]
        parts.append(x8 * cc4 + pltpu.roll(x8, 4, 0) * ss4)
        parts.append(ziT[r0 + 8:r0 + D_IDX, :])
    qi_ref[...] = jnp.concatenate(parts, axis=0).T.astype(BF16)

    zwT = proj(C_KI, N_PERM).T
    x8 = zwT[0:8, :]
    ki = jnp.concatenate([x8 * cc4 + pltpu.roll(x8, 4, 0) * ss4, zwT[8:D_IDX, :]], axis=0)
    wi = zwT[D_IDX:D_IDX + N_IDX_HEADS, :] * IDX_SCALE
    wk_ref[...] = jnp.concatenate([ki, wi, zwT[D_IDX + N_IDX_HEADS:, :]], axis=0).T
    kiTf_ref[...] = ki
    kiT_ref[...] = jnp.concatenate([ki] * (LANES // D_IDX), axis=0).astype(BF16)


def _project(x2d, n_batch, t_real, tm, g, w, rope, qg, kg):
    rows = x2d.shape[0]
    t = rows // n_batch
    nt = t // tm
    d = x2d.shape[1]
    row_map = lambda b, i: (b * nt + i, 0)
    col_map = lambda b, i: (b, 0, i)
    const = lambda b, i: (0, 0)
    out_shape = (
        jax.ShapeDtypeStruct((rows, D_A), BF16),
        jax.ShapeDtypeStruct((n_batch, D_A, t_real), F32),
        jax.ShapeDtypeStruct((n_batch, D_A, t), BF16),
        jax.ShapeDtypeStruct((n_batch, D_A, t_real), F32),
        jax.ShapeDtypeStruct((rows, D_A), BF16),
        jax.ShapeDtypeStruct((rows, N_AUX), F32),
        jax.ShapeDtypeStruct((rows, D_QI), BF16),
        jax.ShapeDtypeStruct((rows, LANES), F32),
        jax.ShapeDtypeStruct((n_batch, D_IDX, t_real), F32),
        jax.ShapeDtypeStruct((n_batch, LANES, t), BF16),
    )
    out_specs = (
        pl.BlockSpec((tm, D_A), row_map),
        pl.BlockSpec((None, D_A, tm), col_map),
        pl.BlockSpec((None, D_A, tm), col_map),
        pl.BlockSpec((None, D_A, tm), col_map),
        pl.BlockSpec((tm, D_A), row_map),
        pl.BlockSpec((tm, N_AUX), row_map),
        pl.BlockSpec((tm, D_QI), row_map),
        pl.BlockSpec((tm, LANES), row_map),
        pl.BlockSpec((None, D_IDX, tm), col_map),
        pl.BlockSpec((None, LANES, tm), col_map),
    )
    in_specs = [
        pl.BlockSpec((tm, d), row_map),
        pl.BlockSpec((1, d), const),
        pl.BlockSpec((d, N_PERM), const),
        pl.BlockSpec((32, tm), lambda b, i: (0, i)),
        pl.BlockSpec((HEAD_DIM, 1), const),
        pl.BlockSpec((HEAD_DIM, 1), const),
    ]
    return pl.pallas_call(
        _proj_kernel,
        out_shape=out_shape,
        grid=(n_batch, nt),
        in_specs=in_specs,
        out_specs=out_specs,
        scratch_shapes=[pltpu.VMEM((2 * D_A, tm), F32)],
        compiler_params=pltpu.CompilerParams(
            dimension_semantics=("parallel", "parallel"), vmem_limit_bytes=VMEM_LIMIT),
        name="proj",
    )(x2d, g, w, rope, qg, kg)


def _fold_t(it_scr, n_blk, ck, fn, inits):
    per = ck // LANES

    def body(c, accs):
        row = pl.multiple_of(c * ck, LANES)
        return fn(accs, it_scr[pl.ds(row, ck), :], row)

    def tail(c, accs):
        row = pl.multiple_of(c * LANES, LANES)
        return fn(accs, it_scr[pl.ds(row, LANES), :], row)

    accs = tuple(jnp.full((ACC_ROWS, LANES), v, F32) for v in inits)
    accs = lax.fori_loop(0, n_blk // per, body, accs)
    return lax.fori_loop((n_blk // per) * per, n_blk, tail, accs)


def _map_t(it_scr, n_blk, ck, fn):
    per = ck // LANES

    def body(c, carry):
        row = pl.multiple_of(c * ck, LANES)
        it_scr[pl.ds(row, ck), :] = fn(it_scr[pl.ds(row, ck), :], row)
        return carry

    def tail(c, carry):
        row = pl.multiple_of(c * LANES, LANES)
        it_scr[pl.ds(row, LANES), :] = fn(it_scr[pl.ds(row, LANES), :], row)
        return carry

    lax.fori_loop(0, n_blk // per, body, 0)
    lax.fori_loop((n_blk // per) * per, n_blk, tail, 0)


def _vreg_fold(x, op):
    return op(x.reshape(x.shape[0] // ACC_ROWS, ACC_ROWS, LANES), axis=0)


def _count_ge_t(it_scr, n_blk, ck, mid):
    (acc,) = _fold_t(
        it_scr, n_blk, ck,
        lambda a, blk, row: (a[0] + _vreg_fold(jnp.where(blk >= mid, 1.0, 0.0), jnp.sum),), (0.0,))
    return jnp.sum(acc, axis=0, keepdims=True)


def _select_bounds_t(it_scr, n_blk, ck, top_k, trivial):
    kf = float(top_k)
    ninf = -jnp.inf
    pinf = jnp.inf
    land, lor, lnot = jnp.logical_and, jnp.logical_or, jnp.logical_not
    passes = PASSES_PER_TEST

    mn, mx = _fold_t(
        it_scr, n_blk, ck,
        lambda a, blk, row: (jnp.minimum(a[0], _vreg_fold(jnp.where(blk == ninf, pinf, blk), jnp.min)),
                             jnp.maximum(a[1], _vreg_fold(blk, jnp.max))),
        (pinf, ninf))
    cpos, cnn = _fold_t(
        it_scr, n_blk, ck,
        lambda a, blk, row: (a[0] + _vreg_fold(jnp.where(blk > 0.0, 1.0, 0.0), jnp.sum),
                             a[1] + _vreg_fold(jnp.where(blk >= 0.0, 1.0, 0.0), jnp.sum)),
        (0.0, 0.0))
    col_min = jnp.min(mn, axis=0, keepdims=True)
    col_max = jnp.max(mx, axis=0, keepdims=True)
    c_pos = jnp.sum(cpos, axis=0, keepdims=True)
    c_nn = jnp.sum(cnn, axis=0, keepdims=True)
    above_max = col_max + jnp.abs(col_max) * ABOVE_MAX_STEP + TINY

    zero_tie = land(land(c_pos < kf, c_nn >= kf), lnot(trivial))
    neg_side = c_nn < kf
    lo0 = jnp.where(zero_tie, 0.0, jnp.where(neg_side, col_min, TINY))
    lo0 = jnp.where(trivial, col_min, lo0)
    hi0 = jnp.where(trivial, pinf,
                    jnp.where(zero_tie, jnp.where(c_nn == kf, pinf, TINY),
                              jnp.where(neg_side, 0.0, above_max)))
    pos_exact = land(lnot(neg_side), c_pos == kf)
    hi0 = jnp.where(land(pos_exact, lnot(lor(trivial, zero_tie))), pinf, hi0)
    done0 = lor(trivial, lor(land(zero_tie, c_nn == kf), pos_exact))
    res0 = jnp.where(done0, 1.0, 0.0)
    act0 = jnp.where(lor(done0, zero_tie), 0.0, 1.0)

    def bis_cond(st):
        return jnp.logical_and(st[0] < MAX_BISECT, st[1] > 0.0)

    def bis_body(st):
        it, _, lo, hi, res, act = st
        for _ in range(passes):
            mid = 0.5 * lo + 0.5 * hi
            ok = land(act > 0.0, land(mid > lo, mid < hi))
            c = _count_ge_t(it_scr, n_blk, ck, mid)
            eq = land(ok, c == kf)
            gt = land(ok, c > kf)
            lt = land(ok, c < kf)
            lo = jnp.where(lor(eq, gt), mid, lo)
            hi = jnp.where(eq, pinf, jnp.where(lt, mid, hi))
            res = jnp.where(eq, 1.0, res)
            act = jnp.where(lor(eq, lnot(ok)), 0.0, act)
        return it + passes, jnp.sum(act), lo, hi, res, act

    _, _, lo, hi, res, _ = lax.while_loop(
        bis_cond, bis_body, (jnp.int32(0), jnp.sum(act0), lo0, hi0, res0, act0))

    def cut_band():
        need = kf - _count_ge_t(it_scr, n_blk, ck, hi)

        def to_index(blk, row):
            idx = (lax.broadcasted_iota(jnp.int32, blk.shape, 0) + row).astype(F32)
            return jnp.where(land(blk >= lo, blk < hi), idx, BIG_INDEX)

        _map_t(it_scr, n_blk, ck, to_index)

        def count_band(j):
            (acc,) = _fold_t(
                it_scr, n_blk, ck,
                lambda a, blk, row: (a[0] + _vreg_fold(jnp.where(blk <= j, 1.0, 0.0), jnp.sum),), (0.0,))
            return jnp.sum(acc, axis=0, keepdims=True)

        max_steps = int(it_scr.shape[0]).bit_length() + 1

        def j_cond(st):
            return jnp.logical_and(st[0] < max_steps, st[1] > 0.0)

        def j_body(st):
            it, _, jl, jh, jc, open_ = st
            jm = jnp.floor((jl + jh + 1.0) * 0.5)
            c = count_band(jm)
            is_open = open_ > 0.0
            eq = land(is_open, c == need)
            jl = jnp.where(land(is_open, c < need), jm, jl)
            jh = jnp.where(land(is_open, c > need), jm, jh)
            jc = jnp.where(eq, jm, jc)
            open_ = jnp.where(eq, 0.0, open_)
            return it + 1, jnp.sum(open_), jl, jh, jc, open_

        open0 = 1.0 - res
        full = lambda v: jnp.full((1, LANES), v, F32)
        st0 = (jnp.int32(0), jnp.sum(open0), full(-1.0), full(float(it_scr.shape[0] - 1)),
               full(BIG_INDEX), open0)
        return lax.while_loop(j_cond, j_body, st0)[4]

    jcut = lax.cond(jnp.sum(1.0 - res) > 0.0, cut_band, lambda: jnp.full((1, LANES), BIG_INDEX, F32))
    return lo, hi, jcut


def _chunked(n_blk, body):
    per = CK // LANES
    n_full = n_blk // per
    rem = n_blk - n_full * per
    base = (n_full // 3) * 3

    def run(c0, count):
        for u in range(count):
            body(pl.multiple_of((c0 + u) * CK, LANES), CK)

    def triple(j, carry):
        run(3 * j, 3)
        return carry

    lax.fori_loop(0, n_full // 3, triple, 0)
    for count in (2, 1):
        pl.when(n_full - base == count)(functools.partial(run, base, count))
    for r in range(1, per):
        pl.when(rem == r)(functools.partial(body, pl.multiple_of(n_full * CK, LANES), r * LANES))


def _as_column(row):
    return jnp.broadcast_to(row, (LANES, LANES)).T


def _selected(blk, col, lo, hi, jcut):
    lane = lax.broadcasted_iota(jnp.int32, blk.shape, 1)
    idx = (lane + col).astype(F32)
    return jnp.logical_or(blk >= hi, jnp.logical_and(blk >= lo, idx <= jcut))


def _attn_kernel(pt_ref, q_ref, qi_ref, wk_ref, kT_ref, kiT_ref, v_ref, *rest, t_real, top_k, n_pages):
    del pt_ref
    k_pages = rest[:n_pages]
    v_pages = rest[n_pages:2 * n_pages]
    (q8_ref, qT_ref, kn_ref, vnT_ref, mask_ref, o_ref, os_ref,
     i_scr, it_scr, s_scr, m_scr, l_scr, acc_scr, oT_scr) = rest[2 * n_pages:]
    i = pl.program_id(1)
    q_pos = i * QB + lax.broadcasted_iota(jnp.int32, (QB, 1), 0)
    lane = lax.broadcasted_iota(jnp.int32, (QB, LANES), 1)

    qi = qi_ref[...]
    q = q_ref[...]
    qiz, qz = [], []
    per = LANES // D_IDX
    for h in range(N_IDX_HEADS):
        blk = qi[:, (h // per) * LANES:(h // per + 1) * LANES]
        qiz.append(jnp.where(lane // D_IDX == h % per, blk, jnp.zeros_like(blk)))
    for h in range(N_HEADS):
        blk = q[:, (h // 2) * LANES:(h // 2 + 1) * LANES]
        qz.append(jnp.where(lane // HEAD_DIM == h % 2, blk, jnp.zeros_like(blk)))
    w = wk_ref[:, WK_WI:WK_WI + N_IDX_HEADS]

    def score_chunk(col, width):
        kc = kiT_ref[:, pl.ds(col, width)]
        acc = jnp.zeros((QB, width), F32)
        for h in range(N_IDX_HEADS):
            d = jnp.dot(qiz[h], kc, preferred_element_type=F32)
            acc = acc + w[:, h:h + 1] * jnp.maximum(d, 0.0)
        key_pos = col + lax.broadcasted_iota(jnp.int32, (QB, width), 1)
        sc = jnp.where(key_pos <= q_pos, acc, -jnp.inf)
        i_scr[:, pl.ds(col, width)] = sc
        for k in range(width // LANES):
            row = pl.multiple_of(col + k * LANES, LANES)
            it_scr[pl.ds(row, LANES), :] = sc[:, k * LANES:(k + 1) * LANES].T

    _chunked(i + 1, score_chunk)

    q_row = i * QB + lax.broadcasted_iota(jnp.int32, (1, LANES), 1)
    trivial = jnp.logical_or(q_row + 1 <= top_k, q_row >= t_real)
    lo, hi, jcut = (_as_column(r)[:, 0:1] for r in _select_bounds_t(it_scr, i + 1, CK, top_k, trivial))

    m_scr[...] = jnp.full(m_scr.shape, MASKED, F32)

    def logits_chunk(col, width):
        sel = _selected(i_scr[:, pl.ds(col, width)], col, lo, hi, jcut)
        bias = jnp.where(sel, 0.0, MASKED)
        for h in range(N_HEADS):
            pr = h // 2
            kc = kT_ref[pr * LANES:(pr + 1) * LANES, pl.ds(col, width)]
            s = jnp.dot(qz[h], kc, preferred_element_type=F32) + bias
            s_scr[h, :, pl.ds(col, width)] = s
            mx = s[:, 0:LANES]
            for k in range(1, width // LANES):
                mx = jnp.maximum(mx, s[:, k * LANES:(k + 1) * LANES])
            m_scr[h] = jnp.maximum(m_scr[h], mx)

    _chunked(i + 1, logits_chunk)

    for h in range(N_HEADS):
        m_scr[h] = jnp.broadcast_to(jnp.max(m_scr[h], axis=1, keepdims=True), (QB, LANES))
    l_scr[...] = jnp.zeros(l_scr.shape, F32)
    acc_scr[...] = jnp.zeros(acc_scr.shape, F32)

    def value_chunk(col, width):
        for h in range(N_HEADS):
            pr = h // 2
            m = m_scr[h]
            ps = [jnp.exp2(s_scr[h, :, pl.ds(pl.multiple_of(col + k * LANES, LANES), LANES)] - m)
                  for k in range(width // LANES)]
            lsum = ps[0]
            for pk in ps[1:]:
                lsum = lsum + pk
            l_scr[h] = l_scr[h] + lsum
            p = (ps[0] if len(ps) == 1 else jnp.concatenate(ps, axis=1)).astype(BF16)
            vc = v_ref[pl.ds(col, width), pr * LANES:(pr + 1) * LANES]
            acc_scr[h] = acc_scr[h] + jnp.dot(p, vc, preferred_element_type=F32)

    _chunked(i + 1, value_chunk)

    for pr in range(N_HEADS // 2):
        first = lane < HEAD_DIM
        num = jnp.where(first, acc_scr[2 * pr], acc_scr[2 * pr + 1])
        den = jnp.where(first, jnp.sum(l_scr[2 * pr], axis=1, keepdims=True),
                        jnp.sum(l_scr[2 * pr + 1], axis=1, keepdims=True))
        o_ref[:, pr * LANES:(pr + 1) * LANES] = num / den

    step = pl.program_id(0) * pl.num_programs(1) + i
    n_steps = pl.num_programs(0) * pl.num_programs(1)

    @pl.when(step == 0)
    def _():
        oT_scr[...] = jnp.zeros(oT_scr.shape, F32)

    @pl.when(step < os_ref.shape[0])
    def _():
        _sample_attend(step, k_pages, v_pages, q8_ref, qT_ref, kn_ref, vnT_ref, mask_ref, oT_scr)

    @pl.when(step == n_steps - 1)
    def _():
        os_ref[...] = oT_scr[...].T


def _attention(layer, page_table, q, qi, wk, kT, kiT, vb, cache_k_t, cache_v_t, q8, qT, kn, vnT, mask3,
               n_batch, t_pad, t_real, top_k):
    nq = t_pad // QB
    n_q, n_pages = page_table.shape
    assert n_q == LANES, "one output lane per sample query"
    assert n_batch * nq >= n_q, "one sample query per grid step"
    kern = functools.partial(_attn_kernel, t_real=t_real, top_k=top_k, n_pages=n_pages)
    row_map = lambda b, i, pt: (b * nq + i, 0)
    query = lambda b, i: jnp.minimum(b * nq + i, n_q - 1)
    per_q = lambda b, i, pt: (query(b, i), 0, 0)
    once = pl.Buffered(1)

    def page_spec(p):
        return pl.BlockSpec((None, None, N_HEADS, HEAD_DIM, PAGE),
                            lambda b, i, pt: (layer, pt[query(b, i), p], 0, 0, 0))

    in_specs = [
        pl.BlockSpec((QB, D_A), row_map),
        pl.BlockSpec((QB, D_QI), row_map),
        pl.BlockSpec((QB, LANES), row_map),
        pl.BlockSpec((None, D_A, t_pad), lambda b, i, pt: (b, 0, 0), pipeline_mode=once),
        pl.BlockSpec((None, LANES, t_pad), lambda b, i, pt: (b, 0, 0), pipeline_mode=once),
        pl.BlockSpec((t_pad, D_A), lambda b, i, pt: (b, 0), pipeline_mode=once),
    ]
    in_specs += [page_spec(p) for p in range(n_pages)] + [page_spec(p) for p in range(n_pages)]
    in_specs += [
        pl.BlockSpec((None, N_HEADS, HEAD_DIM), per_q),
        pl.BlockSpec((None, HEAD_DIM, N_HEADS), per_q),
        pl.BlockSpec((None, N_HEADS, HEAD_DIM), per_q),
        pl.BlockSpec((None, HEAD_DIM, N_HEADS), per_q),
        pl.BlockSpec((None, n_pages + 1, PAGE), per_q),
    ]
    return pl.pallas_call(
        kern,
        out_shape=(jax.ShapeDtypeStruct((n_batch * t_pad, D_A), F32),
                   jax.ShapeDtypeStruct((n_q, D_A), F32)),
        grid_spec=pltpu.PrefetchScalarGridSpec(
            num_scalar_prefetch=1,
            grid=(n_batch, nq),
            in_specs=in_specs,
            out_specs=(pl.BlockSpec((QB, D_A), row_map),
                       pl.BlockSpec((n_q, D_A), lambda b, i, pt: (0, 0))),
            scratch_shapes=[
                pltpu.VMEM((QB, t_pad), F32),
                pltpu.VMEM((t_pad, QB), F32),
                pltpu.VMEM((N_HEADS, QB, t_pad), F32),
                pltpu.VMEM((N_HEADS, QB, LANES), F32),
                pltpu.VMEM((N_HEADS, QB, LANES), F32),
                pltpu.VMEM((N_HEADS, QB, LANES), F32),
                pltpu.VMEM((D_A, n_q), F32),
            ],
        ),
        compiler_params=pltpu.CompilerParams(
            dimension_semantics=("arbitrary", "arbitrary"), vmem_limit_bytes=VMEM_LIMIT),
        name="attention",
    )(page_table, q, qi, wk, kT, kiT, vb, *([cache_k_t] * n_pages), *([cache_v_t] * n_pages),
      q8, qT, kn, vnT, mask3)


def _gate_project(att, aux_ref, pool, conv, wout_ref, resid):
    ya = att * _silu(aux_ref[:, A_GA:A_UB])
    yb = pool * _silu(aux_ref[:, A_GB:A_UC])
    yc = conv * _silu(aux_ref[:, A_GC:N_AUX])
    y = jnp.concatenate([ya, yb, yc], axis=1).astype(BF16)
    return resid + jnp.dot(y, wout_ref[...], preferred_element_type=F32)


def _pool_mix(sums, u, cnts, poolw_ref, pscale_ref):
    lane = lax.broadcasted_iota(jnp.int32, u.shape, 1)
    d = sums[-1] / cnts[-1]
    for g in range(len(POOL_WINDOWS) - 2, -1, -1):
        d = jnp.where(lane < (g + 1) * POOL_GROUP, sums[g] / cnts[g], d)
    d = (d - u).astype(BF16)
    return jnp.dot(d, poolw_ref[...], preferred_element_type=F32) * pscale_ref[...]


def _conv_norm(y, cb_ref, cg_ref, cbeta_ref):
    y = y + cb_ref[...]
    mu = jnp.mean(y, axis=-1, keepdims=True)
    yc = y - mu
    var = jnp.mean(yc * yc, axis=-1, keepdims=True)
    return _silu(yc * lax.rsqrt(var + EPS) * cg_ref[...] + cbeta_ref[...])


def _row_windows(ext_scr, rot_scr, tm):
    n = rot_scr.shape[1]
    for r in range(1, SUBLANES):
        rot_scr[r - 1] = ext_scr[r:r + n, :]

    def window(off):
        r = off % SUBLANES
        a = off - r
        return ext_scr[a:a + tm, :] if r == 0 else rot_scr[r - 1, a:a + tm, :]

    return window


def _merge_kernel(att_ref, aux_ref, hub_ref, huc_ref, hp_ref, wout_ref, poolw_ref, pscale_ref,
                  cw_ref, cb_ref, cg_ref, cbeta_ref, o_ref, ub_scr, uc_scr, ub_rot, uc_rot):
    i = pl.program_id(1)
    tm = att_ref.shape[0]
    first = i == 0
    ub = aux_ref[:, A_UB:A_GB]
    uc = aux_ref[:, A_UC:A_GC]
    ub_scr[0:HALO, :] = jnp.where(first, 0.0, hub_ref[...])
    uc_scr[0:HALO, :] = jnp.where(first, 0.0, huc_ref[...])
    ub_scr[HALO:HALO + tm, :] = ub
    uc_scr[HALO:HALO + tm, :] = uc
    ub_win = _row_windows(ub_scr, ub_rot, tm)
    uc_win = _row_windows(uc_scr, uc_rot, tm)

    pos = i * tm + lax.broadcasted_iota(jnp.int32, (tm, 1), 0)
    run = ub
    sums, cnts = [], []
    for j in range(1, POOL_WINDOWS[-1]):
        run = run + ub_win(HALO - j)
        if j + 1 in POOL_WINDOWS:
            sums.append(run)
            cnts.append(jnp.minimum(pos + 1, j + 1).astype(F32))
    pool = _pool_mix(sums, ub, cnts, poolw_ref, pscale_ref)

    y = jnp.zeros((tm, D_C), F32)
    for j in range(CONV_WIDTH):
        y = y + cw_ref[j:j + 1, :] * uc_win(HALO - CONV_STATE + j)
    conv = _conv_norm(y, cb_ref, cg_ref, cbeta_ref)

    o_ref[...] = _gate_project(att_ref[...], aux_ref, pool, conv, wout_ref, hp_ref[...])


def _prompt_merge(att, aux, hp, wout, poolw, pscale, cw, cb, cg, cbeta, n_batch, t_pad):
    nt = t_pad // TM
    hb = TM // HALO
    row_map = lambda b, i: (b * nt + i, 0)
    const = lambda b, i: (0, 0)
    d = hp.shape[1]

    def halo_map(col_block):
        return lambda b, i: (jnp.maximum((b * nt + i) * hb - 1, 0), col_block)

    return pl.pallas_call(
        _merge_kernel,
        out_shape=jax.ShapeDtypeStruct(hp.shape, F32),
        grid=(n_batch, nt),
        in_specs=[
            pl.BlockSpec((TM, D_A), row_map),
            pl.BlockSpec((TM, N_AUX), row_map),
            pl.BlockSpec((HALO, D_B), halo_map(A_UB // D_B)),
            pl.BlockSpec((HALO, D_C), halo_map(A_UC // D_C)),
            pl.BlockSpec((TM, d), row_map),
            pl.BlockSpec(wout.shape, const),
            pl.BlockSpec(poolw.shape, const),
            pl.BlockSpec((1, D_B), const),
            pl.BlockSpec(cw.shape, const),
            pl.BlockSpec((1, D_C), const),
            pl.BlockSpec((1, D_C), const),
            pl.BlockSpec((1, D_C), const),
        ],
        out_specs=pl.BlockSpec((TM, d), row_map),
        scratch_shapes=[pltpu.VMEM((HALO + TM, D_B), F32), pltpu.VMEM((HALO + TM, D_C), F32),
                        pltpu.VMEM((SUBLANES - 1, HALO + TM - SUBLANES, D_B), F32),
                        pltpu.VMEM((SUBLANES - 1, HALO + TM - SUBLANES, D_C), F32)],
        compiler_params=pltpu.CompilerParams(
            dimension_semantics=("parallel", "parallel"), vmem_limit_bytes=VMEM_LIMIT),
        name="prompt_merge",
    )(att, aux, aux, aux, hp, wout, poolw, pscale, cw, cb, cg, cbeta)


def _sample_index_kernel(pt_ref, *refs, n_pages, top_k):
    del pt_ref
    ki_pages = refs[:SUBLANES * n_pages]
    qi_ref, wi_ref, kin_ref, mask_ref, i_scr, it_scr = refs[SUBLANES * n_pages:]
    g = pl.program_id(0)
    past = n_pages * PAGE
    width = i_scr.shape[1]
    sub = lax.broadcasted_iota(jnp.int32, (SUBLANES, width), 0)
    lane1 = lax.broadcasted_iota(jnp.int32, (1, LANES), 1)

    rows = jnp.full((SUBLANES, width), -jnp.inf, F32)
    for j in range(SUBLANES):
        qi = qi_ref[j]
        w = wi_ref[j]
        pages = ki_pages[j * n_pages:(j + 1) * n_pages]
        kp = jnp.concatenate([pg[...] for pg in pages], axis=1).astype(BF16)
        d = jnp.dot(qi, kp, preferred_element_type=F32)
        sc = jnp.sum(w * jnp.maximum(d, 0.0), axis=0, keepdims=True)
        dn = jnp.sum(qi.astype(F32) * kin_ref[j], axis=1, keepdims=True)
        sn = jnp.sum(w * jnp.maximum(dn, 0.0), axis=0, keepdims=True)
        row = jnp.concatenate([sc, jnp.where(lane1 == 0, sn, -jnp.inf)], axis=1)
        rows = jnp.where(sub == j, row, rows)
    i_scr[pl.ds(pl.multiple_of(g * SUBLANES, SUBLANES), SUBLANES), :] = rows

    @pl.when(g == pl.num_programs(0) - 1)
    def _():
        width = i_scr.shape[1]
        for k in range(width // LANES):
            it_scr[k * LANES:(k + 1) * LANES, :] = i_scr[:, k * LANES:(k + 1) * LANES].T
        trivial = jnp.full((1, LANES), past + 1 <= top_k)
        lo, hi, jcut = (_as_column(r)[:, 0:1] for r in _select_bounds_t(it_scr, width // LANES, width, top_k, trivial))
        for k in range(width // LANES):
            sel = _selected(i_scr[:, k * LANES:(k + 1) * LANES], k * LANES, lo, hi, jcut)
            mask_ref[:, k * LANES:(k + 1) * LANES] = jnp.where(sel, 1.0, 0.0)


def _sample_index(layer, page_table, cache_kidx_t, qi3, wi3, kin3, top_k):
    n_q, n_pages = page_table.shape
    width = (n_pages + 1) * PAGE
    kern = functools.partial(_sample_index_kernel, n_pages=n_pages, top_k=top_k)

    def page_map(j, p):
        return lambda g, pt: (layer, pt[g * SUBLANES + j, p], 0, 0)

    group = lambda g, pt: (g, 0, 0)
    in_specs = [pl.BlockSpec((None, None, D_IDX, PAGE), page_map(j, p))
                for j in range(SUBLANES) for p in range(n_pages)]
    in_specs += [
        pl.BlockSpec((SUBLANES, N_IDX_HEADS, D_IDX), group),
        pl.BlockSpec((SUBLANES, N_IDX_HEADS, 1), group),
        pl.BlockSpec((SUBLANES, 1, D_IDX), group),
    ]
    return pl.pallas_call(
        kern,
        out_shape=jax.ShapeDtypeStruct((n_q, width), F32),
        grid_spec=pltpu.PrefetchScalarGridSpec(
            num_scalar_prefetch=1,
            grid=(n_q // SUBLANES,),
            in_specs=in_specs,
            out_specs=pl.BlockSpec((n_q, width), lambda b, pt: (0, 0)),
            scratch_shapes=[pltpu.VMEM((n_q, width), F32), pltpu.VMEM((width, n_q), F32)],
        ),
        compiler_params=pltpu.CompilerParams(
            dimension_semantics=("arbitrary",), vmem_limit_bytes=VMEM_LIMIT),
        name="sample_index",
    )(page_table, *([cache_kidx_t] * (SUBLANES * n_pages)), qi3, wi3, kin3)


def _sample_attend(b, k_pages, v_pages, q8_ref, qT_ref, kn_ref, vnT_ref, mask_ref, oT_scr):
    n_pages = len(k_pages)
    lane = lax.broadcasted_iota(jnp.int32, (HEAD_DIM, LANES), 1)
    qT = qT_ref[...]
    vnT = vnT_ref[...]
    s_new = jnp.sum(q8_ref[...] * kn_ref[...], axis=1, keepdims=True)
    hit_new = mask_ref[n_pages:n_pages + 1, 0:1] > 0.5
    hits = [mask_ref[p:p + 1, :] > 0.5 for p in range(n_pages)]

    for h in range(N_HEADS):
        qcol = jnp.broadcast_to(qT[:, h:h + 1], (HEAD_DIM, LANES))
        s = [jnp.where(hits[p], jnp.sum(k_pages[p][h] * qcol, axis=0, keepdims=True), MASKED)
             for p in range(n_pages)]
        sn = jnp.where(hit_new, s_new[h:h + 1, :], MASKED)
        mx = s[0]
        for p in range(1, n_pages):
            mx = jnp.maximum(mx, s[p])
        m = jnp.maximum(jnp.max(mx, axis=1, keepdims=True), sn)
        pn = jnp.exp2(sn - m)
        acc = jnp.zeros((HEAD_DIM, LANES), F32)
        lsum = jnp.zeros((1, LANES), F32)
        for p in range(n_pages):
            pp = jnp.exp2(s[p] - m)
            lsum = lsum + pp
            acc = acc + v_pages[p][h] * pp
        l = jnp.sum(lsum, axis=1, keepdims=True) + pn
        o_h = (jnp.sum(acc, axis=1, keepdims=True) + vnT[:, h:h + 1] * pn) / l
        rows = slice(h * HEAD_DIM, (h + 1) * HEAD_DIM)
        oT_scr[rows, :] = jnp.where(lane == b, o_h, oT_scr[rows, :])


def _sample_merge_kernel(att_ref, aux_ref, sp_ref, sc_ref, hs_ref, wout_ref, poolw_ref, pscale_ref,
                         cw_ref, cb_ref, cg_ref, cbeta_ref, o_ref, *, pos):
    ub = aux_ref[:, A_UB:A_GB]
    uc = aux_ref[:, A_UC:A_GC]
    run = ub
    sums, cnts = [], []
    for j in range(1, POOL_WINDOWS[-1]):
        run = run + sp_ref[POOL_STATE - j]
        if j + 1 in POOL_WINDOWS:
            sums.append(run)
            cnts.append(float(min(pos + 1, j + 1)))
    pool = _pool_mix(sums, ub, cnts, poolw_ref, pscale_ref)

    y = cw_ref[CONV_STATE:CONV_WIDTH, :] * uc
    for j in range(CONV_STATE):
        y = y + cw_ref[j:j + 1, :] * sc_ref[j]
    conv = _conv_norm(y, cb_ref, cg_ref, cbeta_ref)
    o_ref[...] = _gate_project(att_ref[...], aux_ref, pool, conv, wout_ref, hs_ref[...])


def _sample_merge(att, aux, sp_t, sc_t, hs, wout, poolw, pscale, cw, cb, cg, cbeta, pos):
    args = (att, aux, sp_t, sc_t, hs, wout, poolw, pscale, cw, cb, cg, cbeta)
    full = lambda a: pl.BlockSpec(a.shape, lambda i, nd=a.ndim: (0,) * nd)
    return pl.pallas_call(
        functools.partial(_sample_merge_kernel, pos=pos),
        out_shape=jax.ShapeDtypeStruct(hs.shape, F32),
        grid=(1,),
        in_specs=[full(a) for a in args],
        out_specs=full(hs),
        compiler_params=pltpu.CompilerParams(
            dimension_semantics=("arbitrary",), vmem_limit_bytes=VMEM_LIMIT),
        name="sample_merge",
    )(*args)


def _rope_table(pos):
    pos = pos.astype(F32)

    def cs(half):
        inv = ROPE_THETA ** (-(jnp.arange(half, dtype=F32) / half))
        ang = pos[:, None] * inv[None, :]
        return jnp.cos(ang).T, jnp.sin(ang).T

    c8, s8 = cs(HEAD_DIM // 8)
    c4, s4 = cs(D_IDX // 8)
    return jnp.concatenate([c8, s8, c4, c4, -s4, s4], axis=0)


def _permute_w_in(w):
    d = w.shape[0]
    o = 0
    parts = {}
    for name, width in (("q", D_A), ("k", D_A), ("v", D_A), ("ga", D_A), ("qi", D_QI),
                        ("wi", N_IDX_HEADS), ("ki", D_IDX), ("ub", D_B), ("gb", D_B),
                        ("cv", D_C), ("cg", D_C), ("gc", D_C)):
        parts[name] = w[:, o:o + width]
        o += width
    pad = jnp.zeros((d, LANES - D_IDX - N_IDX_HEADS), w.dtype)
    order = ["q", "k", "v", "ga", "qi", "ub", "gb", "cv", "cg", "gc", "ki", "wi"]
    return jnp.concatenate([parts[n] for n in order] + [pad], axis=1).astype(BF16)


def _block_diag(pw):
    n_g, g, _ = pw.shape
    out = jnp.zeros((n_g * g, n_g * g), pw.dtype)
    for i in range(n_g):
        out = out.at[i * g:(i + 1) * g, i * g:(i + 1) * g].set(pw[i])
    return out.astype(BF16)


def kernel(x_prompt, x_sample, cache_k, cache_v, cache_kidx, state_pool, state_conv, page_table,
           meta_tokens, norm_g, w_in, q_norm_g, k_norm_g, pool_w, pool_scale, conv_w, conv_b,
           conv_norm_g, conv_norm_b, w_out):
    n_b, seq, d_model = x_prompt.shape
    n_q = x_sample.shape[0]
    depth = w_in.shape[0]
    n_pages = page_table.shape[1]
    past = n_pages * PAGE
    t_real = seq + N_META
    t_pad = -(-t_real // TM) * TM
    top_p = min(TOPK_MAX, t_real // 4)
    top_s = min(TOPK_MAX, (past + 1) // 4)

    meta = jnp.broadcast_to(meta_tokens[None].astype(F32), (n_b, N_META, d_model))
    tail = jnp.zeros((n_b, t_pad - t_real, d_model), F32)
    hp = jnp.concatenate([meta, x_prompt, tail], axis=1).reshape(n_b * t_pad, d_model)
    hs = x_sample.reshape(n_q, d_model)

    rope_p = _rope_table(jnp.arange(t_pad))
    rope_s = _rope_table(jnp.full((n_q,), past))
    cache_k_t = jnp.transpose(cache_k, (0, 1, 3, 4, 2))
    cache_v_t = jnp.transpose(cache_v, (0, 1, 3, 4, 2))
    cache_kidx_t = jnp.transpose(cache_kidx, (0, 1, 3, 2))

    outs = {name: [] for name in ("kp", "vp", "kip", "plp", "cvp", "ks", "vs", "kis", "pls", "cvs")}
    for l in range(depth):
        w_l = _permute_w_in(w_in[l])
        g_l = norm_g[l].reshape(1, d_model)
        qg = q_norm_g[l].reshape(HEAD_DIM, 1)
        kg = k_norm_g[l].reshape(HEAD_DIM, 1)
        wout = w_out[l].astype(BF16)
        poolw = _block_diag(pool_w[l])
        pscale = pool_scale[l].reshape(1, D_B)
        cw = conv_w[l]
        cb = conv_b[l].reshape(1, D_C)
        cg = conv_norm_g[l].reshape(1, D_C)
        cbeta = conv_norm_b[l].reshape(1, D_C)

        heads_last = lambda xT: jnp.transpose(xT.reshape(xT.shape[0], N_HEADS, HEAD_DIM, -1), (0, 3, 1, 2))

        q, kTf, kT, vTf, vb, aux, qi, wk, kiTf, kiT = _project(
            hp, n_b, t_real, TM, g_l, w_l, rope_p, qg, kg)
        q_s, kTf_s, _, vTf_s, _, aux_s, qi_s, wk_s, kiTf_s, _ = _project(
            hs, 1, n_q, n_q, g_l, w_l, rope_s, qg, kg)
        qi3 = qi_s.reshape(n_q, N_IDX_HEADS, D_IDX)
        wi3 = wk_s[:, WK_WI:WK_WI + N_IDX_HEADS].reshape(n_q, N_IDX_HEADS, 1)
        kin3 = wk_s[:, WK_KI:WK_KI + D_IDX].reshape(n_q, 1, D_IDX)
        mask = _sample_index(l, page_table, cache_kidx_t, qi3, wi3, kin3, top_s)
        q8 = q_s.astype(F32).reshape(n_q, N_HEADS, HEAD_DIM)
        k_new = heads_last(kTf_s).reshape(n_q, 1, N_HEADS, HEAD_DIM)
        v_new = heads_last(vTf_s).reshape(n_q, 1, N_HEADS, HEAD_DIM)

        att, att_s = _attention(
            l, page_table, q, qi, wk, kT, kiT, vb, cache_k_t, cache_v_t, q8, jnp.swapaxes(q8, 1, 2),
            k_new[:, 0], jnp.swapaxes(v_new[:, 0], 1, 2), mask.reshape(n_q, n_pages + 1, PAGE),
            n_b, t_pad, t_real, top_p)
        hp = _prompt_merge(att, aux, hp, wout, poolw, pscale, cw, cb, cg, cbeta, n_b, t_pad)
        aux3 = aux.reshape(n_b, t_pad, N_AUX)
        outs["kp"].append(heads_last(kTf))
        outs["vp"].append(heads_last(vTf))
        outs["kip"].append(jnp.swapaxes(kiTf, 1, 2))
        outs["plp"].append(aux3[:, t_real - POOL_STATE:t_real, A_UB:A_GB])
        outs["cvp"].append(aux3[:, t_real - CONV_STATE:t_real, A_UC:A_GC])

        sp_t = jnp.swapaxes(state_pool[l], 0, 1)
        sc_t = jnp.swapaxes(state_conv[l], 0, 1)
        hs = _sample_merge(att_s, aux_s, sp_t, sc_t, hs, wout, poolw, pscale, cw, cb, cg, cbeta, past)
        outs["ks"].append(k_new)
        outs["vs"].append(v_new)
        outs["kis"].append(jnp.swapaxes(kiTf_s, 1, 2).reshape(n_q, 1, D_IDX))
        outs["pls"].append(jnp.concatenate([state_pool[l][:, 1:], aux_s[:, None, A_UB:A_GB]], axis=1))
        outs["cvs"].append(jnp.concatenate([state_conv[l][:, 1:], aux_s[:, None, A_UC:A_GC]], axis=1))

    y_prompt = hp.reshape(n_b, t_pad, d_model)[:, N_META:t_real]
    y_sample = hs.reshape(n_q, 1, d_model)
    st = lambda name: jnp.stack(outs[name], axis=0)
    return (y_prompt, y_sample, st("kp"), st("vp"), st("kip"), st("plp"), st("cvp"),
            st("ks"), st("vs"), st("kis"), st("pls"), st("cvs"))
```

```python
import functools

import jax
import jax.numpy as jnp
from jax import lax
from jax.experimental import pallas as pl
from jax.experimental.pallas import tpu as pltpu

F32 = jnp.float32
BF16 = jnp.bfloat16

LANES = 128
SUBLANES = 8
N_META = 16
HEAD_DIM = 64
N_HEADS = 8
D_A = N_HEADS * HEAD_DIM
N_IDX_HEADS = 8
D_IDX = 32
D_QI = N_IDX_HEADS * D_IDX
D_B = 256
D_C = 256
POOL_WINDOWS = (2, 4, 8, 16)
POOL_GROUP = D_B // len(POOL_WINDOWS)
POOL_STATE = 15
CONV_WIDTH = 31
CONV_STATE = CONV_WIDTH - 1
HALO = 32
TOPK_MAX = 256
ROPE_THETA = 500000.0
EPS = 1e-6
PAGE = 128
LOG2E = 1.4426950408889634
Q_SCALE = HEAD_DIM ** -0.5 * LOG2E
IDX_SCALE = (D_IDX ** -0.5) * (N_IDX_HEADS ** -0.5)

QB = 128
CK = 384
TM = 384
MAX_BISECT = 64
BIG_INDEX = 1.0e9
MASKED = -1.0e30
TINY = 1.1754944e-38
ACC_ROWS = 64
ABOVE_MAX_STEP = 2.0 ** -20
PASSES_PER_TEST = 4
CHUNK_GROUP = 4
VMEM_LIMIT = 56 * 1024 * 1024

C_Q, C_K, C_V, C_GA, C_QI = 0, 512, 1024, 1536, 2048
C_UB, C_GB, C_CV, C_CG, C_GC, C_KI = 2304, 2560, 2816, 3072, 3328, 3584
N_PERM = C_KI + LANES
A_GA, A_UB, A_GB, A_UC, A_GC, N_AUX = 0, 512, 768, 1024, 1280, 1536
WK_KI, WK_WI = 0, D_IDX


def _sigmoid(x):
    return 0.5 * jnp.tanh(0.5 * x) + 0.5


def _silu(x):
    return x * _sigmoid(x)


def _proj_kernel(x_ref, g_ref, w_ref, rope_ref, qg_ref, kg_ref,
                 q_ref, kTf_ref, kT_ref, vTf_ref, vb_ref, aux_ref, qi_ref, wk_ref, kiTf_ref, kiT_ref,
                 t_scr):
    x = x_ref[...]
    ms = jnp.mean(x * x, axis=-1, keepdims=True)
    h = (x * lax.rsqrt(ms + EPS) * g_ref[...]).astype(BF16)

    def proj(c0, c1):
        return jnp.dot(h, w_ref[:, c0:c1], preferred_element_type=F32)

    cos8 = rope_ref[0:8, :]
    sin8 = rope_ref[8:16, :]
    cc4 = rope_ref[16:24, :]
    ss4 = rope_ref[24:32, :]

    zT = proj(C_Q, C_V).T
    for j in range(2 * N_HEADS):
        xh = zT[j * HEAD_DIM:(j + 1) * HEAD_DIM, :]
        g = qg_ref[...] if j < N_HEADS else kg_ref[...]
        xn = xh * lax.rsqrt(jnp.mean(xh * xh, axis=0, keepdims=True) + EPS) * g
        x1 = xn[0:8, :]
        x2 = xn[8:16, :]
        r0 = j * HEAD_DIM
        t_scr[r0:r0 + 8, :] = x1 * cos8 - x2 * sin8
        t_scr[r0 + 8:r0 + 16, :] = x2 * cos8 + x1 * sin8
        t_scr[r0 + 16:r0 + HEAD_DIM, :] = xn[16:HEAD_DIM, :]
    q_ref[...] = (t_scr[0:D_A, :].T * Q_SCALE).astype(BF16)
    kT = t_scr[D_A:2 * D_A, :]
    kTf_ref[...] = kT
    kT_ref[...] = kT.astype(BF16)

    zv = proj(C_V, C_GA)
    vTf_ref[...] = zv.T
    vb_ref[...] = zv.astype(BF16)

    aux_ref[:, A_GA:A_UB] = proj(C_GA, C_QI)
    aux_ref[:, A_UB:A_UC] = proj(C_UB, C_CV)
    zc = proj(C_CV, C_KI)
    aux_ref[:, A_UC:A_GC] = zc[:, 0:D_C] * _sigmoid(zc[:, D_C:2 * D_C])
    aux_ref[:, A_GC:N_AUX] = zc[:, 2 * D_C:3 * D_C]

    ziT = proj(C_QI, C_UB).T
    parts = []
    for hh in range(N_IDX_HEADS):
        r0 = hh * D_IDX
        x8 = ziT[r0:r0 + 8, :]
        parts.append(x8 * cc4 + pltpu.roll(x8, 4, 0) * ss4)
        parts.append(ziT[r0 + 8:r0 + D_IDX, :])
    qi_ref[...] = jnp.concatenate(parts, axis=0).T.astype(BF16)

    zwT = proj(C_KI, N_PERM).T
    x8 = zwT[0:8, :]
    ki = jnp.concatenate([x8 * cc4 + pltpu.roll(x8, 4, 0) * ss4, zwT[8:D_IDX, :]], axis=0)
    wi = zwT[D_IDX:D_IDX + N_IDX_HEADS, :] * IDX_SCALE
    wk_ref[...] = jnp.concatenate([ki, wi, zwT[D_IDX + N_IDX_HEADS:, :]], axis=0).T
    kiTf_ref[...] = ki
    kiT_ref[...] = jnp.concatenate([ki] * (LANES // D_IDX), axis=0).astype(BF16)


def _project(x2d, n_batch, t_real, tm, g, w, rope, qg, kg):
    rows = x2d.shape[0]
    t = rows // n_batch
    nt = t // tm
    d = x2d.shape[1]
    row_map = lambda b, i: (b * nt + i, 0)
    col_map = lambda b, i: (b, 0, i)
    const = lambda b, i: (0, 0)
    out_shape = (
        jax.ShapeDtypeStruct((rows, D_A), BF16),
        jax.ShapeDtypeStruct((n_batch, D_A, t_real), F32),
        jax.ShapeDtypeStruct((n_batch, D_A, t), BF16),
        jax.ShapeDtypeStruct((n_batch, D_A, t_real), F32),
        jax.ShapeDtypeStruct((rows, D_A), BF16),
        jax.ShapeDtypeStruct((rows, N_AUX), F32),
        jax.ShapeDtypeStruct((rows, D_QI), BF16),
        jax.ShapeDtypeStruct((rows, LANES), F32),
        jax.ShapeDtypeStruct((n_batch, D_IDX, t_real), F32),
        jax.ShapeDtypeStruct((n_batch, LANES, t), BF16),
    )
    out_specs = (
        pl.BlockSpec((tm, D_A), row_map),
        pl.BlockSpec((None, D_A, tm), col_map),
        pl.BlockSpec((None, D_A, tm), col_map),
        pl.BlockSpec((None, D_A, tm), col_map),
        pl.BlockSpec((tm, D_A), row_map),
        pl.BlockSpec((tm, N_AUX), row_map),
        pl.BlockSpec((tm, D_QI), row_map),
        pl.BlockSpec((tm, LANES), row_map),
        pl.BlockSpec((None, D_IDX, tm), col_map),
        pl.BlockSpec((None, LANES, tm), col_map),
    )
    in_specs = [
        pl.BlockSpec((tm, d), row_map),
        pl.BlockSpec((1, d), const),
        pl.BlockSpec((d, N_PERM), const),
        pl.BlockSpec((32, tm), lambda b, i: (0, i)),
        pl.BlockSpec((HEAD_DIM, 1), const),
        pl.BlockSpec((HEAD_DIM, 1), const),
    ]
    return pl.pallas_call(
        _proj_kernel,
        out_shape=out_shape,
        grid=(n_batch, nt),
        in_specs=in_specs,
        out_specs=out_specs,
        scratch_shapes=[pltpu.VMEM((2 * D_A, tm), F32)],
        compiler_params=pltpu.CompilerParams(
            dimension_semantics=("parallel", "parallel"), vmem_limit_bytes=VMEM_LIMIT),
        name="proj",
    )(x2d, g, w, rope, qg, kg)


def _fold_t(it_scr, n_blk, ck, fn, inits):
    per = ck // LANES

    def body(c, accs):
        row = pl.multiple_of(c * ck, LANES)
        return fn(accs, it_scr[pl.ds(row, ck), :], row)

    def tail(c, accs):
        row = pl.multiple_of(c * LANES, LANES)
        return fn(accs, it_scr[pl.ds(row, LANES), :], row)

    accs = tuple(jnp.full((ACC_ROWS, LANES), v, F32) for v in inits)
    accs = lax.fori_loop(0, n_blk // per, body, accs)
    return lax.fori_loop((n_blk // per) * per, n_blk, tail, accs)


def _map_t(it_scr, n_blk, ck, fn):
    per = ck // LANES

    def body(c, carry):
        row = pl.multiple_of(c * ck, LANES)
        it_scr[pl.ds(row, ck), :] = fn(it_scr[pl.ds(row, ck), :], row)
        return carry

    def tail(c, carry):
        row = pl.multiple_of(c * LANES, LANES)
        it_scr[pl.ds(row, LANES), :] = fn(it_scr[pl.ds(row, LANES), :], row)
        return carry

    lax.fori_loop(0, n_blk // per, body, 0)
    lax.fori_loop((n_blk // per) * per, n_blk, tail, 0)


def _vreg_fold(x, op):
    return op(x.reshape(x.shape[0] // ACC_ROWS, ACC_ROWS, LANES), axis=0)


def _count_ge_t(it_scr, n_blk, ck, mid):
    (acc,) = _fold_t(
        it_scr, n_blk, ck,
        lambda a, blk, row: (a[0] + _vreg_fold(jnp.where(blk >= mid, 1.0, 0.0), jnp.sum),), (0.0,))
    return jnp.sum(acc, axis=0, keepdims=True)


def _select_bounds_t(it_scr, n_blk, ck, top_k, trivial):
    kf = float(top_k)
    ninf = -jnp.inf
    pinf = jnp.inf
    land, lor, lnot = jnp.logical_and, jnp.logical_or, jnp.logical_not
    passes = PASSES_PER_TEST

    mn, mx = _fold_t(
        it_scr, n_blk, ck,
        lambda a, blk, row: (jnp.minimum(a[0], _vreg_fold(jnp.where(blk == ninf, pinf, blk), jnp.min)),
                             jnp.maximum(a[1], _vreg_fold(blk, jnp.max))),
        (pinf, ninf))
    cpos, cnn = _fold_t(
        it_scr, n_blk, ck,
        lambda a, blk, row: (a[0] + _vreg_fold(jnp.where(blk > 0.0, 1.0, 0.0), jnp.sum),
                             a[1] + _vreg_fold(jnp.where(blk >= 0.0, 1.0, 0.0), jnp.sum)),
        (0.0, 0.0))
    col_min = jnp.min(mn, axis=0, keepdims=True)
    col_max = jnp.max(mx, axis=0, keepdims=True)
    c_pos = jnp.sum(cpos, axis=0, keepdims=True)
    c_nn = jnp.sum(cnn, axis=0, keepdims=True)
    above_max = col_max + jnp.abs(col_max) * ABOVE_MAX_STEP + TINY

    zero_tie = land(land(c_pos < kf, c_nn >= kf), lnot(trivial))
    neg_side = c_nn < kf
    lo0 = jnp.where(zero_tie, 0.0, jnp.where(neg_side, col_min, TINY))
    lo0 = jnp.where(trivial, col_min, lo0)
    hi0 = jnp.where(trivial, pinf,
                    jnp.where(zero_tie, jnp.where(c_nn == kf, pinf, TINY),
                              jnp.where(neg_side, 0.0, above_max)))
    pos_exact = land(lnot(neg_side), c_pos == kf)
    hi0 = jnp.where(land(pos_exact, lnot(lor(trivial, zero_tie))), pinf, hi0)
    done0 = lor(trivial, lor(land(zero_tie, c_nn == kf), pos_exact))
    res0 = jnp.where(done0, 1.0, 0.0)
    act0 = jnp.where(lor(done0, zero_tie), 0.0, 1.0)

    def bis_cond(st):
        return jnp.logical_and(st[0] < MAX_BISECT, st[1] > 0.0)

    def bis_body(st):
        it, _, lo, hi, res, act = st
        for _ in range(passes):
            mid = 0.5 * lo + 0.5 * hi
            ok = land(act > 0.0, land(mid > lo, mid < hi))
            c = _count_ge_t(it_scr, n_blk, ck, mid)
            eq = land(ok, c == kf)
            gt = land(ok, c > kf)
            lt = land(ok, c < kf)
            lo = jnp.where(lor(eq, gt), mid, lo)
            hi = jnp.where(eq, pinf, jnp.where(lt, mid, hi))
            res = jnp.where(eq, 1.0, res)
            act = jnp.where(lor(eq, lnot(ok)), 0.0, act)
        return it + passes, jnp.sum(act), lo, hi, res, act

    _, _, lo, hi, res, _ = lax.while_loop(
        bis_cond, bis_body, (jnp.int32(0), jnp.sum(act0), lo0, hi0, res0, act0))

    def cut_band():
        need = kf - _count_ge_t(it_scr, n_blk, ck, hi)

        def to_index(blk, row):
            idx = (lax.broadcasted_iota(jnp.int32, blk.shape, 0) + row).astype(F32)
            return jnp.where(land(blk >= lo, blk < hi), idx, BIG_INDEX)

        _map_t(it_scr, n_blk, ck, to_index)

        def count_band(j):
            (acc,) = _fold_t(
                it_scr, n_blk, ck,
                lambda a, blk, row: (a[0] + _vreg_fold(jnp.where(blk <= j, 1.0, 0.0), jnp.sum),), (0.0,))
            return jnp.sum(acc, axis=0, keepdims=True)

        max_steps = int(it_scr.shape[0]).bit_length() + 1

        def j_cond(st):
            return jnp.logical_and(st[0] < max_steps, st[1] > 0.0)

        def j_body(st):
            it, _, jl, jh, jc, open_ = st
            jm = jnp.floor((jl + jh + 1.0) * 0.5)
            c = count_band(jm)
            is_open = open_ > 0.0
            eq = land(is_open, c == need)
            jl = jnp.where(land(is_open, c < need), jm, jl)
            jh = jnp.where(land(is_open, c > need), jm, jh)
            jc = jnp.where(eq, jm, jc)
            open_ = jnp.where(eq, 0.0, open_)
            return it + 1, jnp.sum(open_), jl, jh, jc, open_

        open0 = 1.0 - res
        full = lambda v: jnp.full((1, LANES), v, F32)
        st0 = (jnp.int32(0), jnp.sum(open0), full(-1.0), full(float(it_scr.shape[0] - 1)),
               full(BIG_INDEX), open0)
        return lax.while_loop(j_cond, j_body, st0)[4]

    jcut = lax.cond(jnp.sum(1.0 - res) > 0.0, cut_band, lambda: jnp.full((1, LANES), BIG_INDEX, F32))
    return lo, hi, jcut


def _chunked(n_blk, body):
    per = CK // LANES
    n_full = n_blk // per
    rem = n_blk - n_full * per
    base = (n_full // CHUNK_GROUP) * CHUNK_GROUP

    def run(c0, count):
        for u in range(count):
            body(pl.multiple_of((c0 + u) * CK, LANES), CK)

    def grouped(j, carry):
        run(CHUNK_GROUP * j, CHUNK_GROUP)
        return carry

    lax.fori_loop(0, n_full // CHUNK_GROUP, grouped, 0)
    for count in range(CHUNK_GROUP - 1, 0, -1):
        pl.when(n_full - base == count)(functools.partial(run, base, count))
    for r in range(1, per):
        pl.when(rem == r)(functools.partial(body, pl.multiple_of(n_full * CK, LANES), r * LANES))


def _as_column(row):
    return jnp.broadcast_to(row, (LANES, LANES)).T


def _selected(blk, col, lo, hi, jcut):
    lane = lax.broadcasted_iota(jnp.int32, blk.shape, 1)
    idx = (lane + col).astype(F32)
    return jnp.logical_or(blk >= hi, jnp.logical_and(blk >= lo, idx <= jcut))


def _attn_kernel(pt_ref, q_ref, qi_ref, wk_ref, kT_ref, kiT_ref, v_ref, *rest, t_real, top_k, n_pages):
    del pt_ref
    k_pages = rest[:n_pages]
    v_pages = rest[n_pages:2 * n_pages]
    (q8_ref, qT_ref, kn_ref, vnT_ref, mask_ref, o_ref, os_ref,
     i_scr, it_scr, s_scr, m_scr, l_scr, acc_scr, oT_scr) = rest[2 * n_pages:]
    i = pl.program_id(1)
    q_pos = i * QB + lax.broadcasted_iota(jnp.int32, (QB, 1), 0)
    lane = lax.broadcasted_iota(jnp.int32, (QB, LANES), 1)

    qi = qi_ref[...]
    q = q_ref[...]
    qiz, qz = [], []
    per = LANES // D_IDX
    for h in range(N_IDX_HEADS):
        blk = qi[:, (h // per) * LANES:(h // per + 1) * LANES]
        qiz.append(jnp.where(lane // D_IDX == h % per, blk, jnp.zeros_like(blk)))
    for h in range(N_HEADS):
        blk = q[:, (h // 2) * LANES:(h // 2 + 1) * LANES]
        qz.append(jnp.where(lane // HEAD_DIM == h % 2, blk, jnp.zeros_like(blk)))
    w = wk_ref[:, WK_WI:WK_WI + N_IDX_HEADS]

    def score_chunk(col, width):
        kc = kiT_ref[:, pl.ds(col, width)]
        acc = jnp.zeros((QB, width), F32)
        for h in range(N_IDX_HEADS):
            d = jnp.dot(qiz[h], kc, preferred_element_type=F32)
            acc = acc + w[:, h:h + 1] * jnp.maximum(d, 0.0)
        key_pos = col + lax.broadcasted_iota(jnp.int32, (QB, width), 1)
        sc = jnp.where(key_pos <= q_pos, acc, -jnp.inf)
        i_scr[:, pl.ds(col, width)] = sc
        for k in range(width // LANES):
            row = pl.multiple_of(col + k * LANES, LANES)
            it_scr[pl.ds(row, LANES), :] = sc[:, k * LANES:(k + 1) * LANES].T

    _chunked(i + 1, score_chunk)

    q_row = i * QB + lax.broadcasted_iota(jnp.int32, (1, LANES), 1)
    trivial = jnp.logical_or(q_row + 1 <= top_k, q_row >= t_real)
    lo, hi, jcut = (_as_column(r)[:, 0:1] for r in _select_bounds_t(it_scr, i + 1, CK, top_k, trivial))

    m_scr[...] = jnp.full(m_scr.shape, MASKED, F32)

    def logits_chunk(col, width):
        sel = _selected(i_scr[:, pl.ds(col, width)], col, lo, hi, jcut)
        bias = jnp.where(sel, 0.0, MASKED)
        for h in range(N_HEADS):
            pr = h // 2
            kc = kT_ref[pr * LANES:(pr + 1) * LANES, pl.ds(col, width)]
            s = jnp.dot(qz[h], kc, preferred_element_type=F32) + bias
            s_scr[h, :, pl.ds(col, width)] = s
            mx = s[:, 0:LANES]
            for k in range(1, width // LANES):
                mx = jnp.maximum(mx, s[:, k * LANES:(k + 1) * LANES])
            m_scr[h] = jnp.maximum(m_scr[h], mx)

    _chunked(i + 1, logits_chunk)

    for h in range(N_HEADS):
        m_scr[h] = jnp.broadcast_to(jnp.max(m_scr[h], axis=1, keepdims=True), (QB, LANES))
    l_scr[...] = jnp.zeros(l_scr.shape, F32)
    acc_scr[...] = jnp.zeros(acc_scr.shape, F32)

    def value_chunk(col, width):
        for h in range(N_HEADS):
            pr = h // 2
            m = m_scr[h]
            ps = [jnp.exp2(s_scr[h, :, pl.ds(pl.multiple_of(col + k * LANES, LANES), LANES)] - m)
                  for k in range(width // LANES)]
            lsum = ps[0]
            for pk in ps[1:]:
                lsum = lsum + pk
            l_scr[h] = l_scr[h] + lsum
            p = (ps[0] if len(ps) == 1 else jnp.concatenate(ps, axis=1)).astype(BF16)
            vc = v_ref[pl.ds(col, width), pr * LANES:(pr + 1) * LANES]
            acc_scr[h] = acc_scr[h] + jnp.dot(p, vc, preferred_element_type=F32)

    _chunked(i + 1, value_chunk)

    for pr in range(N_HEADS // 2):
        first = lane < HEAD_DIM
        num = jnp.where(first, acc_scr[2 * pr], acc_scr[2 * pr + 1])
        den = jnp.where(first, jnp.sum(l_scr[2 * pr], axis=1, keepdims=True),
                        jnp.sum(l_scr[2 * pr + 1], axis=1, keepdims=True))
        o_ref[:, pr * LANES:(pr + 1) * LANES] = num / den

    step = pl.program_id(0) * pl.num_programs(1) + i
    n_steps = pl.num_programs(0) * pl.num_programs(1)

    @pl.when(step == 0)
    def _():
        oT_scr[...] = jnp.zeros(oT_scr.shape, F32)

    @pl.when(step < os_ref.shape[0])
    def _():
        _sample_attend(step, k_pages, v_pages, q8_ref, qT_ref, kn_ref, vnT_ref, mask_ref, oT_scr)

    @pl.when(step == n_steps - 1)
    def _():
        os_ref[...] = oT_scr[...].T


def _attention(layer, page_table, q, qi, wk, kT, kiT, vb, cache_k_t, cache_v_t, q8, qT, kn, vnT, mask3,
               n_batch, t_pad, t_real, top_k):
    nq = t_pad // QB
    n_q, n_pages = page_table.shape
    assert n_q == LANES, "one output lane per sample query"
    assert n_batch * nq >= n_q, "one sample query per grid step"
    kern = functools.partial(_attn_kernel, t_real=t_real, top_k=top_k, n_pages=n_pages)
    row_map = lambda b, i, pt: (b * nq + i, 0)
    query = lambda b, i: jnp.minimum(b * nq + i, n_q - 1)
    per_q = lambda b, i, pt: (query(b, i), 0, 0)
    once = pl.Buffered(1)

    def page_spec(p):
        return pl.BlockSpec((None, None, N_HEADS, HEAD_DIM, PAGE),
                            lambda b, i, pt: (layer, pt[query(b, i), p], 0, 0, 0))

    in_specs = [
        pl.BlockSpec((QB, D_A), row_map),
        pl.BlockSpec((QB, D_QI), row_map),
        pl.BlockSpec((QB, LANES), row_map),
        pl.BlockSpec((None, D_A, t_pad), lambda b, i, pt: (b, 0, 0), pipeline_mode=once),
        pl.BlockSpec((None, LANES, t_pad), lambda b, i, pt: (b, 0, 0), pipeline_mode=once),
        pl.BlockSpec((t_pad, D_A), lambda b, i, pt: (b, 0), pipeline_mode=once),
    ]
    in_specs += [page_spec(p) for p in range(n_pages)] + [page_spec(p) for p in range(n_pages)]
    in_specs += [
        pl.BlockSpec((None, N_HEADS, HEAD_DIM), per_q),
        pl.BlockSpec((None, HEAD_DIM, N_HEADS), per_q),
        pl.BlockSpec((None, N_HEADS, HEAD_DIM), per_q),
        pl.BlockSpec((None, HEAD_DIM, N_HEADS), per_q),
        pl.BlockSpec((None, n_pages + 1, PAGE), per_q),
    ]
    return pl.pallas_call(
        kern,
        out_shape=(jax.ShapeDtypeStruct((n_batch * t_pad, D_A), F32),
                   jax.ShapeDtypeStruct((n_q, D_A), F32)),
        grid_spec=pltpu.PrefetchScalarGridSpec(
            num_scalar_prefetch=1,
            grid=(n_batch, nq),
            in_specs=in_specs,
            out_specs=(pl.BlockSpec((QB, D_A), row_map),
                       pl.BlockSpec((n_q, D_A), lambda b, i, pt: (0, 0))),
            scratch_shapes=[
                pltpu.VMEM((QB, t_pad), F32),
                pltpu.VMEM((t_pad, QB), F32),
                pltpu.VMEM((N_HEADS, QB, t_pad), F32),
                pltpu.VMEM((N_HEADS, QB, LANES), F32),
                pltpu.VMEM((N_HEADS, QB, LANES), F32),
                pltpu.VMEM((N_HEADS, QB, LANES), F32),
                pltpu.VMEM((D_A, n_q), F32),
            ],
        ),
        compiler_params=pltpu.CompilerParams(
            dimension_semantics=("arbitrary", "arbitrary"), vmem_limit_bytes=VMEM_LIMIT),
        name="attention",
    )(page_table, q, qi, wk, kT, kiT, vb, *([cache_k_t] * n_pages), *([cache_v_t] * n_pages),
      q8, qT, kn, vnT, mask3)


def _gate_project(att, aux_ref, pool, conv, wout_ref, resid):
    ya = att * _silu(aux_ref[:, A_GA:A_UB])
    yb = pool * _silu(aux_ref[:, A_GB:A_UC])
    yc = conv * _silu(aux_ref[:, A_GC:N_AUX])
    y = jnp.concatenate([ya, yb, yc], axis=1).astype(BF16)
    return resid + jnp.dot(y, wout_ref[...], preferred_element_type=F32)


def _pool_mix(sums, u, cnts, poolw_ref, pscale_ref):
    lane = lax.broadcasted_iota(jnp.int32, u.shape, 1)
    d = sums[-1] / cnts[-1]
    for g in range(len(POOL_WINDOWS) - 2, -1, -1):
        d = jnp.where(lane < (g + 1) * POOL_GROUP, sums[g] / cnts[g], d)
    d = (d - u).astype(BF16)
    return jnp.dot(d, poolw_ref[...], preferred_element_type=F32) * pscale_ref[...]


def _conv_norm(y, cb_ref, cg_ref, cbeta_ref):
    y = y + cb_ref[...]
    mu = jnp.mean(y, axis=-1, keepdims=True)
    yc = y - mu
    var = jnp.mean(yc * yc, axis=-1, keepdims=True)
    return _silu(yc * lax.rsqrt(var + EPS) * cg_ref[...] + cbeta_ref[...])


def _row_windows(ext_scr, rot_scr, tm):
    n = rot_scr.shape[1]
    for r in range(1, SUBLANES):
        rot_scr[r - 1] = ext_scr[r:r + n, :]

    def window(off):
        r = off % SUBLANES
        a = off - r
        return ext_scr[a:a + tm, :] if r == 0 else rot_scr[r - 1, a:a + tm, :]

    return window


def _merge_kernel(att_ref, aux_ref, hub_ref, huc_ref, hp_ref, wout_ref, poolw_ref, pscale_ref,
                  cw_ref, cb_ref, cg_ref, cbeta_ref, o_ref, ub_scr, uc_scr, ub_rot, uc_rot):
    i = pl.program_id(1)
    tm = att_ref.shape[0]
    first = i == 0
    ub = aux_ref[:, A_UB:A_GB]
    uc = aux_ref[:, A_UC:A_GC]
    ub_scr[0:HALO, :] = jnp.where(first, 0.0, hub_ref[...])
    uc_scr[0:HALO, :] = jnp.where(first, 0.0, huc_ref[...])
    ub_scr[HALO:HALO + tm, :] = ub
    uc_scr[HALO:HALO + tm, :] = uc
    ub_win = _row_windows(ub_scr, ub_rot, tm)
    uc_win = _row_windows(uc_scr, uc_rot, tm)

    pos = i * tm + lax.broadcasted_iota(jnp.int32, (tm, 1), 0)
    run = ub
    sums, cnts = [], []
    for j in range(1, POOL_WINDOWS[-1]):
        run = run + ub_win(HALO - j)
        if j + 1 in POOL_WINDOWS:
            sums.append(run)
            cnts.append(jnp.minimum(pos + 1, j + 1).astype(F32))
    pool = _pool_mix(sums, ub, cnts, poolw_ref, pscale_ref)

    y = jnp.zeros((tm, D_C), F32)
    for j in range(CONV_WIDTH):
        y = y + cw_ref[j:j + 1, :] * uc_win(HALO - CONV_STATE + j)
    conv = _conv_norm(y, cb_ref, cg_ref, cbeta_ref)

    o_ref[...] = _gate_project(att_ref[...], aux_ref, pool, conv, wout_ref, hp_ref[...])


def _prompt_merge(att, aux, hp, wout, poolw, pscale, cw, cb, cg, cbeta, n_batch, t_pad):
    nt = t_pad // TM
    hb = TM // HALO
    row_map = lambda b, i: (b * nt + i, 0)
    const = lambda b, i: (0, 0)
    d = hp.shape[1]

    def halo_map(col_block):
        return lambda b, i: (jnp.maximum((b * nt + i) * hb - 1, 0), col_block)

    return pl.pallas_call(
        _merge_kernel,
        out_shape=jax.ShapeDtypeStruct(hp.shape, F32),
        grid=(n_batch, nt),
        in_specs=[
            pl.BlockSpec((TM, D_A), row_map),
            pl.BlockSpec((TM, N_AUX), row_map),
            pl.BlockSpec((HALO, D_B), halo_map(A_UB // D_B)),
            pl.BlockSpec((HALO, D_C), halo_map(A_UC // D_C)),
            pl.BlockSpec((TM, d), row_map),
            pl.BlockSpec(wout.shape, const),
            pl.BlockSpec(poolw.shape, const),
            pl.BlockSpec((1, D_B), const),
            pl.BlockSpec(cw.shape, const),
            pl.BlockSpec((1, D_C), const),
            pl.BlockSpec((1, D_C), const),
            pl.BlockSpec((1, D_C), const),
        ],
        out_specs=pl.BlockSpec((TM, d), row_map),
        scratch_shapes=[pltpu.VMEM((HALO + TM, D_B), F32), pltpu.VMEM((HALO + TM, D_C), F32),
                        pltpu.VMEM((SUBLANES - 1, HALO + TM - SUBLANES, D_B), F32),
                        pltpu.VMEM((SUBLANES - 1, HALO + TM - SUBLANES, D_C), F32)],
        compiler_params=pltpu.CompilerParams(
            dimension_semantics=("parallel", "parallel"), vmem_limit_bytes=VMEM_LIMIT),
        name="prompt_merge",
    )(att, aux, aux, aux, hp, wout, poolw, pscale, cw, cb, cg, cbeta)


def _sample_index_kernel(pt_ref, *refs, n_pages, top_k):
    del pt_ref
    ki_pages = refs[:SUBLANES * n_pages]
    qi_ref, wi_ref, kin_ref, mask_ref, i_scr, it_scr = refs[SUBLANES * n_pages:]
    g = pl.program_id(0)
    past = n_pages * PAGE
    width = i_scr.shape[1]
    sub = lax.broadcasted_iota(jnp.int32, (SUBLANES, width), 0)
    lane1 = lax.broadcasted_iota(jnp.int32, (1, LANES), 1)

    rows = jnp.full((SUBLANES, width), -jnp.inf, F32)
    for j in range(SUBLANES):
        qi = qi_ref[j]
        w = wi_ref[j]
        pages = ki_pages[j * n_pages:(j + 1) * n_pages]
        kp = jnp.concatenate([pg[...] for pg in pages], axis=1).astype(BF16)
        d = jnp.dot(qi, kp, preferred_element_type=F32)
        sc = jnp.sum(w * jnp.maximum(d, 0.0), axis=0, keepdims=True)
        dn = jnp.sum(qi.astype(F32) * kin_ref[j], axis=1, keepdims=True)
        sn = jnp.sum(w * jnp.maximum(dn, 0.0), axis=0, keepdims=True)
        row = jnp.concatenate([sc, jnp.where(lane1 == 0, sn, -jnp.inf)], axis=1)
        rows = jnp.where(sub == j, row, rows)
    i_scr[pl.ds(pl.multiple_of(g * SUBLANES, SUBLANES), SUBLANES), :] = rows

    @pl.when(g == pl.num_programs(0) - 1)
    def _():
        width = i_scr.shape[1]
        for k in range(width // LANES):
            it_scr[k * LANES:(k + 1) * LANES, :] = i_scr[:, k * LANES:(k + 1) * LANES].T
        trivial = jnp.full((1, LANES), past + 1 <= top_k)
        lo, hi, jcut = (_as_column(r)[:, 0:1] for r in _select_bounds_t(it_scr, width // LANES, width, top_k, trivial))
        for k in range(width // LANES):
            sel = _selected(i_scr[:, k * LANES:(k + 1) * LANES], k * LANES, lo, hi, jcut)
            mask_ref[:, k * LANES:(k + 1) * LANES] = jnp.where(sel, 1.0, 0.0)


def _sample_index(layer, page_table, cache_kidx_t, qi3, wi3, kin3, top_k):
    n_q, n_pages = page_table.shape
    width = (n_pages + 1) * PAGE
    kern = functools.partial(_sample_index_kernel, n_pages=n_pages, top_k=top_k)

    def page_map(j, p):
        return lambda g, pt: (layer, pt[g * SUBLANES + j, p], 0, 0)

    group = lambda g, pt: (g, 0, 0)
    in_specs = [pl.BlockSpec((None, None, D_IDX, PAGE), page_map(j, p))
                for j in range(SUBLANES) for p in range(n_pages)]
    in_specs += [
        pl.BlockSpec((SUBLANES, N_IDX_HEADS, D_IDX), group),
        pl.BlockSpec((SUBLANES, N_IDX_HEADS, 1), group),
        pl.BlockSpec((SUBLANES, 1, D_IDX), group),
    ]
    return pl.pallas_call(
        kern,
        out_shape=jax.ShapeDtypeStruct((n_q, width), F32),
        grid_spec=pltpu.PrefetchScalarGridSpec(
            num_scalar_prefetch=1,
            grid=(n_q // SUBLANES,),
            in_specs=in_specs,
            out_specs=pl.BlockSpec((n_q, width), lambda b, pt: (0, 0)),
            scratch_shapes=[pltpu.VMEM((n_q, width), F32), pltpu.VMEM((width, n_q), F32)],
        ),
        compiler_params=pltpu.CompilerParams(
            dimension_semantics=("arbitrary",), vmem_limit_bytes=VMEM_LIMIT),
        name="sample_index",
    )(page_table, *([cache_kidx_t] * (SUBLANES * n_pages)), qi3, wi3, kin3)


def _sample_attend(b, k_pages, v_pages, q8_ref, qT_ref, kn_ref, vnT_ref, mask_ref, oT_scr):
    n_pages = len(k_pages)
    lane = lax.broadcasted_iota(jnp.int32, (HEAD_DIM, LANES), 1)
    qT = qT_ref[...]
    vnT = vnT_ref[...]
    s_new = jnp.sum(q8_ref[...] * kn_ref[...], axis=1, keepdims=True)
    hit_new = mask_ref[n_pages:n_pages + 1, 0:1] > 0.5
    hits = [mask_ref[p:p + 1, :] > 0.5 for p in range(n_pages)]

    for h in range(N_HEADS):
        qcol = jnp.broadcast_to(qT[:, h:h + 1], (HEAD_DIM, LANES))
        s = [jnp.where(hits[p], jnp.sum(k_pages[p][h] * qcol, axis=0, keepdims=True), MASKED)
             for p in range(n_pages)]
        sn = jnp.where(hit_new, s_new[h:h + 1, :], MASKED)
        mx = s[0]
        for p in range(1, n_pages):
            mx = jnp.maximum(mx, s[p])
        m = jnp.maximum(jnp.max(mx, axis=1, keepdims=True), sn)
        pn = jnp.exp2(sn - m)
        acc = jnp.zeros((HEAD_DIM, LANES), F32)
        lsum = jnp.zeros((1, LANES), F32)
        for p in range(n_pages):
            pp = jnp.exp2(s[p] - m)
            lsum = lsum + pp
            acc = acc + v_pages[p][h] * pp
        l = jnp.sum(lsum, axis=1, keepdims=True) + pn
        o_h = (jnp.sum(acc, axis=1, keepdims=True) + vnT[:, h:h + 1] * pn) / l
        rows = slice(h * HEAD_DIM, (h + 1) * HEAD_DIM)
        oT_scr[rows, :] = jnp.where(lane == b, o_h, oT_scr[rows, :])


def _sample_merge_kernel(att_ref, aux_ref, sp_ref, sc_ref, hs_ref, wout_ref, poolw_ref, pscale_ref,
                         cw_ref, cb_ref, cg_ref, cbeta_ref, o_ref, *, pos):
    ub = aux_ref[:, A_UB:A_GB]
    uc = aux_ref[:, A_UC:A_GC]
    run = ub
    sums, cnts = [], []
    for j in range(1, POOL_WINDOWS[-1]):
        run = run + sp_ref[POOL_STATE - j]
        if j + 1 in POOL_WINDOWS:
            sums.append(run)
            cnts.append(float(min(pos + 1, j + 1)))
    pool = _pool_mix(sums, ub, cnts, poolw_ref, pscale_ref)

    y = cw_ref[CONV_STATE:CONV_WIDTH, :] * uc
    for j in range(CONV_STATE):
        y = y + cw_ref[j:j + 1, :] * sc_ref[j]
    conv = _conv_norm(y, cb_ref, cg_ref, cbeta_ref)
    o_ref[...] = _gate_project(att_ref[...], aux_ref, pool, conv, wout_ref, hs_ref[...])


def _sample_merge(att, aux, sp_t, sc_t, hs, wout, poolw, pscale, cw, cb, cg, cbeta, pos):
    args = (att, aux, sp_t, sc_t, hs, wout, poolw, pscale, cw, cb, cg, cbeta)
    full = lambda a: pl.BlockSpec(a.shape, lambda i, nd=a.ndim: (0,) * nd)
    return pl.pallas_call(
        functools.partial(_sample_merge_kernel, pos=pos),
        out_shape=jax.ShapeDtypeStruct(hs.shape, F32),
        grid=(1,),
        in_specs=[full(a) for a in args],
        out_specs=full(hs),
        compiler_params=pltpu.CompilerParams(
            dimension_semantics=("arbitrary",), vmem_limit_bytes=VMEM_LIMIT),
        name="sample_merge",
    )(*args)


def _rope_table(pos):
    pos = pos.astype(F32)

    def cs(half):
        inv = ROPE_THETA ** (-(jnp.arange(half, dtype=F32) / half))
        ang = pos[:, None] * inv[None, :]
        return jnp.cos(ang).T, jnp.sin(ang).T

    c8, s8 = cs(HEAD_DIM // 8)
    c4, s4 = cs(D_IDX // 8)
    return jnp.concatenate([c8, s8, c4, c4, -s4, s4], axis=0)


def _permute_w_in(w):
    d = w.shape[0]
    o = 0
    parts = {}
    for name, width in (("q", D_A), ("k", D_A), ("v", D_A), ("ga", D_A), ("qi", D_QI),
                        ("wi", N_IDX_HEADS), ("ki", D_IDX), ("ub", D_B), ("gb", D_B),
                        ("cv", D_C), ("cg", D_C), ("gc", D_C)):
        parts[name] = w[:, o:o + width]
        o += width
    pad = jnp.zeros((d, LANES - D_IDX - N_IDX_HEADS), w.dtype)
    order = ["q", "k", "v", "ga", "qi", "ub", "gb", "cv", "cg", "gc", "ki", "wi"]
    return jnp.concatenate([parts[n] for n in order] + [pad], axis=1).astype(BF16)


def _block_diag(pw):
    n_g, g, _ = pw.shape
    out = jnp.zeros((n_g * g, n_g * g), pw.dtype)
    for i in range(n_g):
        out = out.at[i * g:(i + 1) * g, i * g:(i + 1) * g].set(pw[i])
    return out.astype(BF16)


def kernel(x_prompt, x_sample, cache_k, cache_v, cache_kidx, state_pool, state_conv, page_table,
           meta_tokens, norm_g, w_in, q_norm_g, k_norm_g, pool_w, pool_scale, conv_w, conv_b,
           conv_norm_g, conv_norm_b, w_out):
    n_b, seq, d_model = x_prompt.shape
    n_q = x_sample.shape[0]
    depth = w_in.shape[0]
    n_pages = page_table.shape[1]
    past = n_pages * PAGE
    t_real = seq + N_META
    t_pad = -(-t_real // TM) * TM
    top_p = min(TOPK_MAX, t_real // 4)
    top_s = min(TOPK_MAX, (past + 1) // 4)

    meta = jnp.broadcast_to(meta_tokens[None].astype(F32), (n_b, N_META, d_model))
    tail = jnp.zeros((n_b, t_pad - t_real, d_model), F32)
    hp = jnp.concatenate([meta, x_prompt, tail], axis=1).reshape(n_b * t_pad, d_model)
    hs = x_sample.reshape(n_q, d_model)

    rope_p = _rope_table(jnp.arange(t_pad))
    rope_s = _rope_table(jnp.full((n_q,), past))
    cache_k_t = jnp.transpose(cache_k, (0, 1, 3, 4, 2))
    cache_v_t = jnp.transpose(cache_v, (0, 1, 3, 4, 2))
    cache_kidx_t = jnp.transpose(cache_kidx, (0, 1, 3, 2))

    outs = {name: [] for name in ("kp", "vp", "kip", "plp", "cvp", "ks", "vs", "kis", "pls", "cvs")}
    for l in range(depth):
        w_l = _permute_w_in(w_in[l])
        g_l = norm_g[l].reshape(1, d_model)
        qg = q_norm_g[l].reshape(HEAD_DIM, 1)
        kg = k_norm_g[l].reshape(HEAD_DIM, 1)
        wout = w_out[l].astype(BF16)
        poolw = _block_diag(pool_w[l])
        pscale = pool_scale[l].reshape(1, D_B)
        cw = conv_w[l]
        cb = conv_b[l].reshape(1, D_C)
        cg = conv_norm_g[l].reshape(1, D_C)
        cbeta = conv_norm_b[l].reshape(1, D_C)

        heads_last = lambda xT: jnp.transpose(xT.reshape(xT.shape[0], N_HEADS, HEAD_DIM, -1), (0, 3, 1, 2))

        q, kTf, kT, vTf, vb, aux, qi, wk, kiTf, kiT = _project(
            hp, n_b, t_real, TM, g_l, w_l, rope_p, qg, kg)
        q_s, kTf_s, _, vTf_s, _, aux_s, qi_s, wk_s, kiTf_s, _ = _project(
            hs, 1, n_q, n_q, g_l, w_l, rope_s, qg, kg)
        qi3 = qi_s.reshape(n_q, N_IDX_HEADS, D_IDX)
        wi3 = wk_s[:, WK_WI:WK_WI + N_IDX_HEADS].reshape(n_q, N_IDX_HEADS, 1)
        kin3 = wk_s[:, WK_KI:WK_KI + D_IDX].reshape(n_q, 1, D_IDX)
        mask = _sample_index(l, page_table, cache_kidx_t, qi3, wi3, kin3, top_s)
        q8 = q_s.astype(F32).reshape(n_q, N_HEADS, HEAD_DIM)
        k_new = heads_last(kTf_s).reshape(n_q, 1, N_HEADS, HEAD_DIM)
        v_new = heads_last(vTf_s).reshape(n_q, 1, N_HEADS, HEAD_DIM)

        att, att_s = _attention(
            l, page_table, q, qi, wk, kT, kiT, vb, cache_k_t, cache_v_t, q8, jnp.swapaxes(q8, 1, 2),
            k_new[:, 0], jnp.swapaxes(v_new[:, 0], 1, 2), mask.reshape(n_q, n_pages + 1, PAGE),
            n_b, t_pad, t_real, top_p)
        hp = _prompt_merge(att, aux, hp, wout, poolw, pscale, cw, cb, cg, cbeta, n_b, t_pad)
        aux3 = aux.reshape(n_b, t_pad, N_AUX)
        outs["kp"].append(heads_last(kTf))
        outs["vp"].append(heads_last(vTf))
        outs["kip"].append(jnp.swapaxes(kiTf, 1, 2))
        outs["plp"].append(aux3[:, t_real - POOL_STATE:t_real, A_UB:A_GB])
        outs["cvp"].append(aux3[:, t_real - CONV_STATE:t_real, A_UC:A_GC])

        sp_t = jnp.swapaxes(state_pool[l], 0, 1)
        sc_t = jnp.swapaxes(state_conv[l], 0, 1)
        hs = _sample_merge(att_s, aux_s, sp_t, sc_t, hs, wout, poolw, pscale, cw, cb, cg, cbeta, past)
        outs["ks"].append(k_new)
        outs["vs"].append(v_new)
        outs["kis"].append(jnp.swapaxes(kiTf_s, 1, 2).reshape(n_q, 1, D_IDX))
        outs["pls"].append(jnp.concatenate([state_pool[l][:, 1:], aux_s[:, None, A_UB:A_GB]], axis=1))
        outs["cvs"].append(jnp.concatenate([state_conv[l][:, 1:], aux_s[:, None, A_UC:A_GC]], axis=1))

    y_prompt = hp.reshape(n_b, t_pad, d_model)[:, N_META:t_real]
    y_sample = hs.reshape(n_q, 1, d_model)
    st = lambda name: jnp.stack(outs[name], axis=0)
    return (y_prompt, y_sample, st("kp"), st("vp"), st("kip"), st("plp"), st("cvp"),
            st("ks"), st("vs"), st("kis"), st("pls"), st("cvs"))
```
